```python
import jax, jax.numpy as jnp
from jax import lax
import numpy as np

D_MODEL = 1024
BATCH = 8
SEQ = 2048
DEPTH = 2
DEC_BATCH = 128
DEC_SEQ = 8
PAST_LEN = 16384
PAGE_SIZE = 128

MIX_WIDTH = D_MODEL
POOL_WIDTH = MIX_WIDTH // 2
CONV_WIDTH = MIX_WIDTH - POOL_WIDTH
POOL_WINDOWS = (2, 4, 8, 16)
N_POOL_GROUPS = len(POOL_WINDOWS)
POOL_GC = POOL_WIDTH // N_POOL_GROUPS
POOL_BUF = max(POOL_WINDOWS) - 1
CONV_K = 3
CONV_BUF = CONV_K - 1
N_CONV_HEADS = 8
D_FF = 4 * D_MODEL
PLE_DIM = 256
IN_COLS = POOL_WIDTH + 3 * CONV_WIDTH
EPS = 1e-6

kernel_name = "hybrid_pool_shortconv_decoder_step"


def rmsnorm(x, g):
    xf = x.astype(jnp.float32)
    y = xf * lax.rsqrt(jnp.mean(xf * xf, axis=-1, keepdims=True) + EPS)
    return (y * g.astype(jnp.float32)).astype(x.dtype)


def pool_mixer(u, buf, start, pool_w, pool_scale):
    T = u.shape[1]
    ext = jnp.concatenate([buf, u], axis=1)
    extf = ext.astype(jnp.float32)
    cs = jnp.concatenate([jnp.zeros_like(extf[:, :1]), jnp.cumsum(extf, axis=1)], axis=1)
    pos = start + jnp.arange(T)
    L = POOL_BUF
    outs = []
    for g, w in enumerate(POOL_WINDOWS):
        sl = slice(g * POOL_GC, (g + 1) * POOL_GC)
        win_sum = cs[:, L + 1:L + 1 + T, sl] - cs[:, L + 1 - w:L + 1 - w + T, sl]
        cnt = jnp.minimum(w, pos + 1).astype(jnp.float32)
        d = win_sum / cnt[None, :, None] - extf[:, L:, sl]
        outs.append(jnp.einsum('btc,cd->btd', d.astype(u.dtype), pool_w[g]))
    out = jnp.concatenate(outs, axis=-1) * pool_scale
    return out, ext[:, -L:]


def short_conv(v, buf, w, b):
    T = v.shape[1]
    ext = jnp.concatenate([buf, v], axis=1)
    z = b + sum(w[k] * ext[:, k:k + T] for k in range(CONV_K))
    return z, ext[:, -CONV_BUF:]


def trunk(x, p, pool_bufs, conv_bufs, start,
          norm_mix, w_in, pool_w, pool_scale, conv_w, conv_b, w_out,
          norm_mlp, w_up, w_down, norm_ple, w_ple_gate, w_ple_proj, norm_f):
    h = x
    new_pool, new_conv = [], []
    for i in range(DEPTH):
        a = rmsnorm(h, norm_mix[i])
        proj = jnp.einsum('btd,de->bte', a, w_in[i])
        u = proj[..., :POOL_WIDTH]
        bg = proj[..., POOL_WIDTH:POOL_WIDTH + CONV_WIDTH]
        cg = proj[..., POOL_WIDTH + CONV_WIDTH:POOL_WIDTH + 2 * CONV_WIDTH]
        v = proj[..., POOL_WIDTH + 2 * CONV_WIDTH:]
        y_pool, pb = pool_mixer(u, pool_bufs[i], start, pool_w[i], pool_scale[i])
        z, cb = short_conv(cg * v, conv_bufs[i], conv_w[i], conv_b[i])
        y_conv = bg * z
        mix = jnp.concatenate([y_pool, y_conv], axis=-1)
        h = h + jnp.einsum('btm,md->btd', mix, w_out[i])
        m = rmsnorm(h, norm_mlp[i])
        f = jax.nn.relu(jnp.einsum('btd,df->btf', m, w_up[i]))
        h = h + jnp.einsum('btf,fd->btd', f * f, w_down[i])
        gate = jax.nn.sigmoid(jnp.einsum('btd,de->bte', rmsnorm(h, norm_ple[i]), w_ple_gate[i]))
        h = h + gate * jnp.einsum('btq,qd->btd', p[i], w_ple_proj[i])
        new_pool.append(pb)
        new_conv.append(cb)
    return rmsnorm(h, norm_f), jnp.stack(new_pool), jnp.stack(new_conv)


def setup_inputs(seed: int = 0) -> dict:
    key = jax.random.key(seed)
    ks = jax.random.split(key, 24)
    n = lambda k, s, sc=1.0: jax.random.normal(k, s, jnp.float32) * sc
    return {
        "x_prompt": n(ks[0], (BATCH, SEQ, D_MODEL)),
        "x_sample": n(ks[1], (DEC_BATCH, DEC_SEQ, D_MODEL)),
        "state_pool": n(ks[2], (DEPTH, DEC_BATCH, POOL_BUF, POOL_WIDTH)),
        "state_conv": n(ks[3], (DEPTH, DEC_BATCH, CONV_BUF, CONV_WIDTH)),
        "p_prompt": n(ks[4], (DEPTH, BATCH, SEQ, PLE_DIM)),
        "p_sample": n(ks[5], (DEPTH, DEC_BATCH, DEC_SEQ, PLE_DIM)),
        "norm_mix": 1.0 + n(ks[6], (DEPTH, D_MODEL), 0.05),
        "w_in": n(ks[7], (DEPTH, D_MODEL, IN_COLS), D_MODEL ** -0.5),
        "pool_w": n(ks[8], (DEPTH, N_POOL_GROUPS, POOL_GC, POOL_GC), POOL_GC ** -0.5),
        "pool_scale": 1.0 + n(ks[9], (DEPTH, POOL_WIDTH), 0.1),
        "conv_w": n(ks[10], (DEPTH, CONV_K, CONV_WIDTH), CONV_K ** -0.5),
        "conv_b": n(ks[11], (DEPTH, CONV_WIDTH), 0.02),
        "w_out": n(ks[12], (DEPTH, MIX_WIDTH, D_MODEL), MIX_WIDTH ** -0.5),
        "norm_mlp": 1.0 + n(ks[13], (DEPTH, D_MODEL), 0.05),
        "w_up": n(ks[14], (DEPTH, D_MODEL, D_FF), D_MODEL ** -0.5),
        "w_down": n(ks[15], (DEPTH, D_FF, D_MODEL), D_FF ** -0.5),
        "norm_ple": 1.0 + n(ks[16], (DEPTH, D_MODEL), 0.05),
        "w_ple_gate": n(ks[17], (DEPTH, D_MODEL, D_MODEL), D_MODEL ** -0.5),
        "w_ple_proj": n(ks[18], (DEPTH, PLE_DIM, D_MODEL), PLE_DIM ** -0.5),
        "norm_f": 1.0 + n(ks[19], (D_MODEL,), 0.05),
    }


def reference(x_prompt, x_sample, state_pool, state_conv, p_prompt, p_sample,
              norm_mix, w_in, pool_w, pool_scale, conv_w, conv_b, w_out,
              norm_mlp, w_up, w_down, norm_ple, w_ple_gate, w_ple_proj, norm_f):
    weights = (norm_mix, w_in, pool_w, pool_scale, conv_w, conv_b, w_out,
               norm_mlp, w_up, w_down, norm_ple, w_ple_gate, w_ple_proj, norm_f)
    pool0 = jnp.zeros((DEPTH, x_prompt.shape[0], POOL_BUF, POOL_WIDTH), x_prompt.dtype)
    conv0 = jnp.zeros((DEPTH, x_prompt.shape[0], CONV_BUF, CONV_WIDTH), x_prompt.dtype)
    y_prompt, new_pool_prompt, new_conv_prompt = trunk(x_prompt, p_prompt, pool0, conv0, 0, *weights)
    y_sample, new_pool_sample, new_conv_sample = trunk(x_sample, p_sample, state_pool, state_conv, PAST_LEN, *weights)
    return (y_prompt, y_sample, new_pool_prompt, new_conv_prompt, new_pool_sample, new_conv_sample)
```

```python
import functools

import jax
import jax.numpy as jnp
from jax import lax
from jax.experimental import pallas as pl
from jax.experimental.pallas import tpu as pltpu

D_MODEL = 1024
DEPTH = 2
PAST_LEN = 16384
POOL_WIDTH = 512
CONV_WIDTH = 512
POOL_WINDOWS = (2, 4, 8, 16)
POOL_GC = 128
POOL_BUF = 15
CONV_K = 3
CONV_BUF = 2
D_FF = 4 * D_MODEL
PLE_DIM = 256
IN_COLS = POOL_WIDTH + 3 * CONV_WIDTH
EPS = 1e-6

SUBLANES = 8
MXU_DIM = 256
POOL_HALO = POOL_BUF + 1
TOKEN_TILE = 512
FF_CHUNK = 1024
SAMPLE_GROUPS = 2
VMEM_LIMIT_BYTES = 56 * 1024 * 1024

_BF16 = jnp.bfloat16
_F32 = jnp.float32


def _rmsnorm(x, g):
    ms = jnp.mean(x * x, axis=-1, keepdims=True)
    return (x * lax.rsqrt(ms + EPS)) * g


def _dot(a, b):
    return jnp.dot(a, b, preferred_element_type=_F32)


def _conv_halo(stride):
    return -(-CONV_BUF * stride // SUBLANES) * SUBLANES


def _layer_kernel(x_ref, p_ref, pool0_ref, conv0_ref,
                  nmix_ref, w_in_ref, pw_ref, pscale_ref, cw_ref, cb_ref, w_out_ref,
                  nmlp_ref, w_up_ref, w_down_ref, nple_ref, w_gate_ref, w_proj_ref, nf_ref,
                  out_ref, npool_ref, nconv_ref, pool_ext, conv_ext,
                  *, stride, tile, n_tiles, start, final_norm):
    i = pl.program_id(1)
    halo = POOL_HALO * stride
    chalo = _conv_halo(stride)

    @pl.when(i == 0)
    def _():
        pool_ext[0:halo, :] = pool0_ref[...]
        conv_ext[0:chalo, :] = conv0_ref[...]

    h = x_ref[...]
    a = _rmsnorm(h, nmix_ref[...]).astype(_BF16)
    proj = _dot(a, w_in_ref[...])
    u = proj[:, :POOL_WIDTH]
    bg = proj[:, POOL_WIDTH:POOL_WIDTH + CONV_WIDTH]
    cg = proj[:, POOL_WIDTH + CONV_WIDTH:POOL_WIDTH + 2 * CONV_WIDTH]
    v = proj[:, POOL_WIDTH + 2 * CONV_WIDTH:]
    pool_ext[halo:halo + tile, :] = u
    conv_ext[chalo:chalo + tile, :] = cg * v

    if start < POOL_BUF:
        rows = lax.broadcasted_iota(jnp.int32, (tile, 1), 0)
        pos = start + i * (tile // stride) + rows // stride
    ds = []
    for g, w in enumerate(POOL_WINDOWS):
        sl = slice(g * POOL_GC, (g + 1) * POOL_GC)
        e = pool_ext[:, sl]
        s = e
        span = 1
        while span < w:
            s = s + pltpu.roll(s, span * stride, axis=0)
            span *= 2
        win = s[halo:, :]
        if start < POOL_BUF:
            cnt = jnp.minimum(w, pos + 1).astype(_F32)
            mean = win / cnt
        else:
            mean = win * (1.0 / w)
        ds.append((mean - e[halo:, :]).astype(_BF16))
    y_pool = jnp.concatenate(
        [_dot(jnp.concatenate(ds[2 * k:2 * k + 2], axis=-1), pw_ref[k]) for k in range(2)],
        axis=-1) * pscale_ref[...]

    cvx = conv_ext[...]
    cw = cw_ref[...]
    z = cb_ref[...] + (cw[0:1, :] * pltpu.roll(cvx, 2 * stride, axis=0)[chalo:, :]
                       + cw[1:2, :] * pltpu.roll(cvx, stride, axis=0)[chalo:, :]
                       + cw[2:3, :] * cvx[chalo:, :])
    y_conv = bg * z

    mix = jnp.concatenate([y_pool, y_conv], axis=-1).astype(_BF16)
    h = h + _dot(mix, w_out_ref[...])

    m = _rmsnorm(h, nmlp_ref[...]).astype(_BF16)
    acc = h
    for c in range(D_FF // FF_CHUNK):
        cs = slice(c * FF_CHUNK, (c + 1) * FF_CHUNK)
        f = jnp.maximum(_dot(m, w_up_ref[:, cs]), 0.0)
        acc = acc + _dot((f * f).astype(_BF16), w_down_ref[cs, :])
    h = acc

    gate = jax.nn.sigmoid(_dot(_rmsnorm(h, nple_ref[...]).astype(_BF16), w_gate_ref[...]))
    h = h + gate * _dot(p_ref[...].astype(_BF16), w_proj_ref[...])

    if final_norm:
        h = _rmsnorm(h, nf_ref[...])
    out_ref[...] = h

    @pl.when(i == n_tiles - 1)
    def _():
        npool_ref[...] = pool_ext[tile + stride:tile + halo, :]
        nconv_ref[...] = conv_ext[tile + chalo - CONV_BUF * stride:tile + chalo, :]

    if n_tiles > 1:
        @pl.when(i < n_tiles - 1)
        def _():
            pool_ext[0:halo, :] = pool_ext[tile:tile + halo, :]
            conv_ext[0:chalo, :] = conv_ext[tile:tile + chalo, :]


def _layer_call(layer, x, p, pool0, conv0, weights, nf, *, stride, tile, start, final_norm):
    groups, rows, _ = x.shape
    n_tiles = rows // tile
    assert rows == n_tiles * tile and tile % stride == 0
    halo = POOL_HALO * stride
    chalo = _conv_halo(stride)
    assert n_tiles == 1 or tile >= halo

    def resident(arr):
        nd = arr.ndim - 1
        return pl.BlockSpec((None,) + arr.shape[1:], lambda b, i: (layer,) + (0,) * nd,
                            pipeline_mode=pl.Buffered(1))

    in_specs = [
        pl.BlockSpec((None, tile, D_MODEL), lambda b, i: (b, i, 0)),
        pl.BlockSpec((None, None, tile, PLE_DIM), lambda b, i: (layer, b, i, 0)),
        pl.BlockSpec((None, halo, POOL_WIDTH), lambda b, i: (b, 0, 0)),
        pl.BlockSpec((None, chalo, CONV_WIDTH), lambda b, i: (b, 0, 0)),
    ] + [resident(w) for w in weights] + [
        pl.BlockSpec((1, D_MODEL), lambda b, i: (0, 0), pipeline_mode=pl.Buffered(1)),
    ]
    out_specs = [
        pl.BlockSpec((None, tile, D_MODEL), lambda b, i: (b, i, 0)),
        pl.BlockSpec((None, POOL_BUF * stride, POOL_WIDTH), lambda b, i: (b, 0, 0)),
        pl.BlockSpec((None, CONV_BUF * stride, CONV_WIDTH), lambda b, i: (b, 0, 0)),
    ]
    out_shape = [
        jax.ShapeDtypeStruct((groups, rows, D_MODEL), _F32),
        jax.ShapeDtypeStruct((groups, POOL_BUF * stride, POOL_WIDTH), _F32),
        jax.ShapeDtypeStruct((groups, CONV_BUF * stride, CONV_WIDTH), _F32),
    ]
    body = functools.partial(_layer_kernel, stride=stride, tile=tile, n_tiles=n_tiles,
                             start=start, final_norm=final_norm)
    return pl.pallas_call(
        body,
        grid=(groups, n_tiles),
        in_specs=in_specs,
        out_specs=out_specs,
        out_shape=out_shape,
        scratch_shapes=[pltpu.VMEM((halo + tile, POOL_WIDTH), _F32),
                        pltpu.VMEM((chalo + tile, CONV_WIDTH), _F32)],
        compiler_params=pltpu.CompilerParams(
            dimension_semantics=("arbitrary", "arbitrary"),
            vmem_limit_bytes=VMEM_LIMIT_BYTES),
        name=f"trunk_layer{layer}_stride{stride}",
    )(x, p, pool0, conv0, *weights, nf)


def _trunk(x, p, pool0, conv0, weights, nf, *, stride, tile, start):
    h = x
    pools, convs = [], []
    for layer in range(DEPTH):
        h, pb, cb = _layer_call(layer, h, p, pool0[layer], conv0[layer], weights, nf,
                                stride=stride, tile=tile, start=start,
                                final_norm=(layer == DEPTH - 1))
        pools.append(pb)
        convs.append(cb)
    return h, jnp.stack(pools), jnp.stack(convs)


def _to_position_major(a, groups):
    *lead, b, t, c = a.shape
    a = a.reshape(*lead, groups, b // groups, t, c)
    a = jnp.swapaxes(a, -3, -2)
    return a.reshape(*lead, groups, t * (b // groups), c)


def _from_position_major(a, batch):
    *lead, groups, rows, c = a.shape
    per = batch // groups
    a = a.reshape(*lead, groups, rows // per, per, c)
    a = jnp.swapaxes(a, -3, -2)
    return a.reshape(*lead, batch, rows // per, c)


def _front_pad(a, rows):
    pad = [(0, 0)] * a.ndim
    pad[-2] = (rows - a.shape[-2], 0)
    return jnp.pad(a, pad)


def kernel(x_prompt, x_sample, state_pool, state_conv, p_prompt, p_sample, norm_mix, w_in, pool_w, pool_scale, conv_w, conv_b, w_out, norm_mlp, w_up, w_down, norm_ple, w_ple_gate, w_ple_proj, norm_f):
    pw = pool_w.reshape(DEPTH, 2, 2, POOL_GC, POOL_GC)
    zero = jnp.zeros_like(pw[:, :, 0])
    pw_bd = jnp.concatenate(
        [jnp.concatenate([pw[:, :, 0], zero], axis=-1),
         jnp.concatenate([zero, pw[:, :, 1]], axis=-1)], axis=-2).astype(_BF16)
    row = lambda a: a.reshape(DEPTH, 1, a.shape[-1])
    weights = (row(norm_mix), w_in.astype(_BF16), pw_bd, row(pool_scale), conv_w, row(conv_b),
               w_out.astype(_BF16), row(norm_mlp), w_up.astype(_BF16), w_down.astype(_BF16),
               row(norm_ple), w_ple_gate.astype(_BF16), w_ple_proj.astype(_BF16))
    nf = norm_f.reshape(1, D_MODEL)

    batch = x_prompt.shape[0]
    pool0 = jnp.zeros((DEPTH, batch, POOL_HALO, POOL_WIDTH), _F32)
    conv0 = jnp.zeros((DEPTH, batch, _conv_halo(1), CONV_WIDTH), _F32)
    y_prompt, pool_p, conv_p = _trunk(x_prompt, p_prompt, pool0, conv0, weights, nf,
                                      stride=1, tile=TOKEN_TILE, start=0)

    dec_batch, dec_seq, _ = x_sample.shape
    stride = dec_batch // SAMPLE_GROUPS
    xs = _to_position_major(x_sample, SAMPLE_GROUPS)
    ps = _to_position_major(p_sample, SAMPLE_GROUPS)
    pool_s0 = _front_pad(_to_position_major(state_pool, SAMPLE_GROUPS), POOL_HALO * stride)
    conv_s0 = _front_pad(_to_position_major(state_conv, SAMPLE_GROUPS), _conv_halo(stride))
    ys, pool_s, conv_s = _trunk(xs, ps, pool_s0, conv_s0, weights, nf,
                                stride=stride, tile=dec_seq * stride, start=PAST_LEN)
    y_sample = _from_position_major(ys, dec_batch)
    pool_s = _from_position_major(pool_s, dec_batch)
    conv_s = _from_position_major(conv_s, dec_batch)
    return (y_prompt, y_sample, pool_p, conv_p, pool_s, conv_s)
```

```python
import functools

import jax
import jax.numpy as jnp
from jax import lax
from jax.experimental import pallas as pl
from jax.experimental.pallas import tpu as pltpu

D_MODEL = 1024
DEPTH = 2
PAST_LEN = 16384
POOL_WIDTH = 512
CONV_WIDTH = 512
POOL_WINDOWS = (2, 4, 8, 16)
POOL_GC = 128
POOL_BUF = 15
CONV_K = 3
CONV_BUF = 2
D_FF = 4 * D_MODEL
PLE_DIM = 256
IN_COLS = POOL_WIDTH + 3 * CONV_WIDTH
EPS = 1e-6

SUBLANES = 8
POOL_HALO = POOL_BUF + 1
TOKEN_TILE = 512
FF_CHUNK = 1024
VMEM_LIMIT_BYTES = 56 * 1024 * 1024

_BF16 = jnp.bfloat16
_F32 = jnp.float32


def _rmsnorm(x, g):
    ms = jnp.mean(x * x, axis=-1, keepdims=True)
    return (x * lax.rsqrt(ms + EPS)) * g


def _dot(a, b):
    return jnp.dot(a, b, preferred_element_type=_F32)


def _in_proj(h, nmix_ref, w_in_ref):
    proj = _dot(_rmsnorm(h, nmix_ref[...]).astype(_BF16), w_in_ref[...])
    u = proj[:, :POOL_WIDTH]
    bg = proj[:, POOL_WIDTH:POOL_WIDTH + CONV_WIDTH]
    cg = proj[:, POOL_WIDTH + CONV_WIDTH:POOL_WIDTH + 2 * CONV_WIDTH]
    v = proj[:, POOL_WIDTH + 2 * CONV_WIDTH:]
    return u, bg, cg * v


def _pool_out(ds, pw_ref, pscale_ref):
    return jnp.concatenate(
        [_dot(jnp.concatenate(ds[2 * k:2 * k + 2], axis=-1), pw_ref[k]) for k in range(2)],
        axis=-1) * pscale_ref[...]


def _conv_out(cb_ref, cw_ref, back2, back1, cur):
    cw = cw_ref[...]
    return cb_ref[...] + (cw[0:1, :] * back2 + cw[1:2, :] * back1 + cw[2:3, :] * cur)


def _after_mix(h, y_pool, y_conv, p_ref, w_out_ref, nmlp_ref, w_up_ref, w_down_ref,
               nple_ref, w_gate_ref, w_proj_ref, nf_ref, final_norm):
    mix = jnp.concatenate([y_pool, y_conv], axis=-1).astype(_BF16)
    h = h + _dot(mix, w_out_ref[...])

    m = _rmsnorm(h, nmlp_ref[...]).astype(_BF16)
    acc = h
    for c in range(D_FF // FF_CHUNK):
        cs = slice(c * FF_CHUNK, (c + 1) * FF_CHUNK)
        f = jnp.maximum(_dot(m, w_up_ref[:, cs]), 0.0)
        acc = acc + _dot((f * f).astype(_BF16), w_down_ref[cs, :])
    h = acc

    gate = jax.nn.sigmoid(_dot(_rmsnorm(h, nple_ref[...]).astype(_BF16), w_gate_ref[...]))
    h = h + gate * _dot(p_ref[...].astype(_BF16), w_proj_ref[...])
    if final_norm:
        h = _rmsnorm(h, nf_ref[...])
    return h


def _prompt_kernel(x_ref, p_ref,
                   nmix_ref, w_in_ref, pw_ref, pscale_ref, cw_ref, cb_ref, w_out_ref,
                   nmlp_ref, w_up_ref, w_down_ref, nple_ref, w_gate_ref, w_proj_ref, nf_ref,
                   out_ref, npool_ref, nconv_ref, pool_ext, conv_ext,
                   *, tile, n_tiles, final_norm):
    i = pl.program_id(1)
    halo = POOL_HALO
    chalo = SUBLANES

    @pl.when(i == 0)
    def _():
        pool_ext[0:halo, :] = jnp.zeros((halo, POOL_WIDTH), _F32)
        conv_ext[0:chalo, :] = jnp.zeros((chalo, CONV_WIDTH), _F32)

    h = x_ref[...]
    u, bg, cv = _in_proj(h, nmix_ref, w_in_ref)
    pool_ext[halo:halo + tile, :] = u
    conv_ext[chalo:chalo + tile, :] = cv

    pos = i * tile + lax.broadcasted_iota(jnp.int32, (tile, 1), 0)
    ds = []
    for g, w in enumerate(POOL_WINDOWS):
        e = pool_ext[:, g * POOL_GC:(g + 1) * POOL_GC]
        s = e
        span = 1
        while span < w:
            s = s + pltpu.roll(s, span, axis=0)
            span *= 2
        cnt = jnp.minimum(w, pos + 1).astype(_F32)
        ds.append((s[halo:, :] / cnt - e[halo:, :]).astype(_BF16))
    y_pool = _pool_out(ds, pw_ref, pscale_ref)

    cvx = conv_ext[...]
    z = _conv_out(cb_ref, cw_ref, pltpu.roll(cvx, 2, axis=0)[chalo:, :],
                  pltpu.roll(cvx, 1, axis=0)[chalo:, :], cvx[chalo:, :])

    out_ref[...] = _after_mix(h, y_pool, bg * z, p_ref, w_out_ref, nmlp_ref, w_up_ref, w_down_ref,
                              nple_ref, w_gate_ref, w_proj_ref, nf_ref, final_norm)

    @pl.when(i == n_tiles - 1)
    def _():
        npool_ref[...] = pool_ext[tile:tile + halo, :]
        nconv_ref[...] = conv_ext[tile:tile + chalo, :]

    @pl.when(i < n_tiles - 1)
    def _():
        pool_ext[0:halo, :] = pool_ext[tile:tile + halo, :]
        conv_ext[0:chalo, :] = conv_ext[tile:tile + chalo, :]


def _sample_kernel(x_ref, p_ref, pool_a_ref, pool_b_ref, conv_b_ref,
                   nmix_ref, w_in_ref, pw_ref, pscale_ref, cw_ref, cb_ref, w_out_ref,
                   nmlp_ref, w_up_ref, w_down_ref, nple_ref, w_gate_ref, w_proj_ref, nf_ref,
                   out_ref, u_ref, cv_ref, *, tile, final_norm):
    h = x_ref[...]
    u, bg, cv = _in_proj(h, nmix_ref, w_in_ref)
    u_ref[...] = u
    cv_ref[...] = cv

    t = lax.broadcasted_iota(jnp.int32, (tile, 1), 0) % SUBLANES

    def back(j, cur, prev):
        if j == SUBLANES:
            return prev
        return jnp.where(t >= j, pltpu.roll(cur, j, axis=0), pltpu.roll(prev, tile + j - SUBLANES, axis=0))

    ds = []
    for g, w in enumerate(POOL_WINDOWS):
        sl = slice(g * POOL_GC, (g + 1) * POOL_GC)
        e = u[:, sl]
        groups = [pool_a_ref[:, :, sl].reshape(tile, POOL_GC),
                  pool_b_ref[:, :, sl].reshape(tile, POOL_GC), e]
        span = 1
        while span < w:
            groups = [cur + back(span, cur, prev) for prev, cur in zip([groups[0]] + groups[:-1], groups)]
            span *= 2
        ds.append((groups[-1] * (1.0 / w) - e).astype(_BF16))
    y_pool = _pool_out(ds, pw_ref, pscale_ref)

    cb = conv_b_ref[...]
    z = _conv_out(cb_ref, cw_ref, back(2, cv, cb), back(1, cv, cb), cv)

    out_ref[...] = _after_mix(h, y_pool, bg * z, p_ref, w_out_ref, nmlp_ref, w_up_ref, w_down_ref,
                              nple_ref, w_gate_ref, w_proj_ref, nf_ref, final_norm)


def _weight_specs(layer, weights):
    def resident(arr):
        nd = arr.ndim - 1
        return pl.BlockSpec((None,) + arr.shape[1:], lambda *_: (layer,) + (0,) * nd,
                            pipeline_mode=pl.Buffered(1))
    return [resident(w) for w in weights] + [
        pl.BlockSpec((1, D_MODEL), lambda *_: (0, 0), pipeline_mode=pl.Buffered(1))]


def _prompt_layer(layer, x, p, weights, nf, *, final_norm):
    batch, seq, _ = x.shape
    tile = TOKEN_TILE
    n_tiles = seq // tile
    assert seq == n_tiles * tile and tile >= POOL_HALO
    in_specs = [
        pl.BlockSpec((None, tile, D_MODEL), lambda b, i: (b, i, 0)),
        pl.BlockSpec((None, None, tile, PLE_DIM), lambda b, i: (layer, b, i, 0)),
    ] + _weight_specs(layer, weights)
    out_specs = [
        pl.BlockSpec((None, tile, D_MODEL), lambda b, i: (b, i, 0)),
        pl.BlockSpec((None, POOL_HALO, POOL_WIDTH), lambda b, i: (b, 0, 0)),
        pl.BlockSpec((None, SUBLANES, CONV_WIDTH), lambda b, i: (b, 0, 0)),
    ]
    out_shape = [
        jax.ShapeDtypeStruct((batch, seq, D_MODEL), _F32),
        jax.ShapeDtypeStruct((batch, POOL_HALO, POOL_WIDTH), _F32),
        jax.ShapeDtypeStruct((batch, SUBLANES, CONV_WIDTH), _F32),
    ]
    return pl.pallas_call(
        functools.partial(_prompt_kernel, tile=tile, n_tiles=n_tiles, final_norm=final_norm),
        grid=(batch, n_tiles),
        in_specs=in_specs,
        out_specs=out_specs,
        out_shape=out_shape,
        scratch_shapes=[pltpu.VMEM((POOL_HALO + tile, POOL_WIDTH), _F32),
                        pltpu.VMEM((SUBLANES + tile, CONV_WIDTH), _F32)],
        compiler_params=pltpu.CompilerParams(
            dimension_semantics=("arbitrary", "arbitrary"),
            vmem_limit_bytes=VMEM_LIMIT_BYTES),
        name=f"prompt_layer{layer}",
    )(x, p, *weights, nf)


def _sample_layer(layer, x, p, pool_hist, conv_hist, weights, nf, *, final_norm):
    rows, _ = x.shape
    tile = TOKEN_TILE
    seqs = tile // SUBLANES
    n_tiles = rows // tile
    assert rows == n_tiles * tile
    in_specs = [
        pl.BlockSpec((tile, D_MODEL), lambda i: (i, 0)),
        pl.BlockSpec((None, tile, PLE_DIM), lambda i: (layer, i, 0)),
        pl.BlockSpec((None, seqs, None, SUBLANES, POOL_WIDTH), lambda i: (layer, i, 0, 0, 0)),
        pl.BlockSpec((None, seqs, None, SUBLANES, POOL_WIDTH), lambda i: (layer, i, 1, 0, 0)),
        pl.BlockSpec((None, tile, CONV_WIDTH), lambda i: (layer, i, 0)),
    ] + _weight_specs(layer, weights)
    out_specs = [
        pl.BlockSpec((tile, D_MODEL), lambda i: (i, 0)),
        pl.BlockSpec((tile, POOL_WIDTH), lambda i: (i, 0)),
        pl.BlockSpec((tile, CONV_WIDTH), lambda i: (i, 0)),
    ]
    out_shape = [
        jax.ShapeDtypeStruct((rows, D_MODEL), _F32),
        jax.ShapeDtypeStruct((rows, POOL_WIDTH), _F32),
        jax.ShapeDtypeStruct((rows, CONV_WIDTH), _F32),
    ]
    return pl.pallas_call(
        functools.partial(_sample_kernel, tile=tile, final_norm=final_norm),
        grid=(n_tiles,),
        in_specs=in_specs,
        out_specs=out_specs,
        out_shape=out_shape,
        compiler_params=pltpu.CompilerParams(
            dimension_semantics=("arbitrary",),
            vmem_limit_bytes=VMEM_LIMIT_BYTES),
        name=f"sample_layer{layer}",
    )(x, p, pool_hist, pool_hist, conv_hist, *weights, nf)


def _front_pad(a, rows):
    pad = [(0, 0)] * a.ndim
    pad[-2] = (rows - a.shape[-2], 0)
    return jnp.pad(a, pad)


def kernel(x_prompt, x_sample, state_pool, state_conv, p_prompt, p_sample, norm_mix, w_in, pool_w, pool_scale, conv_w, conv_b, w_out, norm_mlp, w_up, w_down, norm_ple, w_ple_gate, w_ple_proj, norm_f):
    pw = pool_w.reshape(DEPTH, 2, 2, POOL_GC, POOL_GC)
    zero = jnp.zeros_like(pw[:, :, 0])
    pw_bd = jnp.concatenate(
        [jnp.concatenate([pw[:, :, 0], zero], axis=-1),
         jnp.concatenate([zero, pw[:, :, 1]], axis=-1)], axis=-2).astype(_BF16)
    row = lambda a: a.reshape(DEPTH, 1, a.shape[-1])
    weights = (row(norm_mix), w_in.astype(_BF16), pw_bd, row(pool_scale), conv_w, row(conv_b),
               w_out.astype(_BF16), row(norm_mlp), w_up.astype(_BF16), w_down.astype(_BF16),
               row(norm_ple), w_ple_gate.astype(_BF16), w_ple_proj.astype(_BF16))
    nf = norm_f.reshape(1, D_MODEL)

    h = x_prompt
    pools, convs = [], []
    for layer in range(DEPTH):
        h, pb, cb = _prompt_layer(layer, h, p_prompt, weights, nf, final_norm=(layer == DEPTH - 1))
        pools.append(pb[:, POOL_HALO - POOL_BUF:, :])
        convs.append(cb[:, SUBLANES - CONV_BUF:, :])
    y_prompt, pool_p, conv_p = h, jnp.stack(pools), jnp.stack(convs)

    dec_batch, dec_seq, _ = x_sample.shape
    assert dec_seq == SUBLANES
    rows = dec_batch * dec_seq
    pool_hist = _front_pad(state_pool, 2 * SUBLANES).reshape(DEPTH, dec_batch, 2, SUBLANES, POOL_WIDTH)
    conv_hist = _front_pad(state_conv, SUBLANES).reshape(DEPTH, rows, CONV_WIDTH)
    h = x_sample.reshape(rows, D_MODEL)
    ps = p_sample.reshape(DEPTH, rows, PLE_DIM)
    us, cvs = [], []
    for layer in range(DEPTH):
        h, u, cv = _sample_layer(layer, h, ps, pool_hist, conv_hist, weights, nf,
                                 final_norm=(layer == DEPTH - 1))
        us.append(u.reshape(dec_batch, dec_seq, POOL_WIDTH))
        cvs.append(cv.reshape(dec_batch, dec_seq, CONV_WIDTH))
    y_sample = h.reshape(dec_batch, dec_seq, D_MODEL)
    pool_s = jnp.concatenate([state_pool, jnp.stack(us)], axis=2)[:, :, -POOL_BUF:, :]
    conv_s = jnp.stack(cvs)[:, :, -CONV_BUF:, :]
    return (y_prompt, y_sample, pool_p, conv_p, pool_s, conv_s)
```

```python
import functools

import jax
import jax.numpy as jnp
from jax import lax
from jax.experimental import pallas as pl
from jax.experimental.pallas import tpu as pltpu

D_MODEL = 1024
DEPTH = 2
PAST_LEN = 16384
POOL_WIDTH = 512
CONV_WIDTH = 512
POOL_WINDOWS = (2, 4, 8, 16)
POOL_GC = 128
POOL_BUF = 15
CONV_K = 3
CONV_BUF = 2
D_FF = 4 * D_MODEL
PLE_DIM = 256
IN_COLS = POOL_WIDTH + 3 * CONV_WIDTH
EPS = 1e-6

SUBLANES = 8
BF16_ROWS = 16
POOL_HALO = POOL_BUF + 1
TOKEN_TILE = 512
FF_CHUNK = 1024
VMEM_LIMIT_BYTES = 56 * 1024 * 1024

_BF16 = jnp.bfloat16
_F32 = jnp.float32


def _rmsnorm(x, g):
    ms = jnp.mean(x * x, axis=-1, keepdims=True)
    return (x * lax.rsqrt(ms + EPS)) * g


def _dot(a, b):
    return jnp.dot(a, b, preferred_element_type=_F32)


def _in_proj(h, nmix_ref, w_in_ref):
    proj = _dot(_rmsnorm(h, nmix_ref[...]).astype(_BF16), w_in_ref[...])
    u = proj[:, :POOL_WIDTH]
    bg = proj[:, POOL_WIDTH:POOL_WIDTH + CONV_WIDTH]
    cg = proj[:, POOL_WIDTH + CONV_WIDTH:POOL_WIDTH + 2 * CONV_WIDTH]
    v = proj[:, POOL_WIDTH + 2 * CONV_WIDTH:]
    return u, bg, cg * v


def _pool_out(ds, pw_ref, pscale_ref):
    return jnp.concatenate(
        [_dot(jnp.concatenate(ds[2 * k:2 * k + 2], axis=-1), pw_ref[k]) for k in range(2)],
        axis=-1) * pscale_ref[...]


def _conv_out(cb_ref, cw_ref, back2, back1, cur):
    cw = cw_ref[...]
    return cb_ref[...] + (cw[0:1, :] * back2 + cw[1:2, :] * back1 + cw[2:3, :] * cur)


def _after_mix(h, y_pool, y_conv, p_ref, w_out_ref, nmlp_ref, w_up_ref, w_down_ref,
               nple_ref, w_gate_ref, w_proj_ref, nf_ref, final_norm):
    mix = jnp.concatenate([y_pool, y_conv], axis=-1).astype(_BF16)
    h = h + _dot(mix, w_out_ref[...])

    m = _rmsnorm(h, nmlp_ref[...]).astype(_BF16)
    acc = h
    for c in range(D_FF // FF_CHUNK):
        cs = slice(c * FF_CHUNK, (c + 1) * FF_CHUNK)
        f = jnp.maximum(_dot(m, w_up_ref[:, cs]), 0.0)
        acc = acc + _dot((f * f).astype(_BF16), w_down_ref[cs, :])
    h = acc

    gate = jax.nn.sigmoid(_dot(_rmsnorm(h, nple_ref[...]).astype(_BF16), w_gate_ref[...]))
    h = h + gate * _dot(p_ref[...].astype(_BF16), w_proj_ref[...])
    if final_norm:
        h = _rmsnorm(h, nf_ref[...])
    return h


def _prompt_kernel(x_ref, p_ref,
                   nmix_ref, w_in_ref, pw_ref, pscale_ref, cw_ref, cb_ref, w_out_ref,
                   nmlp_ref, w_up_ref, w_down_ref, nple_ref, w_gate_ref, w_proj_ref, nf_ref,
                   *rest, tile, n_tiles, final_norm, n_cast):
    cast_in = rest[:n_cast]
    out_ref, npool_ref, nconv_ref = rest[n_cast:n_cast + 3]
    cast_out = rest[n_cast + 3:2 * n_cast + 3]
    pool_ext, conv_ext = rest[2 * n_cast + 3:]
    for src, dst in zip(cast_in, cast_out):
        dst[...] = src[...].astype(_BF16)

    i = pl.program_id(1)
    halo = POOL_HALO
    chalo = SUBLANES

    @pl.when(i == 0)
    def _():
        pool_ext[0:halo, :] = jnp.zeros((halo, POOL_WIDTH), _F32)
        conv_ext[0:chalo, :] = jnp.zeros((chalo, CONV_WIDTH), _F32)

    h = x_ref[...]
    u, bg, cv = _in_proj(h, nmix_ref, w_in_ref)
    pool_ext[halo:halo + tile, :] = u
    conv_ext[chalo:chalo + tile, :] = cv

    pos = i * tile + lax.broadcasted_iota(jnp.int32, (tile, 1), 0)
    ds = []
    for g, w in enumerate(POOL_WINDOWS):
        e = pool_ext[:, g * POOL_GC:(g + 1) * POOL_GC]
        s = e
        span = 1
        while span < w:
            s = s + pltpu.roll(s, span, axis=0)
            span *= 2
        cnt = jnp.minimum(w, pos + 1).astype(_F32)
        ds.append((s[halo:, :] / cnt - e[halo:, :]).astype(_BF16))
    y_pool = _pool_out(ds, pw_ref, pscale_ref)

    cvx = conv_ext[...]
    z = _conv_out(cb_ref, cw_ref, pltpu.roll(cvx, 2, axis=0)[chalo:, :],
                  pltpu.roll(cvx, 1, axis=0)[chalo:, :], cvx[chalo:, :])

    out_ref[...] = _after_mix(h, y_pool, bg * z, p_ref, w_out_ref, nmlp_ref, w_up_ref, w_down_ref,
                              nple_ref, w_gate_ref, w_proj_ref, nf_ref, final_norm)

    @pl.when(i == n_tiles - 1)
    def _():
        npool_ref[...] = pool_ext[tile:tile + halo, :]
        nconv_ref[...] = conv_ext[tile:tile + chalo, :]

    @pl.when(i < n_tiles - 1)
    def _():
        pool_ext[0:halo, :] = pool_ext[tile:tile + halo, :]
        conv_ext[0:chalo, :] = conv_ext[tile:tile + chalo, :]


def _sample_kernel(x_ref, p_ref, pool_a_ref, pool_b_ref, conv_b_ref,
                   nmix_ref, w_in_ref, pw_ref, pscale_ref, cw_ref, cb_ref, w_out_ref,
                   nmlp_ref, w_up_ref, w_down_ref, nple_ref, w_gate_ref, w_proj_ref, nf_ref,
                   out_ref, u_ref, cv_ref, *, tile, final_norm):
    h = x_ref[...]
    u, bg, cv = _in_proj(h, nmix_ref, w_in_ref)
    u_ref[...] = u
    cv_ref[...] = cv

    t = lax.broadcasted_iota(jnp.int32, (tile, 1), 0) % SUBLANES

    def back(j, cur, prev):
        if j == SUBLANES:
            return prev
        return jnp.where(t >= j, pltpu.roll(cur, j, axis=0), pltpu.roll(prev, tile + j - SUBLANES, axis=0))

    ds = []
    for g, w in enumerate(POOL_WINDOWS):
        sl = slice(g * POOL_GC, (g + 1) * POOL_GC)
        e = u[:, sl]
        groups = [pool_a_ref[:, :, sl].reshape(tile, POOL_GC),
                  pool_b_ref[:, :, sl].reshape(tile, POOL_GC), e]
        span = 1
        while span < w:
            groups = [cur + back(span, cur, prev) for prev, cur in zip([groups[0]] + groups[:-1], groups)]
            span *= 2
        ds.append((groups[-1] * (1.0 / w) - e).astype(_BF16))
    y_pool = _pool_out(ds, pw_ref, pscale_ref)

    cb = conv_b_ref[...]
    z = _conv_out(cb_ref, cw_ref, back(2, cv, cb), back(1, cv, cb), cv)

    out_ref[...] = _after_mix(h, y_pool, bg * z, p_ref, w_out_ref, nmlp_ref, w_up_ref, w_down_ref,
                              nple_ref, w_gate_ref, w_proj_ref, nf_ref, final_norm)


def _weight_specs(weights):
    return [pl.BlockSpec(w.shape, lambda *_, nd=w.ndim: (0,) * nd, pipeline_mode=pl.Buffered(1))
            for w in weights]


def _prompt_layer(layer, x, p, weights, *, final_norm, cast_next=()):
    batch, seq, _ = x.shape
    tile = TOKEN_TILE
    n_tiles = seq // tile
    steps = batch * n_tiles
    assert seq == n_tiles * tile and tile >= POOL_HALO
    for w in cast_next:
        assert w.shape[1] % (steps * BF16_ROWS) == 0
    in_specs = [
        pl.BlockSpec((None, tile, D_MODEL), lambda b, i: (b, i, 0)),
        pl.BlockSpec((None, None, tile, PLE_DIM), lambda b, i: (layer, b, i, 0)),
    ] + _weight_specs(weights) + [
        pl.BlockSpec((None, w.shape[1] // steps, w.shape[2]),
                     lambda b, i: (layer + 1, b * n_tiles + i, 0)) for w in cast_next]
    out_specs = [
        pl.BlockSpec((None, tile, D_MODEL), lambda b, i: (b, i, 0)),
        pl.BlockSpec((None, POOL_HALO, POOL_WIDTH), lambda b, i: (b, 0, 0)),
        pl.BlockSpec((None, SUBLANES, CONV_WIDTH), lambda b, i: (b, 0, 0)),
    ] + [pl.BlockSpec((w.shape[1] // steps, w.shape[2]), lambda b, i: (b * n_tiles + i, 0))
         for w in cast_next]
    out_shape = [
        jax.ShapeDtypeStruct((batch, seq, D_MODEL), _F32),
        jax.ShapeDtypeStruct((batch, POOL_HALO, POOL_WIDTH), _F32),
        jax.ShapeDtypeStruct((batch, SUBLANES, CONV_WIDTH), _F32),
    ] + [jax.ShapeDtypeStruct(w.shape[1:], _BF16) for w in cast_next]
    return pl.pallas_call(
        functools.partial(_prompt_kernel, tile=tile, n_tiles=n_tiles, final_norm=final_norm,
                          n_cast=len(cast_next)),
        grid=(batch, n_tiles),
        in_specs=in_specs,
        out_specs=out_specs,
        out_shape=out_shape,
        scratch_shapes=[pltpu.VMEM((POOL_HALO + tile, POOL_WIDTH), _F32),
                        pltpu.VMEM((SUBLANES + tile, CONV_WIDTH), _F32)],
        compiler_params=pltpu.CompilerParams(
            dimension_semantics=("arbitrary", "arbitrary"),
            vmem_limit_bytes=VMEM_LIMIT_BYTES),
        name=f"prompt_layer{layer}",
    )(x, p, *weights, *cast_next)


def _sample_layer(layer, x, p, pool_hist, conv_hist, weights, *, final_norm):
    rows, _ = x.shape
    tile = TOKEN_TILE
    seqs = tile // SUBLANES
    n_tiles = rows // tile
    assert rows == n_tiles * tile
    in_specs = [
        pl.BlockSpec((tile, D_MODEL), lambda i: (i, 0)),
        pl.BlockSpec((None, tile, PLE_DIM), lambda i: (layer, i, 0)),
        pl.BlockSpec((None, seqs, None, SUBLANES, POOL_WIDTH), lambda i: (layer, i, 0, 0, 0)),
        pl.BlockSpec((None, seqs, None, SUBLANES, POOL_WIDTH), lambda i: (layer, i, 1, 0, 0)),
        pl.BlockSpec((None, tile, CONV_WIDTH), lambda i: (layer, i, 0)),
    ] + _weight_specs(weights)
    out_specs = [
        pl.BlockSpec((tile, D_MODEL), lambda i: (i, 0)),
        pl.BlockSpec((tile, POOL_WIDTH), lambda i: (i, 0)),
        pl.BlockSpec((tile, CONV_WIDTH), lambda i: (i, 0)),
    ]
    out_shape = [
        jax.ShapeDtypeStruct((rows, D_MODEL), _F32),
        jax.ShapeDtypeStruct((rows, POOL_WIDTH), _F32),
        jax.ShapeDtypeStruct((rows, CONV_WIDTH), _F32),
    ]
    return pl.pallas_call(
        functools.partial(_sample_kernel, tile=tile, final_norm=final_norm),
        grid=(n_tiles,),
        in_specs=in_specs,
        out_specs=out_specs,
        out_shape=out_shape,
        compiler_params=pltpu.CompilerParams(
            dimension_semantics=("arbitrary",),
            vmem_limit_bytes=VMEM_LIMIT_BYTES),
        name=f"sample_layer{layer}",
    )(x, p, pool_hist, pool_hist, conv_hist, *weights)


def _front_pad(a, rows):
    pad = [(0, 0)] * a.ndim
    pad[-2] = (rows - a.shape[-2], 0)
    return jnp.pad(a, pad)


def kernel(x_prompt, x_sample, state_pool, state_conv, p_prompt, p_sample, norm_mix, w_in, pool_w, pool_scale, conv_w, conv_b, w_out, norm_mlp, w_up, w_down, norm_ple, w_ple_gate, w_ple_proj, norm_f):
    pw = pool_w.reshape(DEPTH, 2, 2, POOL_GC, POOL_GC)
    zero = jnp.zeros_like(pw[:, :, 0])
    pw_bd = jnp.concatenate(
        [jnp.concatenate([pw[:, :, 0], zero], axis=-1),
         jnp.concatenate([zero, pw[:, :, 1]], axis=-1)], axis=-2).astype(_BF16)
    nf = norm_f.reshape(1, D_MODEL)

    def layer_weights(layer, big):
        row = lambda a: a[layer].reshape(1, -1)
        return (row(norm_mix), big[0], pw_bd[layer], row(pool_scale), conv_w[layer], row(conv_b),
                big[1], row(norm_mlp), big[2], big[3], row(norm_ple), big[4],
                w_ple_proj[layer].astype(_BF16), nf)

    big_f32 = (w_in, w_out, w_up, w_down, w_ple_gate)
    weights = [layer_weights(0, tuple(w[0].astype(_BF16) for w in big_f32))]

    h = x_prompt
    pools, convs = [], []
    for layer in range(DEPTH):
        last = layer == DEPTH - 1
        h, pb, cb, *big_next = _prompt_layer(
            layer, h, p_prompt, weights[layer], final_norm=last,
            cast_next=() if last else big_f32)
        if not last:
            weights.append(layer_weights(layer + 1, big_next))
        pools.append(pb[:, POOL_HALO - POOL_BUF:, :])
        convs.append(cb[:, SUBLANES - CONV_BUF:, :])
    y_prompt, pool_p, conv_p = h, jnp.stack(pools), jnp.stack(convs)

    dec_batch, dec_seq, _ = x_sample.shape
    assert dec_seq == SUBLANES
    rows = dec_batch * dec_seq
    pool_hist = _front_pad(state_pool, 2 * SUBLANES).reshape(DEPTH, dec_batch, 2, SUBLANES, POOL_WIDTH)
    conv_hist = _front_pad(state_conv, SUBLANES).reshape(DEPTH, rows, CONV_WIDTH)
    h = x_sample.reshape(rows, D_MODEL)
    ps = p_sample.reshape(DEPTH, rows, PLE_DIM)
    us, cvs = [], []
    for layer in range(DEPTH):
        h, u, cv = _sample_layer(layer, h, ps, pool_hist, conv_hist, weights[layer],
                                 final_norm=(layer == DEPTH - 1))
        us.append(u.reshape(dec_batch, dec_seq, POOL_WIDTH))
        cvs.append(cv.reshape(dec_batch, dec_seq, CONV_WIDTH))
    y_sample = h.reshape(dec_batch, dec_seq, D_MODEL)
    pool_s = jnp.concatenate([state_pool, jnp.stack(us)], axis=2)[:, :, -POOL_BUF:, :]
    conv_s = jnp.stack(cvs)[:, :, -CONV_BUF:, :]
    return (y_prompt, y_sample, pool_p, conv_p, pool_s, conv_s)
```

```python
import functools

import jax
import jax.numpy as jnp
from jax import lax
from jax.experimental import pallas as pl
from jax.experimental.pallas import tpu as pltpu

D_MODEL = 1024
DEPTH = 2
PAST_LEN = 16384
POOL_WIDTH = 512
CONV_WIDTH = 512
POOL_WINDOWS = (2, 4, 8, 16)
POOL_GC = 128
POOL_BUF = 15
CONV_K = 3
CONV_BUF = 2
D_FF = 4 * D_MODEL
PLE_DIM = 256
IN_COLS = POOL_WIDTH + 3 * CONV_WIDTH
EPS = 1e-6

SUBLANES = 8
BF16_ROWS = 16
POOL_HALO = POOL_BUF + 1
TOKEN_TILE = 512
FF_CHUNK = 1024
VMEM_LIMIT_BYTES = 56 * 1024 * 1024

_BF16 = jnp.bfloat16
_F32 = jnp.float32


def _rmsnorm(x, g):
    ms = jnp.mean(x * x, axis=-1, keepdims=True)
    return (x * lax.rsqrt(ms + EPS)) * g


def _dot(a, b):
    return jnp.dot(a, b, preferred_element_type=_F32)


def _norm_split(x, g):
    scale = lax.rsqrt(jnp.mean(x * x, axis=-1, keepdims=True) + EPS)
    return (x * g).astype(_BF16), scale


def _in_proj(h, nmix_ref, w_in_ref):
    hg, scale = _norm_split(h, nmix_ref[...])
    proj = _dot(hg, w_in_ref[...])
    u = proj[:, :POOL_WIDTH] * scale
    bg = proj[:, POOL_WIDTH:POOL_WIDTH + CONV_WIDTH] * scale
    cg = proj[:, POOL_WIDTH + CONV_WIDTH:POOL_WIDTH + 2 * CONV_WIDTH] * scale
    v = proj[:, POOL_WIDTH + 2 * CONV_WIDTH:] * scale
    return u, bg, cg * v


def _pool_out(ds, pw_ref, pscale_ref):
    return jnp.concatenate(
        [_dot(jnp.concatenate(ds[2 * k:2 * k + 2], axis=-1), pw_ref[k]) for k in range(2)],
        axis=-1) * pscale_ref[...]


def _conv_out(cb_ref, cw_ref, back2, back1, cur):
    cw = cw_ref[...]
    return cb_ref[...] + (cw[0:1, :] * back2 + cw[1:2, :] * back1 + cw[2:3, :] * cur)


def _after_mix(h, y_pool, y_conv, p_ref, w_out_ref, nmlp_ref, w_up_ref, w_down_ref,
               nple_ref, w_gate_ref, w_proj_ref, nf_ref, final_norm):
    mix = jnp.concatenate([y_pool, y_conv], axis=-1).astype(_BF16)
    h = h + _dot(mix, w_out_ref[...])

    hg, scale = _norm_split(h, nmlp_ref[...])
    acc = h
    for c in range(D_FF // FF_CHUNK):
        cs = slice(c * FF_CHUNK, (c + 1) * FF_CHUNK)
        f = jnp.maximum(_dot(hg, w_up_ref[:, cs]) * scale, 0.0)
        acc = acc + _dot((f * f).astype(_BF16), w_down_ref[cs, :])
    h = acc

    hg, scale = _norm_split(h, nple_ref[...])
    gate = jax.nn.sigmoid(_dot(hg, w_gate_ref[...]) * scale)
    h = h + gate * _dot(p_ref[...].astype(_BF16), w_proj_ref[...])
    if final_norm:
        h = _rmsnorm(h, nf_ref[...])
    return h


def _prompt_kernel(x_ref, p_ref,
                   nmix_ref, w_in_ref, pw_ref, pscale_ref, cw_ref, cb_ref, w_out_ref,
                   nmlp_ref, w_up_ref, w_down_ref, nple_ref, w_gate_ref, w_proj_ref, nf_ref,
                   *rest, tile, n_tiles, final_norm, n_cast):
    cast_in = rest[:n_cast]
    out_ref, npool_ref, nconv_ref = rest[n_cast:n_cast + 3]
    cast_out = rest[n_cast + 3:2 * n_cast + 3]
    pool_ext, conv_ext = rest[2 * n_cast + 3:]
    i = pl.program_id(1)
    halo = POOL_HALO
    chalo = SUBLANES

    @pl.when(i == 0)
    def _():
        pool_ext[0:halo, :] = jnp.zeros((halo, POOL_WIDTH), _F32)
        conv_ext[0:chalo, :] = jnp.zeros((chalo, CONV_WIDTH), _F32)

    h = x_ref[...]
    u, bg, cv = _in_proj(h, nmix_ref, w_in_ref)
    pool_ext[halo:halo + tile, :] = u
    conv_ext[chalo:chalo + tile, :] = cv

    for src, dst in zip(cast_in, cast_out):
        dst[...] = src[...].astype(_BF16)

    pos = i * tile + lax.broadcasted_iota(jnp.int32, (tile, 1), 0)
    ds = []
    for g, w in enumerate(POOL_WINDOWS):
        e = pool_ext[:, g * POOL_GC:(g + 1) * POOL_GC]
        s = e
        span = 1
        while span < w:
            s = s + pltpu.roll(s, span, axis=0)
            span *= 2
        cnt = jnp.minimum(w, pos + 1).astype(_F32)
        ds.append((s[halo:, :] / cnt - e[halo:, :]).astype(_BF16))
    y_pool = _pool_out(ds, pw_ref, pscale_ref)

    cvx = conv_ext[...]
    z = _conv_out(cb_ref, cw_ref, pltpu.roll(cvx, 2, axis=0)[chalo:, :],
                  pltpu.roll(cvx, 1, axis=0)[chalo:, :], cvx[chalo:, :])

    out_ref[...] = _after_mix(h, y_pool, bg * z, p_ref, w_out_ref, nmlp_ref, w_up_ref, w_down_ref,
                              nple_ref, w_gate_ref, w_proj_ref, nf_ref, final_norm)

    @pl.when(i == n_tiles - 1)
    def _():
        npool_ref[...] = pool_ext[tile:tile + halo, :]
        nconv_ref[...] = conv_ext[tile:tile + chalo, :]

    @pl.when(i < n_tiles - 1)
    def _():
        pool_ext[0:halo, :] = pool_ext[tile:tile + halo, :]
        conv_ext[0:chalo, :] = conv_ext[tile:tile + chalo, :]


def _sample_kernel(x_ref, p_ref, pool_a_ref, pool_b_ref, conv_b_ref,
                   nmix_ref, w_in_ref, pw_ref, pscale_ref, cw_ref, cb_ref, w_out_ref,
                   nmlp_ref, w_up_ref, w_down_ref, nple_ref, w_gate_ref, w_proj_ref, nf_ref,
                   out_ref, u_ref, cv_ref, *, tile, final_norm):
    h = x_ref[...]
    u, bg, cv = _in_proj(h, nmix_ref, w_in_ref)
    u_ref[...] = u
    cv_ref[...] = cv

    t = lax.broadcasted_iota(jnp.int32, (tile, 1), 0) % SUBLANES

    def back(j, cur, prev):
        if j == SUBLANES:
            return prev
        return jnp.where(t >= j, pltpu.roll(cur, j, axis=0), pltpu.roll(prev, tile + j - SUBLANES, axis=0))

    ds = []
    for g, w in enumerate(POOL_WINDOWS):
        sl = slice(g * POOL_GC, (g + 1) * POOL_GC)
        e = u[:, sl]
        groups = [pool_a_ref[:, :, sl].reshape(tile, POOL_GC),
                  pool_b_ref[:, :, sl].reshape(tile, POOL_GC), e]
        span = 1
        while span < w:
            groups = [cur + back(span, cur, prev) for prev, cur in zip([groups[0]] + groups[:-1], groups)]
            span *= 2
        ds.append((groups[-1] * (1.0 / w) - e).astype(_BF16))
    y_pool = _pool_out(ds, pw_ref, pscale_ref)

    cb = conv_b_ref[...]
    z = _conv_out(cb_ref, cw_ref, back(2, cv, cb), back(1, cv, cb), cv)

    out_ref[...] = _after_mix(h, y_pool, bg * z, p_ref, w_out_ref, nmlp_ref, w_up_ref, w_down_ref,
                              nple_ref, w_gate_ref, w_proj_ref, nf_ref, final_norm)


def _weight_specs(weights):
    return [pl.BlockSpec(w.shape, lambda *_, nd=w.ndim: (0,) * nd, pipeline_mode=pl.Buffered(1))
            for w in weights]


def _prompt_layer(layer, x, p, weights, *, final_norm, cast_next=()):
    batch, seq, _ = x.shape
    tile = TOKEN_TILE
    n_tiles = seq // tile
    steps = batch * n_tiles
    assert seq == n_tiles * tile and tile >= POOL_HALO
    for w in cast_next:
        assert w.shape[1] % (steps * BF16_ROWS) == 0
    in_specs = [
        pl.BlockSpec((None, tile, D_MODEL), lambda b, i: (b, i, 0)),
        pl.BlockSpec((None, None, tile, PLE_DIM), lambda b, i: (layer, b, i, 0)),
    ] + _weight_specs(weights) + [
        pl.BlockSpec((None, w.shape[1] // steps, w.shape[2]),
                     lambda b, i: (layer + 1, b * n_tiles + i, 0)) for w in cast_next]
    out_specs = [
        pl.BlockSpec((None, tile, D_MODEL), lambda b, i: (b, i, 0)),
        pl.BlockSpec((None, POOL_HALO, POOL_WIDTH), lambda b, i: (b, 0, 0)),
        pl.BlockSpec((None, SUBLANES, CONV_WIDTH), lambda b, i: (b, 0, 0)),
    ] + [pl.BlockSpec((w.shape[1] // steps, w.shape[2]), lambda b, i: (b * n_tiles + i, 0))
         for w in cast_next]
    out_shape = [
        jax.ShapeDtypeStruct((batch, seq, D_MODEL), _F32),
        jax.ShapeDtypeStruct((batch, POOL_HALO, POOL_WIDTH), _F32),
        jax.ShapeDtypeStruct((batch, SUBLANES, CONV_WIDTH), _F32),
    ] + [jax.ShapeDtypeStruct(w.shape[1:], _BF16) for w in cast_next]
    return pl.pallas_call(
        functools.partial(_prompt_kernel, tile=tile, n_tiles=n_tiles, final_norm=final_norm,
                          n_cast=len(cast_next)),
        grid=(batch, n_tiles),
        in_specs=in_specs,
        out_specs=out_specs,
        out_shape=out_shape,
        scratch_shapes=[pltpu.VMEM((POOL_HALO + tile, POOL_WIDTH), _F32),
                        pltpu.VMEM((SUBLANES + tile, CONV_WIDTH), _F32)],
        compiler_params=pltpu.CompilerParams(
            dimension_semantics=("arbitrary", "arbitrary"),
            vmem_limit_bytes=VMEM_LIMIT_BYTES),
        name=f"prompt_layer{layer}",
    )(x, p, *weights, *cast_next)


def _sample_layer(layer, x, p, pool_hist, conv_hist, weights, *, final_norm):
    rows, _ = x.shape
    tile = TOKEN_TILE
    seqs = tile // SUBLANES
    n_tiles = rows // tile
    assert rows == n_tiles * tile
    in_specs = [
        pl.BlockSpec((tile, D_MODEL), lambda i: (i, 0)),
        pl.BlockSpec((None, tile, PLE_DIM), lambda i: (layer, i, 0)),
        pl.BlockSpec((None, seqs, None, SUBLANES, POOL_WIDTH), lambda i: (layer, i, 0, 0, 0)),
        pl.BlockSpec((None, seqs, None, SUBLANES, POOL_WIDTH), lambda i: (layer, i, 1, 0, 0)),
        pl.BlockSpec((None, tile, CONV_WIDTH), lambda i: (layer, i, 0)),
    ] + _weight_specs(weights)
    out_specs = [
        pl.BlockSpec((tile, D_MODEL), lambda i: (i, 0)),
        pl.BlockSpec((tile, POOL_WIDTH), lambda i: (i, 0)),
        pl.BlockSpec((tile, CONV_WIDTH), lambda i: (i, 0)),
    ]
    out_shape = [
        jax.ShapeDtypeStruct((rows, D_MODEL), _F32),
        jax.ShapeDtypeStruct((rows, POOL_WIDTH), _F32),
        jax.ShapeDtypeStruct((rows, CONV_WIDTH), _F32),
    ]
    return pl.pallas_call(
        functools.partial(_sample_kernel, tile=tile, final_norm=final_norm),
        grid=(n_tiles,),
        in_specs=in_specs,
        out_specs=out_specs,
        out_shape=out_shape,
        compiler_params=pltpu.CompilerParams(
            dimension_semantics=("arbitrary",),
            vmem_limit_bytes=VMEM_LIMIT_BYTES),
        name=f"sample_layer{layer}",
    )(x, p, pool_hist, pool_hist, conv_hist, *weights)


def _front_pad(a, rows):
    pad = [(0, 0)] * a.ndim
    pad[-2] = (rows - a.shape[-2], 0)
    return jnp.pad(a, pad)


def kernel(x_prompt, x_sample, state_pool, state_conv, p_prompt, p_sample, norm_mix, w_in, pool_w, pool_scale, conv_w, conv_b, w_out, norm_mlp, w_up, w_down, norm_ple, w_ple_gate, w_ple_proj, norm_f):
    pw = pool_w.reshape(DEPTH, 2, 2, POOL_GC, POOL_GC)
    zero = jnp.zeros_like(pw[:, :, 0])
    pw_bd = jnp.concatenate(
        [jnp.concatenate([pw[:, :, 0], zero], axis=-1),
         jnp.concatenate([zero, pw[:, :, 1]], axis=-1)], axis=-2).astype(_BF16)
    nf = norm_f.reshape(1, D_MODEL)

    def layer_weights(layer, big):
        row = lambda a: a[layer].reshape(1, -1)
        return (row(norm_mix), big[0], pw_bd[layer], row(pool_scale), conv_w[layer], row(conv_b),
                big[1], row(norm_mlp), big[2], big[3], row(norm_ple), big[4],
                w_ple_proj[layer].astype(_BF16), nf)

    big_f32 = (w_in, w_out, w_up, w_down, w_ple_gate)
    weights = [layer_weights(0, tuple(w[0].astype(_BF16) for w in big_f32))]

    h = x_prompt
    pools, convs = [], []
    for layer in range(DEPTH):
        last = layer == DEPTH - 1
        h, pb, cb, *big_next = _prompt_layer(
            layer, h, p_prompt, weights[layer], final_norm=last,
            cast_next=() if last else big_f32)
        if not last:
            weights.append(layer_weights(layer + 1, big_next))
        pools.append(pb[:, POOL_HALO - POOL_BUF:, :])
        convs.append(cb[:, SUBLANES - CONV_BUF:, :])
    y_prompt, pool_p, conv_p = h, jnp.stack(pools), jnp.stack(convs)

    dec_batch, dec_seq, _ = x_sample.shape
    assert dec_seq == SUBLANES
    rows = dec_batch * dec_seq
    pool_hist = _front_pad(state_pool, 2 * SUBLANES).reshape(DEPTH, dec_batch, 2, SUBLANES, POOL_WIDTH)
    conv_hist = _front_pad(state_conv, SUBLANES).reshape(DEPTH, rows, CONV_WIDTH)
    h = x_sample.reshape(rows, D_MODEL)
    ps = p_sample.reshape(DEPTH, rows, PLE_DIM)
    us, cvs = [], []
    for layer in range(DEPTH):
        h, u, cv = _sample_layer(layer, h, ps, pool_hist, conv_hist, weights[layer],
                                 final_norm=(layer == DEPTH - 1))
        us.append(u.reshape(dec_batch, dec_seq, POOL_WIDTH))
        cvs.append(cv.reshape(dec_batch, dec_seq, CONV_WIDTH))
    y_sample = h.reshape(dec_batch, dec_seq, D_MODEL)
    pool_s = jnp.concatenate([state_pool, jnp.stack(us)], axis=2)[:, :, -POOL_BUF:, :]
    conv_s = jnp.stack(cvs)[:, :, -CONV_BUF:, :]
    return (y_prompt, y_sample, pool_p, conv_p, pool_s, conv_s)
```

```python
import functools
from typing import NamedTuple

import jax
import jax.numpy as jnp
from jax import lax
from jax.experimental import pallas as pl
from jax.experimental.pallas import tpu as pltpu

D_MODEL = 1024
DEPTH = 2
PAST_LEN = 16384
POOL_WIDTH = 512
CONV_WIDTH = 512
POOL_WINDOWS = (2, 4, 8, 16)
POOL_GC = 128
POOL_BUF = 15
CONV_K = 3
CONV_BUF = 2
D_FF = 4 * D_MODEL
PLE_DIM = 256
IN_COLS = POOL_WIDTH + 3 * CONV_WIDTH
EPS = 1e-6

SUBLANES = 8
BF16_ROWS = 16
POOL_HALO = POOL_BUF + 1
TOKEN_TILE = 512
FF_CHUNK = 1024
VMEM_LIMIT_BYTES = 56 * 1024 * 1024

_BF16 = jnp.bfloat16
_F32 = jnp.float32


class _Weights(NamedTuple):
    norm_mix: jax.Array
    pool_scale: jax.Array
    conv_w: jax.Array
    conv_b: jax.Array
    norm_mlp: jax.Array
    norm_ple: jax.Array
    norm_f: jax.Array
    pool_w: jax.Array
    w_proj: jax.Array
    w_in: jax.Array
    w_out: jax.Array
    w_up: jax.Array
    w_down: jax.Array
    w_gate: jax.Array


def _rmsnorm(x, g):
    ms = jnp.mean(x * x, axis=-1, keepdims=True)
    return (x * lax.rsqrt(ms + EPS)) * g


def _dot(a, b):
    return jnp.dot(a, b, preferred_element_type=_F32)


def _norm_split(x, g):
    scale = lax.rsqrt(jnp.mean(x * x, axis=-1, keepdims=True) + EPS)
    return (x * g).astype(_BF16), scale


def _in_proj(h, w, layer):
    hg, scale = _norm_split(h, w.norm_mix[layer:layer + 1, :])
    proj = _dot(hg, w.w_in[...])
    u = proj[:, :POOL_WIDTH] * scale
    bg = proj[:, POOL_WIDTH:POOL_WIDTH + CONV_WIDTH] * scale
    cg = proj[:, POOL_WIDTH + CONV_WIDTH:POOL_WIDTH + 2 * CONV_WIDTH] * scale
    v = proj[:, POOL_WIDTH + 2 * CONV_WIDTH:] * scale
    return u, bg, cg * v


def _pool_out(ds, w, layer):
    return jnp.concatenate(
        [_dot(jnp.concatenate(ds[2 * k:2 * k + 2], axis=-1), w.pool_w[layer, k]) for k in range(2)],
        axis=-1) * w.pool_scale[layer:layer + 1, :]


def _conv_out(w, layer, back2, back1, cur):
    cw = w.conv_w[layer]
    return w.conv_b[layer:layer + 1, :] + (cw[0:1, :] * back2 + cw[1:2, :] * back1 + cw[2:3, :] * cur)


def _after_mix(h, y_pool, y_conv, p_ref, w, layer):
    mix = jnp.concatenate([y_pool, y_conv], axis=-1).astype(_BF16)
    h = h + _dot(mix, w.w_out[...])

    hg, scale = _norm_split(h, w.norm_mlp[layer:layer + 1, :])
    acc = h
    for c in range(D_FF // FF_CHUNK):
        cs = slice(c * FF_CHUNK, (c + 1) * FF_CHUNK)
        f = jnp.maximum(_dot(hg, w.w_up[:, cs]) * scale, 0.0)
        acc = acc + _dot((f * f).astype(_BF16), w.w_down[cs, :])
    h = acc

    hg, scale = _norm_split(h, w.norm_ple[layer:layer + 1, :])
    gate = jax.nn.sigmoid(_dot(hg, w.w_gate[...]) * scale)
    h = h + gate * _dot(p_ref[...].astype(_BF16), w.w_proj[layer])
    if layer == DEPTH - 1:
        h = _rmsnorm(h, w.norm_f[...])
    return h


def _prompt_kernel(*refs, layer, tile, n_tiles, n_cast):
    n_w = len(_Weights._fields)
    x_ref, p_ref = refs[:2]
    w = _Weights(*refs[2:2 + n_w])
    pos = 2 + n_w
    prev = refs[pos:pos + 2 * layer]
    pos += 2 * layer
    cast_in = refs[pos:pos + n_cast]
    pos += n_cast
    out_ref, npool_ref, nconv_ref = refs[pos:pos + 3]
    cast_out = refs[pos + 3:pos + 3 + n_cast]
    pool_ext, conv_ext = refs[pos + 3 + n_cast:]
    i = pl.program_id(1)
    halo = POOL_HALO
    chalo = SUBLANES

    @pl.when(i == 0)
    def _():
        pool_ext[0:halo, :] = jnp.zeros((halo, POOL_WIDTH), _F32)
        conv_ext[0:chalo, :] = jnp.zeros((chalo, CONV_WIDTH), _F32)

    h = x_ref[...]
    u, bg, cv = _in_proj(h, w, layer)
    pool_ext[halo:halo + tile, :] = u
    conv_ext[chalo:chalo + tile, :] = cv

    for src, dst in zip(cast_in, cast_out):
        dst[...] = src[...].astype(_BF16)

    position = i * tile + lax.broadcasted_iota(jnp.int32, (tile, 1), 0)
    ds = []
    for g, win in enumerate(POOL_WINDOWS):
        e = pool_ext[:, g * POOL_GC:(g + 1) * POOL_GC]
        s = e
        span = 1
        while span < win:
            s = s + pltpu.roll(s, span, axis=0)
            span *= 2
        cnt = jnp.minimum(win, position + 1).astype(_F32)
        ds.append((s[halo:, :] / cnt - e[halo:, :]).astype(_BF16))
    y_pool = _pool_out(ds, w, layer)

    cvx = conv_ext[...]
    z = _conv_out(w, layer, pltpu.roll(cvx, 2, axis=0)[chalo:, :],
                  pltpu.roll(cvx, 1, axis=0)[chalo:, :], cvx[chalo:, :])

    out_ref[...] = _after_mix(h, y_pool, bg * z, p_ref, w, layer)

    @pl.when(i == n_tiles - 1)
    def _():
        new_pool = pool_ext[tile + halo - POOL_BUF:tile + halo, :]
        new_conv = conv_ext[tile + chalo - CONV_BUF:tile + chalo, :]
        if layer == 0:
            npool_ref[...] = new_pool
            nconv_ref[...] = new_conv
        else:
            for k in range(layer):
                npool_ref[k] = prev[2 * k][...]
                nconv_ref[k] = prev[2 * k + 1][...]
            npool_ref[layer] = new_pool
            nconv_ref[layer] = new_conv

    @pl.when(i < n_tiles - 1)
    def _():
        pool_ext[0:halo, :] = pool_ext[tile:tile + halo, :]
        conv_ext[0:chalo, :] = conv_ext[tile:tile + chalo, :]


def _sample_kernel(*refs, layer, tile):
    n_w = len(_Weights._fields)
    x_ref, p_ref, pool_hist, conv_hist = refs[:4]
    w = _Weights(*refs[4:4 + n_w])
    prev = refs[4 + n_w:4 + n_w + 2 * layer]
    out_ref, npool_ref, nconv_ref = refs[4 + n_w + 2 * layer:]
    seqs = tile // SUBLANES

    h = x_ref[...]
    u, bg, cv = _in_proj(h, w, layer)

    t = lax.broadcasted_iota(jnp.int32, (tile, 1), 0) % SUBLANES
    flat = lambda a: a.reshape(tile, a.shape[-1])
    per_seq = lambda a: jnp.broadcast_to(a, (seqs, SUBLANES, a.shape[-1])).reshape(tile, a.shape[-1])

    def back(j, cur, prev_group):
        if j == SUBLANES:
            return prev_group
        return jnp.where(t >= j, pltpu.roll(cur, j, axis=0),
                         pltpu.roll(prev_group, tile + j - SUBLANES, axis=0))

    ds = []
    for g, win in enumerate(POOL_WINDOWS):
        sl = slice(g * POOL_GC, (g + 1) * POOL_GC)
        e = u[:, sl]
        oldest = pltpu.roll(flat(pool_hist[:, 0:SUBLANES, sl]), 1, axis=0)
        older = flat(pool_hist[:, POOL_BUF - SUBLANES:POOL_BUF, sl])
        groups = [oldest, older, e]
        span = 1
        while span < win:
            groups = [cur + back(span, cur, prv) for prv, cur in zip([groups[0]] + groups[:-1], groups)]
            span *= 2
        ds.append((groups[-1] * (1.0 / win) - e).astype(_BF16))
    y_pool = _pool_out(ds, w, layer)

    last1 = per_seq(conv_hist[:, 1:2, :])
    last2 = per_seq(conv_hist[:, 0:1, :])
    back1 = jnp.where(t >= 1, pltpu.roll(cv, 1, axis=0), last1)
    back2 = jnp.where(t >= 2, pltpu.roll(cv, 2, axis=0), jnp.where(t == 1, last1, last2))
    z = _conv_out(w, layer, back2, back1, cv)

    out_ref[...] = _after_mix(h, y_pool, bg * z, p_ref, w, layer)

    keep = POOL_BUF - SUBLANES
    pool_dst = npool_ref if layer == 0 else npool_ref.at[layer]
    conv_dst = nconv_ref if layer == 0 else nconv_ref.at[layer]
    pool_dst[:, 0:keep, :] = pool_hist[:, SUBLANES:POOL_BUF, :]
    pool_dst[:, keep:POOL_BUF, :] = u.reshape(seqs, SUBLANES, POOL_WIDTH)
    conv_dst[...] = cv.reshape(seqs, SUBLANES, CONV_WIDTH)[:, SUBLANES - CONV_BUF:, :]
    for k in range(layer):
        npool_ref[k] = prev[2 * k][...]
        nconv_ref[k] = prev[2 * k + 1][...]


def _weight_specs(weights):
    return [pl.BlockSpec(a.shape, lambda *_, nd=a.ndim: (0,) * nd, pipeline_mode=pl.Buffered(1))
            for a in weights]


def _prompt_layer(layer, x, p, weights, prev_state, cast_next):
    batch, seq, _ = x.shape
    tile = TOKEN_TILE
    n_tiles = seq // tile
    steps = batch * n_tiles
    assert seq == n_tiles * tile and tile >= POOL_HALO and len(prev_state) == 2 * layer
    for a in cast_next:
        assert a.shape[1] % (steps * BF16_ROWS) == 0
    per_batch = lambda rows, width: pl.BlockSpec((None, rows, width), lambda b, i: (b, 0, 0))
    stacked = lambda rows, width: pl.BlockSpec((layer + 1, None, rows, width), lambda b, i: (0, b, 0, 0))
    in_specs = [
        pl.BlockSpec((None, tile, D_MODEL), lambda b, i: (b, i, 0)),
        pl.BlockSpec((None, None, tile, PLE_DIM), lambda b, i: (layer, b, i, 0)),
    ] + _weight_specs(weights) + [
        per_batch(POOL_BUF, POOL_WIDTH), per_batch(CONV_BUF, CONV_WIDTH)] * layer + [
        pl.BlockSpec((None, a.shape[1] // steps, a.shape[2]),
                     lambda b, i: (layer + 1, b * n_tiles + i, 0)) for a in cast_next]
    out_specs = [pl.BlockSpec((None, tile, D_MODEL), lambda b, i: (b, i, 0))]
    out_shape = [jax.ShapeDtypeStruct((batch, seq, D_MODEL), _F32)]
    if layer == 0:
        out_specs += [per_batch(POOL_BUF, POOL_WIDTH), per_batch(CONV_BUF, CONV_WIDTH)]
        out_shape += [jax.ShapeDtypeStruct((batch, POOL_BUF, POOL_WIDTH), _F32),
                      jax.ShapeDtypeStruct((batch, CONV_BUF, CONV_WIDTH), _F32)]
    else:
        out_specs += [stacked(POOL_BUF, POOL_WIDTH), stacked(CONV_BUF, CONV_WIDTH)]
        out_shape += [jax.ShapeDtypeStruct((layer + 1, batch, POOL_BUF, POOL_WIDTH), _F32),
                      jax.ShapeDtypeStruct((layer + 1, batch, CONV_BUF, CONV_WIDTH), _F32)]
    out_specs += [pl.BlockSpec((a.shape[1] // steps, a.shape[2]), lambda b, i: (b * n_tiles + i, 0))
                  for a in cast_next]
    out_shape += [jax.ShapeDtypeStruct(a.shape[1:], _BF16) for a in cast_next]
    return pl.pallas_call(
        functools.partial(_prompt_kernel, layer=layer, tile=tile, n_tiles=n_tiles,
                          n_cast=len(cast_next)),
        grid=(batch, n_tiles),
        in_specs=in_specs,
        out_specs=out_specs,
        out_shape=out_shape,
        scratch_shapes=[pltpu.VMEM((POOL_HALO + tile, POOL_WIDTH), _F32),
                        pltpu.VMEM((SUBLANES + tile, CONV_WIDTH), _F32)],
        compiler_params=pltpu.CompilerParams(
            dimension_semantics=("arbitrary", "arbitrary"),
            vmem_limit_bytes=VMEM_LIMIT_BYTES),
        name=f"prompt_layer{layer}",
    )(x, p, *weights, *prev_state, *cast_next)


def _sample_layer(layer, x, p, state_pool, state_conv, weights, prev_state):
    rows, _ = x.shape
    tile = TOKEN_TILE
    seqs = tile // SUBLANES
    n_tiles = rows // tile
    assert rows == n_tiles * tile and len(prev_state) == 2 * layer
    once = pl.Buffered(1)
    per_seq = lambda r, width, **kw: pl.BlockSpec((seqs, r, width), lambda i: (i, 0, 0), **kw)
    stacked = lambda r, width: pl.BlockSpec((layer + 1, seqs, r, width), lambda i: (0, i, 0, 0))
    in_specs = [
        pl.BlockSpec((tile, D_MODEL), lambda i: (i, 0)),
        pl.BlockSpec((None, tile, PLE_DIM), lambda i: (layer, i, 0)),
        pl.BlockSpec((None, seqs, POOL_BUF, POOL_WIDTH), lambda i: (layer, i, 0, 0), pipeline_mode=once),
        pl.BlockSpec((None, seqs, CONV_BUF, CONV_WIDTH), lambda i: (layer, i, 0, 0), pipeline_mode=once),
    ] + _weight_specs(weights) + [
        per_seq(POOL_BUF, POOL_WIDTH, pipeline_mode=once),
        per_seq(CONV_BUF, CONV_WIDTH, pipeline_mode=once)] * layer
    out_specs = [pl.BlockSpec((tile, D_MODEL), lambda i: (i, 0))]
    out_shape = [jax.ShapeDtypeStruct((rows, D_MODEL), _F32)]
    n_seq = rows // SUBLANES
    if layer == 0:
        out_specs += [per_seq(POOL_BUF, POOL_WIDTH), per_seq(CONV_BUF, CONV_WIDTH)]
        out_shape += [jax.ShapeDtypeStruct((n_seq, POOL_BUF, POOL_WIDTH), _F32),
                      jax.ShapeDtypeStruct((n_seq, CONV_BUF, CONV_WIDTH), _F32)]
    else:
        out_specs += [stacked(POOL_BUF, POOL_WIDTH), stacked(CONV_BUF, CONV_WIDTH)]
        out_shape += [jax.ShapeDtypeStruct((layer + 1, n_seq, POOL_BUF, POOL_WIDTH), _F32),
                      jax.ShapeDtypeStruct((layer + 1, n_seq, CONV_BUF, CONV_WIDTH), _F32)]
    return pl.pallas_call(
        functools.partial(_sample_kernel, layer=layer, tile=tile),
        grid=(n_tiles,),
        in_specs=in_specs,
        out_specs=out_specs,
        out_shape=out_shape,
        compiler_params=pltpu.CompilerParams(
            dimension_semantics=("arbitrary",),
            vmem_limit_bytes=VMEM_LIMIT_BYTES),
        name=f"sample_layer{layer}",
    )(x, p, state_pool, state_conv, *weights, *prev_state)


def kernel(x_prompt, x_sample, state_pool, state_conv, p_prompt, p_sample, norm_mix, w_in, pool_w, pool_scale, conv_w, conv_b, w_out, norm_mlp, w_up, w_down, norm_ple, w_ple_gate, w_ple_proj, norm_f):
    assert DEPTH == 2 and PAST_LEN >= POOL_BUF
    pw = pool_w.reshape(DEPTH, 2, 2, POOL_GC, POOL_GC)
    zero = jnp.zeros_like(pw[:, :, 0])
    pw_bd = jnp.concatenate(
        [jnp.concatenate([pw[:, :, 0], zero], axis=-1),
         jnp.concatenate([zero, pw[:, :, 1]], axis=-1)], axis=-2).astype(_BF16)
    shared = (norm_mix, pool_scale, conv_w, conv_b, norm_mlp, norm_ple, norm_f.reshape(1, D_MODEL),
              pw_bd, w_ple_proj.astype(_BF16))
    big_f32 = (w_in, w_out, w_up, w_down, w_ple_gate)
    weights = [_Weights(*shared, *(a[0].astype(_BF16) for a in big_f32))]

    h, pool_p, conv_p, *big_next = _prompt_layer(0, x_prompt, p_prompt, weights[0], (), big_f32)
    weights.append(_Weights(*shared, *big_next))
    y_prompt, pool_p, conv_p = _prompt_layer(1, h, p_prompt, weights[1], (pool_p, conv_p), ())

    dec_batch, dec_seq, _ = x_sample.shape
    assert dec_seq == SUBLANES
    rows = dec_batch * dec_seq
    ps = p_sample.reshape(DEPTH, rows, PLE_DIM)
    h, pool_s, conv_s = _sample_layer(0, x_sample.reshape(rows, D_MODEL), ps, state_pool, state_conv,
                                      weights[0], ())
    h, pool_s, conv_s = _sample_layer(1, h, ps, state_pool, state_conv, weights[1], (pool_s, conv_s))
    y_sample = h.reshape(dec_batch, dec_seq, D_MODEL)
    return (y_prompt, y_sample, pool_p, conv_p, pool_s, conv_s)
```

```python
import functools
from typing import NamedTuple

import jax
import jax.numpy as jnp
from jax import lax
from jax.experimental import pallas as pl
from jax.experimental.pallas import tpu as pltpu

D_MODEL = 1024
DEPTH = 2
PAST_LEN = 16384
POOL_WIDTH = 512
CONV_WIDTH = 512
POOL_WINDOWS = (2, 4, 8, 16)
POOL_GC = 128
POOL_BUF = 15
CONV_K = 3
CONV_BUF = 2
D_FF = 4 * D_MODEL
PLE_DIM = 256
IN_COLS = POOL_WIDTH + 3 * CONV_WIDTH
EPS = 1e-6

SUBLANES = 8
BF16_ROWS = 16
POOL_HALO = POOL_BUF + 1
TOKEN_TILE = 512
PROMPT_FF_CHUNK = 512
SAMPLE_FF_CHUNK = 1024
PROMPT_CHAINS = 1
VMEM_LIMIT_BYTES = 56 * 1024 * 1024
_BF16 = jnp.bfloat16
_F32 = jnp.float32


class _Weights(NamedTuple):
    norm_mix: jax.Array
    pool_scale: jax.Array
    conv_w: jax.Array
    conv_b: jax.Array
    norm_mlp: jax.Array
    norm_ple: jax.Array
    norm_f: jax.Array
    pool_w: jax.Array
    w_proj: jax.Array
    w_in: jax.Array
    w_out: jax.Array
    w_up: jax.Array
    w_down: jax.Array
    w_gate: jax.Array


def _rmsnorm(x, g):
    ms = jnp.mean(x * x, axis=-1, keepdims=True)
    return (x * lax.rsqrt(ms + EPS)) * g


def _dot(a, b):
    return jnp.dot(a, b, preferred_element_type=_F32)


def _norm_split(x, g):
    scale = lax.rsqrt(jnp.mean(x * x, axis=-1, keepdims=True) + EPS)
    return (x * g).astype(_BF16), scale


def _in_proj(h, w, layer):
    hg, scale = _norm_split(h, w.norm_mix[layer:layer + 1, :])
    proj = _dot(hg, w.w_in[...])
    u = proj[:, :POOL_WIDTH] * scale
    bg = proj[:, POOL_WIDTH:POOL_WIDTH + CONV_WIDTH] * scale
    cg = proj[:, POOL_WIDTH + CONV_WIDTH:POOL_WIDTH + 2 * CONV_WIDTH] * scale
    v = proj[:, POOL_WIDTH + 2 * CONV_WIDTH:] * scale
    return u, bg, cg * v


def _pool_out(ds, w, layer):
    return jnp.concatenate(
        [_dot(jnp.concatenate(ds[2 * k:2 * k + 2], axis=-1), w.pool_w[layer, k]) for k in range(2)],
        axis=-1) * w.pool_scale[layer:layer + 1, :]


def _conv_out(w, layer, back2, back1, cur):
    cw = w.conv_w[layer]
    return w.conv_b[layer:layer + 1, :] + (cw[0:1, :] * back2 + cw[1:2, :] * back1 + cw[2:3, :] * cur)


def _out_proj(h, y_pool, y_conv, w):
    mix = jnp.concatenate([y_pool, y_conv], axis=-1).astype(_BF16)
    return h + _dot(mix, w.w_out[...])


def _mlp(h, w, layer, chunk):
    hg, scale = _norm_split(h, w.norm_mlp[layer:layer + 1, :])
    acc = h
    for c in range(D_FF // chunk):
        cs = slice(c * chunk, (c + 1) * chunk)
        f = jnp.maximum(_dot(hg, w.w_up[:, cs]) * scale, 0.0)
        acc = acc + _dot((f * f).astype(_BF16), w.w_down[cs, :])
    return acc


def _gated_embed(h, p, w, layer):
    hg, scale = _norm_split(h, w.norm_ple[layer:layer + 1, :])
    gate = jax.nn.sigmoid(_dot(hg, w.w_gate[...]) * scale)
    h = h + gate * _dot(p.astype(_BF16), w.w_proj[layer])
    if layer == DEPTH - 1:
        h = _rmsnorm(h, w.norm_f[...])
    return h


def _prompt_kernel(*refs, layer, tile, n_tiles, n_cast):
    n_w = len(_Weights._fields)
    x_ref, p_ref = refs[:2]
    w = _Weights(*refs[2:2 + n_w])
    pos = 2 + n_w
    prev = refs[pos:pos + 2 * layer]
    pos += 2 * layer
    cast_in = refs[pos:pos + n_cast]
    pos += n_cast
    out_ref, npool_ref, nconv_ref = refs[pos:pos + 3]
    cast_out = refs[pos + 3:pos + 3 + n_cast]
    pool_ext, conv_ext = refs[pos + 3 + n_cast:]
    i = pl.program_id(1)
    halo = POOL_HALO
    chalo = SUBLANES

    @pl.when(i == 0)
    def _():
        pool_ext[0:halo, :] = jnp.zeros((halo, POOL_WIDTH), _F32)
        conv_ext[0:chalo, :] = jnp.zeros((chalo, CONV_WIDTH), _F32)

    sub = tile // PROMPT_CHAINS
    starts = range(0, tile, sub)

    def project(r0):
        h = x_ref[r0:r0 + sub, :]
        u, bg, cv = _in_proj(h, w, layer)
        pool_ext[halo + r0:halo + r0 + sub, :] = u
        conv_ext[chalo + r0:chalo + r0 + sub, :] = cv
        return h, bg

    def mix(r0, h, bg):
        position = i * tile + r0 + lax.broadcasted_iota(jnp.int32, (sub, 1), 0)
        ds = []
        for g, win in enumerate(POOL_WINDOWS):
            e = pool_ext[r0:r0 + halo + sub, g * POOL_GC:(g + 1) * POOL_GC]
            s = e
            span = 1
            while span < win:
                s = s + pltpu.roll(s, span, axis=0)
                span *= 2
            cnt = jnp.minimum(win, position + 1).astype(_F32)
            ds.append((s[halo:, :] / cnt - e[halo:, :]).astype(_BF16))
        y_pool = _pool_out(ds, w, layer)
        cvx = conv_ext[r0:r0 + chalo + sub, :]
        z = _conv_out(w, layer, pltpu.roll(cvx, 2, axis=0)[chalo:, :],
                      pltpu.roll(cvx, 1, axis=0)[chalo:, :], cvx[chalo:, :])
        return _out_proj(h, y_pool, bg * z, w)

    projected = [project(r0) for r0 in starts]
    for src, dst in zip(cast_in, cast_out):
        dst[...] = src[...].astype(_BF16)
    hs = [mix(r0, h, bg) for r0, (h, bg) in zip(starts, projected)]
    hs = [_mlp(h, w, layer, PROMPT_FF_CHUNK) for h in hs]
    for r0, h in zip(starts, hs):
        out_ref[r0:r0 + sub, :] = _gated_embed(h, p_ref[r0:r0 + sub, :], w, layer)

    @pl.when(i == n_tiles - 1)
    def _():
        seq = pl.ds(pl.program_id(0), 1)
        new_pool = pool_ext[tile + halo - POOL_BUF:tile + halo, :][:, None, :]
        new_conv = conv_ext[tile + chalo - CONV_BUF:tile + chalo, :]
        if layer == 0:
            npool_ref[:, seq, :] = new_pool
            nconv_ref[...] = new_conv
        else:
            for k in range(layer):
                npool_ref[k] = prev[2 * k][...]
                nconv_ref[k] = prev[2 * k + 1][...]
            npool_ref[layer, :, seq, :] = new_pool
            nconv_ref[layer] = new_conv

    @pl.when(i < n_tiles - 1)
    def _():
        pool_ext[0:halo, :] = pool_ext[tile:tile + halo, :]
        conv_ext[0:chalo, :] = conv_ext[tile:tile + chalo, :]


def _sample_kernel(*refs, layer, tile):
    n_w = len(_Weights._fields)
    x_ref, p_ref, pool_hist, conv_hist = refs[:4]
    w = _Weights(*refs[4:4 + n_w])
    prev = refs[4 + n_w:4 + n_w + 2 * layer]
    out_ref, npool_ref, nconv_ref, hist_rows, u_rows = refs[4 + n_w + 2 * layer:]
    seqs = tile // SUBLANES
    n_slabs = POOL_WIDTH // POOL_GC

    h = x_ref[...]
    u, bg, cv = _in_proj(h, w, layer)

    for g in range(n_slabs):
        sl = slice(g * POOL_GC, (g + 1) * POOL_GC)
        for k in range(SUBLANES):
            rows = pl.ds(k, seqs, stride=SUBLANES)
            hist_rows[g, rows, :] = pool_hist[max(k - 1, 0), :, sl]
            hist_rows[n_slabs + g, rows, :] = pool_hist[POOL_BUF - SUBLANES + k, :, sl]
        u_rows[g] = u[:, sl]

    t = lax.broadcasted_iota(jnp.int32, (tile, 1), 0) % SUBLANES
    per_seq = lambda a: jnp.broadcast_to(a, (seqs, SUBLANES, a.shape[-1])).reshape(tile, a.shape[-1])

    def back(j, cur, prev_group):
        if j == SUBLANES:
            return prev_group
        return jnp.where(t >= j, pltpu.roll(cur, j, axis=0),
                         pltpu.roll(prev_group, tile + j - SUBLANES, axis=0))

    ds = []
    for g, win in enumerate(POOL_WINDOWS):
        sl = slice(g * POOL_GC, (g + 1) * POOL_GC)
        e = u[:, sl]
        groups = [hist_rows[g], hist_rows[n_slabs + g], e]
        span = 1
        while span < win:
            groups = [cur + back(span, cur, prv) for prv, cur in zip([groups[0]] + groups[:-1], groups)]
            span *= 2
        ds.append((groups[-1] * (1.0 / win) - e).astype(_BF16))
    y_pool = _pool_out(ds, w, layer)

    last1 = per_seq(conv_hist[:, 1:2, :])
    last2 = per_seq(conv_hist[:, 0:1, :])
    back1 = jnp.where(t >= 1, pltpu.roll(cv, 1, axis=0), last1)
    back2 = jnp.where(t >= 2, pltpu.roll(cv, 2, axis=0), jnp.where(t == 1, last1, last2))
    z = _conv_out(w, layer, back2, back1, cv)

    h = _mlp(_out_proj(h, y_pool, bg * z, w), w, layer, SAMPLE_FF_CHUNK)
    out_ref[...] = _gated_embed(h, p_ref[...], w, layer)

    keep = POOL_BUF - SUBLANES
    pool_dst = npool_ref if layer == 0 else npool_ref.at[layer]
    conv_dst = nconv_ref if layer == 0 else nconv_ref.at[layer]
    for k in range(keep):
        pool_dst[k] = pool_hist[SUBLANES + k]
    for g in range(n_slabs):
        sl = slice(g * POOL_GC, (g + 1) * POOL_GC)
        for k in range(SUBLANES):
            pool_dst[keep + k, :, sl] = u_rows[g, pl.ds(k, seqs, stride=SUBLANES), :]
    conv_dst[...] = cv.reshape(seqs, SUBLANES, CONV_WIDTH)[:, SUBLANES - CONV_BUF:, :]
    for k in range(layer):
        npool_ref[k] = prev[2 * k][...]
        nconv_ref[k] = prev[2 * k + 1][...]


def _weight_specs(weights):
    return [pl.BlockSpec(a.shape, lambda *_, nd=a.ndim: (0,) * nd, pipeline_mode=pl.Buffered(1))
            for a in weights]


def _prompt_layer(layer, x, p, weights, prev_state, cast_next):
    batch, seq, _ = x.shape
    tile = TOKEN_TILE
    n_tiles = seq // tile
    steps = batch * n_tiles
    assert seq == n_tiles * tile and tile >= POOL_HALO and len(prev_state) == 2 * layer
    for a in cast_next:
        assert a.shape[1] % (steps * BF16_ROWS) == 0
    whole = lambda shape: pl.BlockSpec(shape, lambda b, i, nd=len(shape): (0,) * nd)
    per_batch = pl.BlockSpec((None, CONV_BUF, CONV_WIDTH), lambda b, i: (b, 0, 0))
    pool_shape = (POOL_BUF, batch, POOL_WIDTH)
    in_specs = [
        pl.BlockSpec((None, tile, D_MODEL), lambda b, i: (b, i, 0)),
        pl.BlockSpec((None, None, tile, PLE_DIM), lambda b, i: (layer, b, i, 0)),
    ] + _weight_specs(weights) + [whole(pool_shape), per_batch] * layer + [
        pl.BlockSpec((None, a.shape[1] // steps, a.shape[2]),
                     lambda b, i: (layer + 1, b * n_tiles + i, 0)) for a in cast_next]
    out_specs = [pl.BlockSpec((None, tile, D_MODEL), lambda b, i: (b, i, 0))]
    out_shape = [jax.ShapeDtypeStruct((batch, seq, D_MODEL), _F32)]
    if layer == 0:
        out_specs += [whole(pool_shape), per_batch]
        out_shape += [jax.ShapeDtypeStruct(pool_shape, _F32),
                      jax.ShapeDtypeStruct((batch, CONV_BUF, CONV_WIDTH), _F32)]
    else:
        out_specs += [whole((layer + 1,) + pool_shape),
                      pl.BlockSpec((layer + 1, None, CONV_BUF, CONV_WIDTH), lambda b, i: (0, b, 0, 0))]
        out_shape += [jax.ShapeDtypeStruct((layer + 1,) + pool_shape, _F32),
                      jax.ShapeDtypeStruct((layer + 1, batch, CONV_BUF, CONV_WIDTH), _F32)]
    out_specs += [pl.BlockSpec((a.shape[1] // steps, a.shape[2]), lambda b, i: (b * n_tiles + i, 0))
                  for a in cast_next]
    out_shape += [jax.ShapeDtypeStruct(a.shape[1:], _BF16) for a in cast_next]
    return pl.pallas_call(
        functools.partial(_prompt_kernel, layer=layer, tile=tile, n_tiles=n_tiles,
                          n_cast=len(cast_next)),
        grid=(batch, n_tiles),
        in_specs=in_specs,
        out_specs=out_specs,
        out_shape=out_shape,
        scratch_shapes=[pltpu.VMEM((POOL_HALO + tile, POOL_WIDTH), _F32),
                        pltpu.VMEM((SUBLANES + tile, CONV_WIDTH), _F32)],
        compiler_params=pltpu.CompilerParams(
            dimension_semantics=("arbitrary", "arbitrary"),
            vmem_limit_bytes=VMEM_LIMIT_BYTES),
        name=f"prompt_layer{layer}",
    )(x, p, *weights, *prev_state, *cast_next)


def _sample_layer(layer, x, p, state_pool, state_conv, weights, prev_state):
    rows, _ = x.shape
    tile = TOKEN_TILE
    seqs = tile // SUBLANES
    n_tiles = rows // tile
    assert rows == n_tiles * tile and len(prev_state) == 2 * layer
    once = pl.Buffered(1)
    pool_rows = lambda **kw: pl.BlockSpec((POOL_BUF, seqs, POOL_WIDTH), lambda i: (0, i, 0), **kw)
    conv_rows = lambda **kw: pl.BlockSpec((seqs, CONV_BUF, CONV_WIDTH), lambda i: (i, 0, 0), **kw)
    in_specs = [
        pl.BlockSpec((tile, D_MODEL), lambda i: (i, 0)),
        pl.BlockSpec((None, tile, PLE_DIM), lambda i: (layer, i, 0)),
        pl.BlockSpec((None, POOL_BUF, seqs, POOL_WIDTH), lambda i: (layer, 0, i, 0), pipeline_mode=once),
        pl.BlockSpec((None, seqs, CONV_BUF, CONV_WIDTH), lambda i: (layer, i, 0, 0), pipeline_mode=once),
    ] + _weight_specs(weights) + [pool_rows(pipeline_mode=once), conv_rows(pipeline_mode=once)] * layer
    out_specs = [pl.BlockSpec((tile, D_MODEL), lambda i: (i, 0))]
    out_shape = [jax.ShapeDtypeStruct((rows, D_MODEL), _F32)]
    n_seq = rows // SUBLANES
    if layer == 0:
        out_specs += [pool_rows(), conv_rows()]
        out_shape += [jax.ShapeDtypeStruct((POOL_BUF, n_seq, POOL_WIDTH), _F32),
                      jax.ShapeDtypeStruct((n_seq, CONV_BUF, CONV_WIDTH), _F32)]
    else:
        out_specs += [pl.BlockSpec((layer + 1, POOL_BUF, seqs, POOL_WIDTH), lambda i: (0, 0, i, 0)),
                      pl.BlockSpec((layer + 1, seqs, CONV_BUF, CONV_WIDTH), lambda i: (0, i, 0, 0))]
        out_shape += [jax.ShapeDtypeStruct((layer + 1, POOL_BUF, n_seq, POOL_WIDTH), _F32),
                      jax.ShapeDtypeStruct((layer + 1, n_seq, CONV_BUF, CONV_WIDTH), _F32)]
    n_slabs = POOL_WIDTH // POOL_GC
    return pl.pallas_call(
        functools.partial(_sample_kernel, layer=layer, tile=tile),
        grid=(n_tiles,),
        in_specs=in_specs,
        out_specs=out_specs,
        out_shape=out_shape,
        scratch_shapes=[pltpu.VMEM((2 * n_slabs, tile, POOL_GC), _F32),
                        pltpu.VMEM((n_slabs, tile, POOL_GC), _F32)],
        compiler_params=pltpu.CompilerParams(
            dimension_semantics=("arbitrary",),
            vmem_limit_bytes=VMEM_LIMIT_BYTES),
        name=f"sample_layer{layer}",
    )(x, p, state_pool, state_conv, *weights, *prev_state)


def kernel(x_prompt, x_sample, state_pool, state_conv, p_prompt, p_sample, norm_mix, w_in, pool_w, pool_scale, conv_w, conv_b, w_out, norm_mlp, w_up, w_down, norm_ple, w_ple_gate, w_ple_proj, norm_f):
    assert DEPTH == 2 and PAST_LEN >= POOL_BUF
    pw = pool_w.reshape(DEPTH, 2, 2, POOL_GC, POOL_GC)
    zero = jnp.zeros_like(pw[:, :, 0])
    pw_bd = jnp.concatenate(
        [jnp.concatenate([pw[:, :, 0], zero], axis=-1),
         jnp.concatenate([zero, pw[:, :, 1]], axis=-1)], axis=-2).astype(_BF16)
    shared = (norm_mix, pool_scale, conv_w, conv_b, norm_mlp, norm_ple, norm_f.reshape(1, D_MODEL),
              pw_bd, w_ple_proj.astype(_BF16))
    big_f32 = (w_in, w_out, w_up, w_down, w_ple_gate)
    weights = [_Weights(*shared, *(a[0].astype(_BF16) for a in big_f32))]

    h, pool_p, conv_p, *big_next = _prompt_layer(0, x_prompt, p_prompt, weights[0], (), big_f32)
    weights.append(_Weights(*shared, *big_next))
    y_prompt, pool_p, conv_p = _prompt_layer(1, h, p_prompt, weights[1], (pool_p, conv_p), ())

    dec_batch, dec_seq, _ = x_sample.shape
    assert dec_seq == SUBLANES
    rows = dec_batch * dec_seq
    ps = p_sample.reshape(DEPTH, rows, PLE_DIM)
    pool_hist = jnp.swapaxes(state_pool, 1, 2)
    h, pool_s, conv_s = _sample_layer(0, x_sample.reshape(rows, D_MODEL), ps, pool_hist, state_conv,
                                      weights[0], ())
    h, pool_s, conv_s = _sample_layer(1, h, ps, pool_hist, state_conv, weights[1], (pool_s, conv_s))
    y_sample = h.reshape(dec_batch, dec_seq, D_MODEL)
    return (y_prompt, y_sample, jnp.swapaxes(pool_p, 1, 2), conv_p, jnp.swapaxes(pool_s, 1, 2), conv_s)
```

```python
import functools
from typing import NamedTuple

import jax
import jax.numpy as jnp
from jax import lax
from jax.experimental import pallas as pl
from jax.experimental.pallas import tpu as pltpu

D_MODEL = 1024
DEPTH = 2
PAST_LEN = 16384
POOL_WIDTH = 512
CONV_WIDTH = 512
POOL_WINDOWS = (2, 4, 8, 16)
POOL_GC = 128
POOL_BUF = 15
CONV_K = 3
CONV_BUF = 2
D_FF = 4 * D_MODEL
PLE_DIM = 256
IN_COLS = POOL_WIDTH + 3 * CONV_WIDTH
EPS = 1e-6

SUBLANES = 8
BF16_ROWS = 16
POOL_HALO = POOL_BUF + 1
TOKEN_TILE = 512
PROMPT_FF_CHUNK = 512
SAMPLE_FF_CHUNK = 1024
PROMPT_CHAINS = 1
VMEM_LIMIT_BYTES = 56 * 1024 * 1024
_BF16 = jnp.bfloat16
_F32 = jnp.float32


class _Weights(NamedTuple):
    norm_mix: jax.Array
    pool_scale: jax.Array
    conv_w: jax.Array
    conv_b: jax.Array
    norm_mlp: jax.Array
    norm_ple: jax.Array
    norm_f: jax.Array
    pool_w: jax.Array
    w_proj: jax.Array
    w_in: jax.Array
    w_out: jax.Array
    w_up: jax.Array
    w_down: jax.Array
    w_gate: jax.Array


_BIG_FIELDS = _Weights._fields[-5:]
CAST_STAGE = (512, 1024)


def _rmsnorm(x, g):
    ms = jnp.mean(x * x, axis=-1, keepdims=True)
    return (x * lax.rsqrt(ms + EPS)) * g


def _dot(a, b):
    return jnp.dot(a, b, preferred_element_type=_F32)


def _norm_split(x, g):
    scale = lax.rsqrt(jnp.mean(x * x, axis=-1, keepdims=True) + EPS)
    return (x * g).astype(_BF16), scale


def _in_proj(h, w, layer):
    hg, scale = _norm_split(h, w.norm_mix[layer:layer + 1, :])
    proj = _dot(hg, w.w_in[...])
    u = proj[:, :POOL_WIDTH] * scale
    bg = proj[:, POOL_WIDTH:POOL_WIDTH + CONV_WIDTH] * scale
    cg = proj[:, POOL_WIDTH + CONV_WIDTH:POOL_WIDTH + 2 * CONV_WIDTH] * scale
    v = proj[:, POOL_WIDTH + 2 * CONV_WIDTH:] * scale
    return u, bg, cg * v


def _pool_out(ds, w, layer):
    return jnp.concatenate(
        [_dot(jnp.concatenate(ds[2 * k:2 * k + 2], axis=-1), w.pool_w[layer, k]) for k in range(2)],
        axis=-1) * w.pool_scale[layer:layer + 1, :]


def _conv_out(w, layer, back2, back1, cur):
    cw = w.conv_w[layer]
    return w.conv_b[layer:layer + 1, :] + (cw[0:1, :] * back2 + cw[1:2, :] * back1 + cw[2:3, :] * cur)


def _out_proj(h, y_pool, y_conv, w):
    mix = jnp.concatenate([y_pool, y_conv], axis=-1).astype(_BF16)
    return h + _dot(mix, w.w_out[...])


def _mlp(h, w, layer, chunk):
    hg, scale = _norm_split(h, w.norm_mlp[layer:layer + 1, :])
    acc = h
    for c in range(D_FF // chunk):
        cs = slice(c * chunk, (c + 1) * chunk)
        f = jnp.maximum(_dot(hg, w.w_up[:, cs]) * scale, 0.0)
        acc = acc + _dot((f * f).astype(_BF16), w.w_down[cs, :])
    return acc


def _gated_embed(h, p, w, layer):
    hg, scale = _norm_split(h, w.norm_ple[layer:layer + 1, :])
    gate = jax.nn.sigmoid(_dot(hg, w.w_gate[...]) * scale)
    h = h + gate * _dot(p.astype(_BF16), w.w_proj[layer])
    if layer == DEPTH - 1:
        h = _rmsnorm(h, w.norm_f[...])
    return h


def _prompt_kernel(*refs, layer, tile, n_tiles, n_cast, own_cast):
    n_big = len(_BIG_FIELDS)
    n_w = len(_Weights._fields) - (n_big if own_cast else 0)
    x_ref, p_ref = refs[:2]
    pos = 2 + n_w
    prev = refs[pos:pos + 2 * layer]
    pos += 2 * layer
    cast_in = refs[pos:pos + n_cast]
    pos += n_cast
    if own_cast:
        big_hbm = refs[pos:pos + n_big]
        pos += n_big
    out_ref, npool_ref, nconv_ref = refs[pos:pos + 3]
    cast_out = refs[pos + 3:pos + 3 + n_cast]
    pos += 3 + n_cast
    if own_cast:
        big_out = refs[pos:pos + n_big]
        pool_ext, conv_ext = refs[pos + n_big:pos + n_big + 2]
        big_vmem = refs[pos + n_big + 2:pos + 2 * n_big + 2]
        stage, fetch_sem, publish_sem = refs[pos + 2 * n_big + 2:]
        w = _Weights(*refs[2:2 + n_w], *big_vmem)
    else:
        pool_ext, conv_ext = refs[pos:]
        w = _Weights(*refs[2:2 + n_w])
    i = pl.program_id(1)
    halo = POOL_HALO
    chalo = SUBLANES

    if own_cast:
        first = (pl.program_id(0) == 0) & (i == 0)
        last = (pl.program_id(0) == pl.num_programs(0) - 1) & (i == n_tiles - 1)
        rows, cols = stage.shape[1:]
        chunks = [(m, r0, c0) for m in range(n_big)
                  for r0 in range(0, big_vmem[m].shape[0], rows)
                  for c0 in range(0, big_vmem[m].shape[1], cols)]

        def fetch(k):
            m, r0, c0 = chunks[k]
            return pltpu.make_async_copy(big_hbm[m].at[layer, pl.ds(r0, rows), pl.ds(c0, cols)],
                                         stage.at[k % 2], fetch_sem.at[k % 2])

        def publish(m):
            return pltpu.make_async_copy(big_vmem[m], big_out[m], publish_sem.at[m])

        @pl.when(first)
        def _():
            fetch(0).start()
            for k, (m, r0, c0) in enumerate(chunks):
                if k + 1 < len(chunks):
                    fetch(k + 1).start()
                fetch(k).wait()
                big_vmem[m][r0:r0 + rows, c0:c0 + cols] = stage[k % 2].astype(_BF16)
            for m in range(n_big):
                publish(m).start()

        @pl.when(last)
        def _():
            for m in range(n_big):
                publish(m).wait()

    @pl.when(i == 0)
    def _():
        pool_ext[0:halo, :] = jnp.zeros((halo, POOL_WIDTH), _F32)
        conv_ext[0:chalo, :] = jnp.zeros((chalo, CONV_WIDTH), _F32)

    sub = tile // PROMPT_CHAINS
    starts = range(0, tile, sub)

    def project(r0):
        h = x_ref[r0:r0 + sub, :]
        u, bg, cv = _in_proj(h, w, layer)
        pool_ext[halo + r0:halo + r0 + sub, :] = u
        conv_ext[chalo + r0:chalo + r0 + sub, :] = cv
        return h, bg

    def mix(r0, h, bg):
        position = i * tile + r0 + lax.broadcasted_iota(jnp.int32, (sub, 1), 0)
        ds = []
        for g, win in enumerate(POOL_WINDOWS):
            e = pool_ext[r0:r0 + halo + sub, g * POOL_GC:(g + 1) * POOL_GC]
            s = e
            span = 1
            while span < win:
                s = s + pltpu.roll(s, span, axis=0)
                span *= 2
            cnt = jnp.minimum(win, position + 1).astype(_F32)
            ds.append((s[halo:, :] / cnt - e[halo:, :]).astype(_BF16))
        y_pool = _pool_out(ds, w, layer)
        cvx = conv_ext[r0:r0 + chalo + sub, :]
        z = _conv_out(w, layer, pltpu.roll(cvx, 2, axis=0)[chalo:, :],
                      pltpu.roll(cvx, 1, axis=0)[chalo:, :], cvx[chalo:, :])
        return _out_proj(h, y_pool, bg * z, w)

    projected = [project(r0) for r0 in starts]
    for src, dst in zip(cast_in, cast_out):
        dst[...] = src[...].astype(_BF16)
    hs = [mix(r0, h, bg) for r0, (h, bg) in zip(starts, projected)]
    hs = [_mlp(h, w, layer, PROMPT_FF_CHUNK) for h in hs]
    for r0, h in zip(starts, hs):
        out_ref[r0:r0 + sub, :] = _gated_embed(h, p_ref[r0:r0 + sub, :], w, layer)

    @pl.when(i == n_tiles - 1)
    def _():
        seq = pl.ds(pl.program_id(0), 1)
        new_pool = pool_ext[tile + halo - POOL_BUF:tile + halo, :][:, None, :]
        new_conv = conv_ext[tile + chalo - CONV_BUF:tile + chalo, :]
        if layer == 0:
            npool_ref[:, seq, :] = new_pool
            nconv_ref[...] = new_conv
        else:
            for k in range(layer):
                npool_ref[k] = prev[2 * k][...]
                nconv_ref[k] = prev[2 * k + 1][...]
            npool_ref[layer, :, seq, :] = new_pool
            nconv_ref[layer] = new_conv

    @pl.when(i < n_tiles - 1)
    def _():
        pool_ext[0:halo, :] = pool_ext[tile:tile + halo, :]
        conv_ext[0:chalo, :] = conv_ext[tile:tile + chalo, :]


def _sample_kernel(*refs, layer, tile):
    n_w = len(_Weights._fields)
    x_ref, p_ref, pool_hist, conv_hist = refs[:4]
    w = _Weights(*refs[4:4 + n_w])
    prev = refs[4 + n_w:4 + n_w + 2 * layer]
    out_ref, npool_ref, nconv_ref, hist_rows, u_rows = refs[4 + n_w + 2 * layer:]
    seqs = tile // SUBLANES
    n_slabs = POOL_WIDTH // POOL_GC

    h = x_ref[...]
    u, bg, cv = _in_proj(h, w, layer)

    for g in range(n_slabs):
        sl = slice(g * POOL_GC, (g + 1) * POOL_GC)
        for k in range(SUBLANES):
            rows = pl.ds(k, seqs, stride=SUBLANES)
            hist_rows[g, rows, :] = pool_hist[max(k - 1, 0), :, sl]
            hist_rows[n_slabs + g, rows, :] = pool_hist[POOL_BUF - SUBLANES + k, :, sl]
        u_rows[g] = u[:, sl]

    t = lax.broadcasted_iota(jnp.int32, (tile, 1), 0) % SUBLANES
    per_seq = lambda a: jnp.broadcast_to(a, (seqs, SUBLANES, a.shape[-1])).reshape(tile, a.shape[-1])

    def back(j, cur, prev_group):
        if j == SUBLANES:
            return prev_group
        return jnp.where(t >= j, pltpu.roll(cur, j, axis=0),
                         pltpu.roll(prev_group, tile + j - SUBLANES, axis=0))

    ds = []
    for g, win in enumerate(POOL_WINDOWS):
        sl = slice(g * POOL_GC, (g + 1) * POOL_GC)
        e = u[:, sl]
        groups = [hist_rows[g], hist_rows[n_slabs + g], e]
        span = 1
        while span < win:
            groups = [cur + back(span, cur, prv) for prv, cur in zip([groups[0]] + groups[:-1], groups)]
            span *= 2
        ds.append((groups[-1] * (1.0 / win) - e).astype(_BF16))
    y_pool = _pool_out(ds, w, layer)

    last1 = per_seq(conv_hist[:, 1:2, :])
    last2 = per_seq(conv_hist[:, 0:1, :])
    back1 = jnp.where(t >= 1, pltpu.roll(cv, 1, axis=0), last1)
    back2 = jnp.where(t >= 2, pltpu.roll(cv, 2, axis=0), jnp.where(t == 1, last1, last2))
    z = _conv_out(w, layer, back2, back1, cv)

    h = _mlp(_out_proj(h, y_pool, bg * z, w), w, layer, SAMPLE_FF_CHUNK)
    out_ref[...] = _gated_embed(h, p_ref[...], w, layer)

    keep = POOL_BUF - SUBLANES
    pool_dst = npool_ref if layer == 0 else npool_ref.at[layer]
    conv_dst = nconv_ref if layer == 0 else nconv_ref.at[layer]
    for k in range(keep):
        pool_dst[k] = pool_hist[SUBLANES + k]
    for g in range(n_slabs):
        sl = slice(g * POOL_GC, (g + 1) * POOL_GC)
        for k in range(SUBLANES):
            pool_dst[keep + k, :, sl] = u_rows[g, pl.ds(k, seqs, stride=SUBLANES), :]
    conv_dst[...] = cv.reshape(seqs, SUBLANES, CONV_WIDTH)[:, SUBLANES - CONV_BUF:, :]
    for k in range(layer):
        npool_ref[k] = prev[2 * k][...]
        nconv_ref[k] = prev[2 * k + 1][...]


def _weight_specs(weights):
    return [pl.BlockSpec(a.shape, lambda *_, nd=a.ndim: (0,) * nd, pipeline_mode=pl.Buffered(1))
            for a in weights]


def _prompt_layer(layer, x, p, weights, prev_state, cast_next, cast_own=()):
    batch, seq, _ = x.shape
    tile = TOKEN_TILE
    n_tiles = seq // tile
    steps = batch * n_tiles
    assert seq == n_tiles * tile and tile >= POOL_HALO and len(prev_state) == 2 * layer
    for a in cast_next:
        assert a.shape[1] % (steps * BF16_ROWS) == 0
    for a in cast_own:
        assert a.shape[1] % CAST_STAGE[0] == 0 and a.shape[2] % CAST_STAGE[1] == 0
    assert len(weights) + len(cast_own) == len(_Weights._fields)
    hbm = pl.BlockSpec(memory_space=pl.ANY)
    whole = lambda shape: pl.BlockSpec(shape, lambda b, i, nd=len(shape): (0,) * nd)
    per_batch = pl.BlockSpec((None, CONV_BUF, CONV_WIDTH), lambda b, i: (b, 0, 0))
    pool_shape = (POOL_BUF, batch, POOL_WIDTH)
    in_specs = [
        pl.BlockSpec((None, tile, D_MODEL), lambda b, i: (b, i, 0)),
        pl.BlockSpec((None, None, tile, PLE_DIM), lambda b, i: (layer, b, i, 0)),
    ] + _weight_specs(weights) + [whole(pool_shape), per_batch] * layer + [
        pl.BlockSpec((None, a.shape[1] // steps, a.shape[2]),
                     lambda b, i: (layer + 1, b * n_tiles + i, 0)) for a in cast_next] + [
        hbm] * len(cast_own)
    out_specs = [pl.BlockSpec((None, tile, D_MODEL), lambda b, i: (b, i, 0))]
    out_shape = [jax.ShapeDtypeStruct((batch, seq, D_MODEL), _F32)]
    if layer == 0:
        out_specs += [whole(pool_shape), per_batch]
        out_shape += [jax.ShapeDtypeStruct(pool_shape, _F32),
                      jax.ShapeDtypeStruct((batch, CONV_BUF, CONV_WIDTH), _F32)]
    else:
        out_specs += [whole((layer + 1,) + pool_shape),
                      pl.BlockSpec((layer + 1, None, CONV_BUF, CONV_WIDTH), lambda b, i: (0, b, 0, 0))]
        out_shape += [jax.ShapeDtypeStruct((layer + 1,) + pool_shape, _F32),
                      jax.ShapeDtypeStruct((layer + 1, batch, CONV_BUF, CONV_WIDTH), _F32)]
    out_specs += [pl.BlockSpec((a.shape[1] // steps, a.shape[2]), lambda b, i: (b * n_tiles + i, 0))
                  for a in cast_next]
    out_shape += [jax.ShapeDtypeStruct(a.shape[1:], _BF16) for a in cast_next]
    out_specs += [hbm] * len(cast_own)
    out_shape += [jax.ShapeDtypeStruct(a.shape[1:], _BF16) for a in cast_own]
    scratch_shapes = [pltpu.VMEM((POOL_HALO + tile, POOL_WIDTH), _F32),
                      pltpu.VMEM((SUBLANES + tile, CONV_WIDTH), _F32)]
    if cast_own:
        scratch_shapes += [pltpu.VMEM(a.shape[1:], _BF16) for a in cast_own] + [
            pltpu.VMEM((2,) + CAST_STAGE, _F32),
            pltpu.SemaphoreType.DMA((2,)), pltpu.SemaphoreType.DMA((len(cast_own),))]
    return pl.pallas_call(
        functools.partial(_prompt_kernel, layer=layer, tile=tile, n_tiles=n_tiles,
                          n_cast=len(cast_next), own_cast=bool(cast_own)),
        grid=(batch, n_tiles),
        in_specs=in_specs,
        out_specs=out_specs,
        out_shape=out_shape,
        scratch_shapes=scratch_shapes,
        compiler_params=pltpu.CompilerParams(
            dimension_semantics=("arbitrary", "arbitrary"),
            vmem_limit_bytes=VMEM_LIMIT_BYTES),
        name=f"prompt_layer{layer}",
    )(x, p, *weights, *prev_state, *cast_next, *cast_own)


def _sample_layer(layer, x, p, state_pool, state_conv, weights, prev_state):
    rows, _ = x.shape
    tile = TOKEN_TILE
    seqs = tile // SUBLANES
    n_tiles = rows // tile
    assert rows == n_tiles * tile and len(prev_state) == 2 * layer
    once = pl.Buffered(1)
    pool_rows = lambda **kw: pl.BlockSpec((POOL_BUF, seqs, POOL_WIDTH), lambda i: (0, i, 0), **kw)
    conv_rows = lambda **kw: pl.BlockSpec((seqs, CONV_BUF, CONV_WIDTH), lambda i: (i, 0, 0), **kw)
    in_specs = [
        pl.BlockSpec((tile, D_MODEL), lambda i: (i, 0)),
        pl.BlockSpec((None, tile, PLE_DIM), lambda i: (layer, i, 0)),
        pl.BlockSpec((None, POOL_BUF, seqs, POOL_WIDTH), lambda i: (layer, 0, i, 0), pipeline_mode=once),
        pl.BlockSpec((None, seqs, CONV_BUF, CONV_WIDTH), lambda i: (layer, i, 0, 0), pipeline_mode=once),
    ] + _weight_specs(weights) + [pool_rows(pipeline_mode=once), conv_rows(pipeline_mode=once)] * layer
    out_specs = [pl.BlockSpec((tile, D_MODEL), lambda i: (i, 0))]
    out_shape = [jax.ShapeDtypeStruct((rows, D_MODEL), _F32)]
    n_seq = rows // SUBLANES
    if layer == 0:
        out_specs += [pool_rows(), conv_rows()]
        out_shape += [jax.ShapeDtypeStruct((POOL_BUF, n_seq, POOL_WIDTH), _F32),
                      jax.ShapeDtypeStruct((n_seq, CONV_BUF, CONV_WIDTH), _F32)]
    else:
        out_specs += [pl.BlockSpec((layer + 1, POOL_BUF, seqs, POOL_WIDTH), lambda i: (0, 0, i, 0)),
                      pl.BlockSpec((layer + 1, seqs, CONV_BUF, CONV_WIDTH), lambda i: (0, i, 0, 0))]
        out_shape += [jax.ShapeDtypeStruct((layer + 1, POOL_BUF, n_seq, POOL_WIDTH), _F32),
                      jax.ShapeDtypeStruct((layer + 1, n_seq, CONV_BUF, CONV_WIDTH), _F32)]
    n_slabs = POOL_WIDTH // POOL_GC
    return pl.pallas_call(
        functools.partial(_sample_kernel, layer=layer, tile=tile),
        grid=(n_tiles,),
        in_specs=in_specs,
        out_specs=out_specs,
        out_shape=out_shape,
        scratch_shapes=[pltpu.VMEM((2 * n_slabs, tile, POOL_GC), _F32),
                        pltpu.VMEM((n_slabs, tile, POOL_GC), _F32)],
        compiler_params=pltpu.CompilerParams(
            dimension_semantics=("arbitrary",),
            vmem_limit_bytes=VMEM_LIMIT_BYTES),
        name=f"sample_layer{layer}",
    )(x, p, state_pool, state_conv, *weights, *prev_state)


def kernel(x_prompt, x_sample, state_pool, state_conv, p_prompt, p_sample, norm_mix, w_in, pool_w, pool_scale, conv_w, conv_b, w_out, norm_mlp, w_up, w_down, norm_ple, w_ple_gate, w_ple_proj, norm_f):
    assert DEPTH == 2 and PAST_LEN >= POOL_BUF
    pw = pool_w.reshape(DEPTH, 2, 2, POOL_GC, POOL_GC)
    zero = jnp.zeros_like(pw[:, :, 0])
    pw_bd = jnp.concatenate(
        [jnp.concatenate([pw[:, :, 0], zero], axis=-1),
         jnp.concatenate([zero, pw[:, :, 1]], axis=-1)], axis=-2).astype(_BF16)
    shared = (norm_mix, pool_scale, conv_w, conv_b, norm_mlp, norm_ple, norm_f.reshape(1, D_MODEL),
              pw_bd, w_ple_proj.astype(_BF16))
    big_f32 = (w_in, w_out, w_up, w_down, w_ple_gate)

    n_big = len(big_f32)
    h, pool_p, conv_p, *big_bf16 = _prompt_layer(0, x_prompt, p_prompt, shared, (), big_f32, big_f32)
    weights = [_Weights(*shared, *big_bf16[n_big:]), _Weights(*shared, *big_bf16[:n_big])]
    y_prompt, pool_p, conv_p = _prompt_layer(1, h, p_prompt, weights[1], (pool_p, conv_p), ())

    dec_batch, dec_seq, _ = x_sample.shape
    assert dec_seq == SUBLANES
    rows = dec_batch * dec_seq
    ps = p_sample.reshape(DEPTH, rows, PLE_DIM)
    pool_hist = jnp.swapaxes(state_pool, 1, 2)
    h, pool_s, conv_s = _sample_layer(0, x_sample.reshape(rows, D_MODEL), ps, pool_hist, state_conv,
                                      weights[0], ())
    h, pool_s, conv_s = _sample_layer(1, h, ps, pool_hist, state_conv, weights[1], (pool_s, conv_s))
    y_sample = h.reshape(dec_batch, dec_seq, D_MODEL)
    return (y_prompt, y_sample, jnp.swapaxes(pool_p, 1, 2), conv_p, jnp.swapaxes(pool_s, 1, 2), conv_s)
```

```python
import functools
from typing import NamedTuple

import jax
import jax.numpy as jnp
from jax import lax
from jax.experimental import pallas as pl
from jax.experimental.pallas import tpu as pltpu

D_MODEL = 1024
DEPTH = 2
PAST_LEN = 16384
POOL_WIDTH = 512
CONV_WIDTH = 512
POOL_WINDOWS = (2, 4, 8, 16)
POOL_GC = 128
POOL_BUF = 15
CONV_K = 3
CONV_BUF = 2
D_FF = 4 * D_MODEL
PLE_DIM = 256
IN_COLS = POOL_WIDTH + 3 * CONV_WIDTH
EPS = 1e-6

SUBLANES = 8
BF16_ROWS = 16
POOL_HALO = POOL_BUF + 1
TOKEN_TILE = 512
PROMPT_FF_CHUNK = 512
SAMPLE_FF_CHUNK = 1024
PROMPT_CHAINS = 1
VMEM_LIMIT_BYTES = 56 * 1024 * 1024
_BF16 = jnp.bfloat16
_F32 = jnp.float32


class _Weights(NamedTuple):
    norm_mix: jax.Array
    pool_scale: jax.Array
    conv_w: jax.Array
    conv_b: jax.Array
    norm_mlp: jax.Array
    norm_ple: jax.Array
    norm_f: jax.Array
    pool_w: jax.Array
    w_proj: jax.Array
    w_in: jax.Array
    w_out: jax.Array
    w_up: jax.Array
    w_down: jax.Array
    w_gate: jax.Array


_BIG_FIELDS = _Weights._fields[-5:]
CAST_STAGE = (512, 1024)


def _rmsnorm(x, g):
    ms = jnp.mean(x * x, axis=-1, keepdims=True)
    return (x * lax.rsqrt(ms + EPS)) * g


def _dot(a, b):
    return jnp.dot(a, b, preferred_element_type=_F32)


def _norm_split(x, g):
    scale = lax.rsqrt(jnp.mean(x * x, axis=-1, keepdims=True) + EPS)
    return (x * g).astype(_BF16), scale


def _in_proj(h, w, layer):
    hg, scale = _norm_split(h, w.norm_mix[layer:layer + 1, :])
    proj = _dot(hg, w.w_in[...])
    u = proj[:, :POOL_WIDTH] * scale
    bg = proj[:, POOL_WIDTH:POOL_WIDTH + CONV_WIDTH] * scale
    cg = proj[:, POOL_WIDTH + CONV_WIDTH:POOL_WIDTH + 2 * CONV_WIDTH] * scale
    v = proj[:, POOL_WIDTH + 2 * CONV_WIDTH:] * scale
    return u, bg, cg * v


def _pool_out(ds, w, layer):
    outs = []
    for k in range(0, len(POOL_WINDOWS), 2):
        a = w.pool_w[layer, k].astype(_BF16)
        b = w.pool_w[layer, k + 1].astype(_BF16)
        zero = jnp.zeros_like(a)
        both = jnp.concatenate([jnp.concatenate([a, zero], axis=1),
                                jnp.concatenate([zero, b], axis=1)], axis=0)
        outs.append(_dot(jnp.concatenate(ds[k:k + 2], axis=-1), both))
    return jnp.concatenate(outs, axis=-1) * w.pool_scale[layer:layer + 1, :]


def _conv_out(w, layer, back2, back1, cur):
    tap = lambda k: w.conv_w[k, layer:layer + 1, :]
    return w.conv_b[layer:layer + 1, :] + (tap(0) * back2 + tap(1) * back1 + tap(2) * cur)


def _out_proj(h, y_pool, y_conv, w):
    mix = jnp.concatenate([y_pool, y_conv], axis=-1).astype(_BF16)
    return h + _dot(mix, w.w_out[...])


def _mlp(h, w, layer, chunk):
    hg, scale = _norm_split(h, w.norm_mlp[layer:layer + 1, :])
    acc = h
    for c in range(D_FF // chunk):
        cs = slice(c * chunk, (c + 1) * chunk)
        f = jnp.maximum(_dot(hg, w.w_up[:, cs]) * scale, 0.0)
        acc = acc + _dot((f * f).astype(_BF16), w.w_down[cs, :])
    return acc


def _gated_embed(h, p, w, layer):
    hg, scale = _norm_split(h, w.norm_ple[layer:layer + 1, :])
    gate = jax.nn.sigmoid(_dot(hg, w.w_gate[...]) * scale)
    h = h + gate * _dot(p.astype(_BF16), w.w_proj[layer].astype(_BF16))
    if layer == DEPTH - 1:
        h = _rmsnorm(h, w.norm_f[...])
    return h


def _prompt_kernel(*refs, layer, tile, n_tiles, n_cast, own_cast):
    n_big = len(_BIG_FIELDS)
    n_w = len(_Weights._fields) - (n_big if own_cast else 0)
    x_ref, p_ref = refs[:2]
    pos = 2 + n_w
    prev = refs[pos:pos + 2 * layer]
    pos += 2 * layer
    cast_in = refs[pos:pos + n_cast]
    pos += n_cast
    if own_cast:
        big_hbm = refs[pos:pos + n_big]
        pos += n_big
    out_ref, npool_ref, nconv_ref = refs[pos:pos + 3]
    cast_out = refs[pos + 3:pos + 3 + n_cast]
    pos += 3 + n_cast
    if own_cast:
        big_out = refs[pos:pos + n_big]
        pool_ext, conv_ext = refs[pos + n_big:pos + n_big + 2]
        big_vmem = refs[pos + n_big + 2:pos + 2 * n_big + 2]
        stage, fetch_sem, publish_sem = refs[pos + 2 * n_big + 2:]
        w = _Weights(*refs[2:2 + n_w], *big_vmem)
    else:
        pool_ext, conv_ext = refs[pos:]
        w = _Weights(*refs[2:2 + n_w])
    i = pl.program_id(1)
    halo = POOL_HALO
    chalo = SUBLANES

    if own_cast:
        first = (pl.program_id(0) == 0) & (i == 0)
        last = (pl.program_id(0) == pl.num_programs(0) - 1) & (i == n_tiles - 1)
        rows, cols = stage.shape[1:]
        chunks = [(m, r0, c0) for m in range(n_big)
                  for r0 in range(0, big_vmem[m].shape[0], rows)
                  for c0 in range(0, big_vmem[m].shape[1], cols)]

        def fetch(k):
            m, r0, c0 = chunks[k]
            return pltpu.make_async_copy(big_hbm[m].at[layer, pl.ds(r0, rows), pl.ds(c0, cols)],
                                         stage.at[k % 2], fetch_sem.at[k % 2])

        def publish(m):
            return pltpu.make_async_copy(big_vmem[m], big_out[m], publish_sem.at[m])

        @pl.when(first)
        def _():
            fetch(0).start()
            for k, (m, r0, c0) in enumerate(chunks):
                if k + 1 < len(chunks):
                    fetch(k + 1).start()
                fetch(k).wait()
                big_vmem[m][r0:r0 + rows, c0:c0 + cols] = stage[k % 2].astype(_BF16)
            for m in range(n_big):
                publish(m).start()

        @pl.when(last)
        def _():
            for m in range(n_big):
                publish(m).wait()

    @pl.when(i == 0)
    def _():
        pool_ext[0:halo, :] = jnp.zeros((halo, POOL_WIDTH), _F32)
        conv_ext[0:chalo, :] = jnp.zeros((chalo, CONV_WIDTH), _F32)

    sub = tile // PROMPT_CHAINS
    starts = range(0, tile, sub)

    def project(r0):
        h = x_ref[r0:r0 + sub, :]
        u, bg, cv = _in_proj(h, w, layer)
        pool_ext[halo + r0:halo + r0 + sub, :] = u
        conv_ext[chalo + r0:chalo + r0 + sub, :] = cv
        return h, bg

    def mix(r0, h, bg):
        position = i * tile + r0 + lax.broadcasted_iota(jnp.int32, (sub, 1), 0)
        ds = []
        for g, win in enumerate(POOL_WINDOWS):
            e = pool_ext[r0:r0 + halo + sub, g * POOL_GC:(g + 1) * POOL_GC]
            s = e
            span = 1
            while span < win:
                s = s + pltpu.roll(s, span, axis=0)
                span *= 2
            cnt = jnp.minimum(win, position + 1).astype(_F32)
            ds.append((s[halo:, :] / cnt - e[halo:, :]).astype(_BF16))
        y_pool = _pool_out(ds, w, layer)
        cvx = conv_ext[r0:r0 + chalo + sub, :]
        z = _conv_out(w, layer, pltpu.roll(cvx, 2, axis=0)[chalo:, :],
                      pltpu.roll(cvx, 1, axis=0)[chalo:, :], cvx[chalo:, :])
        return _out_proj(h, y_pool, bg * z, w)

    projected = [project(r0) for r0 in starts]
    for src, dst in zip(cast_in, cast_out):
        dst[...] = src[...].astype(_BF16)
    hs = [mix(r0, h, bg) for r0, (h, bg) in zip(starts, projected)]
    hs = [_mlp(h, w, layer, PROMPT_FF_CHUNK) for h in hs]
    for r0, h in zip(starts, hs):
        out_ref[r0:r0 + sub, :] = _gated_embed(h, p_ref[r0:r0 + sub, :], w, layer)

    @pl.when(i == n_tiles - 1)
    def _():
        seq = pl.ds(pl.program_id(0), 1)
        new_pool = pool_ext[tile + halo - POOL_BUF:tile + halo, :][:, None, :]
        new_conv = conv_ext[tile + chalo - CONV_BUF:tile + chalo, :]
        if layer == 0:
            npool_ref[:, seq, :] = new_pool
            nconv_ref[...] = new_conv
        else:
            for k in range(layer):
                npool_ref[k] = prev[2 * k][...]
                nconv_ref[k] = prev[2 * k + 1][...]
            npool_ref[layer, :, seq, :] = new_pool
            nconv_ref[layer] = new_conv

    @pl.when(i < n_tiles - 1)
    def _():
        pool_ext[0:halo, :] = pool_ext[tile:tile + halo, :]
        conv_ext[0:chalo, :] = conv_ext[tile:tile + chalo, :]


def _sample_kernel(*refs, layer, tile):
    n_w = len(_Weights._fields)
    x_ref, p_ref, pool_hist, conv_hist = refs[:4]
    w = _Weights(*refs[4:4 + n_w])
    prev = refs[4 + n_w:4 + n_w + 2 * layer]
    out_ref, npool_ref, nconv_ref, pool_ext, conv_ext = refs[4 + n_w + 2 * layer:]
    seqs = tile // SUBLANES
    halo = POOL_HALO * seqs
    chalo = CONV_BUF * seqs
    positions = range(SUBLANES)
    by_position = lambda ref: jnp.concatenate([ref[:, t, :] for t in positions], axis=0)

    h = by_position(x_ref) if layer == 0 else x_ref[...].reshape(tile, D_MODEL)
    u, bg, cv = _in_proj(h, w, layer)

    pool_ext[0:seqs, :] = pool_hist[0]
    pool_ext[seqs:halo, :] = pool_hist[...].reshape(POOL_BUF * seqs, POOL_WIDTH)
    pool_ext[halo:halo + tile, :] = u
    for k in range(CONV_BUF):
        conv_ext[k * seqs:(k + 1) * seqs, :] = conv_hist[:, k, :]
    conv_ext[chalo:chalo + tile, :] = cv

    ds = []
    for g, win in enumerate(POOL_WINDOWS):
        sl = slice(g * POOL_GC, (g + 1) * POOL_GC)
        s = pool_ext[halo - (win - 1) * seqs:halo + tile, sl]
        span = 1
        while span < win:
            s = s[span * seqs:, :] + s[:-span * seqs, :]
            span *= 2
        ds.append((s * (1.0 / win) - u[:, sl]).astype(_BF16))
    y_pool = _pool_out(ds, w, layer)

    z = _conv_out(w, layer, conv_ext[0:tile, :], conv_ext[seqs:seqs + tile, :], cv)

    h = _mlp(_out_proj(h, y_pool, bg * z, w), w, layer, SAMPLE_FF_CHUNK)
    h = _gated_embed(h, by_position(p_ref), w, layer)
    if layer == DEPTH - 1:
        for t in positions:
            out_ref[:, t, :] = h[t * seqs:(t + 1) * seqs, :]
    else:
        out_ref[...] = h.reshape(SUBLANES, seqs, D_MODEL)

    pool_dst = npool_ref if layer == 0 else npool_ref.at[layer]
    conv_dst = nconv_ref if layer == 0 else nconv_ref.at[layer]
    pool_dst[...] = pool_ext[halo + tile - POOL_BUF * seqs:halo + tile, :].reshape(
        POOL_BUF, seqs, POOL_WIDTH)
    for k in range(CONV_BUF):
        conv_dst[:, k, :] = conv_ext[tile + k * seqs:tile + (k + 1) * seqs, :]
    for k in range(layer):
        npool_ref[k] = prev[2 * k][...]
        nconv_ref[k] = prev[2 * k + 1][...]


def _weight_specs(weights):
    return [pl.BlockSpec(a.shape, lambda *_, nd=a.ndim: (0,) * nd, pipeline_mode=pl.Buffered(1))
            for a in weights]


def _prompt_layer(layer, x, p, weights, prev_state, cast_next, cast_own=()):
    batch, seq, _ = x.shape
    tile = TOKEN_TILE
    n_tiles = seq // tile
    steps = batch * n_tiles
    assert seq == n_tiles * tile and tile >= POOL_HALO and len(prev_state) == 2 * layer
    for a in cast_next:
        assert a.shape[1] % (steps * BF16_ROWS) == 0
    for a in cast_own:
        assert a.shape[1] % CAST_STAGE[0] == 0 and a.shape[2] % CAST_STAGE[1] == 0
    assert len(weights) + len(cast_own) == len(_Weights._fields)
    hbm = pl.BlockSpec(memory_space=pl.ANY)
    whole = lambda shape: pl.BlockSpec(shape, lambda b, i, nd=len(shape): (0,) * nd)
    per_batch = pl.BlockSpec((None, CONV_BUF, CONV_WIDTH), lambda b, i: (b, 0, 0))
    pool_shape = (POOL_BUF, batch, POOL_WIDTH)
    in_specs = [
        pl.BlockSpec((None, tile, D_MODEL), lambda b, i: (b, i, 0)),
        pl.BlockSpec((None, None, tile, PLE_DIM), lambda b, i: (layer, b, i, 0)),
    ] + _weight_specs(weights) + [whole(pool_shape), per_batch] * layer + [
        pl.BlockSpec((None, a.shape[1] // steps, a.shape[2]),
                     lambda b, i: (layer + 1, b * n_tiles + i, 0)) for a in cast_next] + [
        hbm] * len(cast_own)
    out_specs = [pl.BlockSpec((None, tile, D_MODEL), lambda b, i: (b, i, 0))]
    out_shape = [jax.ShapeDtypeStruct((batch, seq, D_MODEL), _F32)]
    if layer == 0:
        out_specs += [whole(pool_shape), per_batch]
        out_shape += [jax.ShapeDtypeStruct(pool_shape, _F32),
                      jax.ShapeDtypeStruct((batch, CONV_BUF, CONV_WIDTH), _F32)]
    else:
        out_specs += [whole((layer + 1,) + pool_shape),
                      pl.BlockSpec((layer + 1, None, CONV_BUF, CONV_WIDTH), lambda b, i: (0, b, 0, 0))]
        out_shape += [jax.ShapeDtypeStruct((layer + 1,) + pool_shape, _F32),
                      jax.ShapeDtypeStruct((layer + 1, batch, CONV_BUF, CONV_WIDTH), _F32)]
    out_specs += [pl.BlockSpec((a.shape[1] // steps, a.shape[2]), lambda b, i: (b * n_tiles + i, 0))
                  for a in cast_next]
    out_shape += [jax.ShapeDtypeStruct(a.shape[1:], _BF16) for a in cast_next]
    out_specs += [hbm] * len(cast_own)
    out_shape += [jax.ShapeDtypeStruct(a.shape[1:], _BF16) for a in cast_own]
    scratch_shapes = [pltpu.VMEM((POOL_HALO + tile, POOL_WIDTH), _F32),
                      pltpu.VMEM((SUBLANES + tile, CONV_WIDTH), _F32)]
    if cast_own:
        scratch_shapes += [pltpu.VMEM(a.shape[1:], _BF16) for a in cast_own] + [
            pltpu.VMEM((2,) + CAST_STAGE, _F32),
            pltpu.SemaphoreType.DMA((2,)), pltpu.SemaphoreType.DMA((len(cast_own),))]
    return pl.pallas_call(
        functools.partial(_prompt_kernel, layer=layer, tile=tile, n_tiles=n_tiles,
                          n_cast=len(cast_next), own_cast=bool(cast_own)),
        grid=(batch, n_tiles),
        in_specs=in_specs,
        out_specs=out_specs,
        out_shape=out_shape,
        scratch_shapes=scratch_shapes,
        compiler_params=pltpu.CompilerParams(
            dimension_semantics=("arbitrary", "arbitrary"),
            vmem_limit_bytes=VMEM_LIMIT_BYTES),
        name=f"prompt_layer{layer}",
    )(x, p, *weights, *prev_state, *cast_next, *cast_own)


def _sample_layer(layer, x, p, state_pool, state_conv, weights, prev_state):
    n_seq = p.shape[1]
    seqs = TOKEN_TILE // SUBLANES
    tile = seqs * SUBLANES
    n_tiles = n_seq // seqs
    assert n_seq == n_tiles * seqs and p.shape[2] == SUBLANES and len(prev_state) == 2 * layer
    once = pl.Buffered(1)
    pool_rows = lambda **kw: pl.BlockSpec((POOL_BUF, seqs, POOL_WIDTH), lambda i: (0, i, 0), **kw)
    conv_rows = lambda **kw: pl.BlockSpec((seqs, CONV_BUF, CONV_WIDTH), lambda i: (i, 0, 0), **kw)
    by_sequence = pl.BlockSpec((seqs, SUBLANES, D_MODEL), lambda i: (i, 0, 0))
    by_position = pl.BlockSpec((SUBLANES, seqs, D_MODEL), lambda i: (0, i, 0))
    last = layer == DEPTH - 1
    in_specs = [
        by_sequence if layer == 0 else by_position,
        pl.BlockSpec((None, seqs, SUBLANES, PLE_DIM), lambda i: (layer, i, 0, 0)),
        pl.BlockSpec((None, POOL_BUF, seqs, POOL_WIDTH), lambda i: (layer, 0, i, 0), pipeline_mode=once),
        pl.BlockSpec((None, seqs, CONV_BUF, CONV_WIDTH), lambda i: (layer, i, 0, 0), pipeline_mode=once),
    ] + _weight_specs(weights) + [pool_rows(pipeline_mode=once), conv_rows(pipeline_mode=once)] * layer
    out_specs = [by_sequence if last else by_position]
    out_shape = [jax.ShapeDtypeStruct((n_seq, SUBLANES, D_MODEL) if last else
                                      (SUBLANES, n_seq, D_MODEL), _F32)]
    if layer == 0:
        out_specs += [pool_rows(), conv_rows()]
        out_shape += [jax.ShapeDtypeStruct((POOL_BUF, n_seq, POOL_WIDTH), _F32),
                      jax.ShapeDtypeStruct((n_seq, CONV_BUF, CONV_WIDTH), _F32)]
    else:
        out_specs += [pl.BlockSpec((layer + 1, POOL_BUF, seqs, POOL_WIDTH), lambda i: (0, 0, i, 0)),
                      pl.BlockSpec((layer + 1, seqs, CONV_BUF, CONV_WIDTH), lambda i: (0, i, 0, 0))]
        out_shape += [jax.ShapeDtypeStruct((layer + 1, POOL_BUF, n_seq, POOL_WIDTH), _F32),
                      jax.ShapeDtypeStruct((layer + 1, n_seq, CONV_BUF, CONV_WIDTH), _F32)]
    return pl.pallas_call(
        functools.partial(_sample_kernel, layer=layer, tile=tile),
        grid=(n_tiles,),
        in_specs=in_specs,
        out_specs=out_specs,
        out_shape=out_shape,
        scratch_shapes=[pltpu.VMEM(((POOL_HALO + SUBLANES) * seqs, POOL_WIDTH), _F32),
                        pltpu.VMEM(((CONV_BUF + SUBLANES) * seqs, CONV_WIDTH), _F32)],
        compiler_params=pltpu.CompilerParams(
            dimension_semantics=("arbitrary",),
            vmem_limit_bytes=VMEM_LIMIT_BYTES),
        name=f"sample_layer{layer}",
    )(x, p, state_pool, state_conv, *weights, *prev_state)


def kernel(x_prompt, x_sample, state_pool, state_conv, p_prompt, p_sample, norm_mix, w_in, pool_w, pool_scale, conv_w, conv_b, w_out, norm_mlp, w_up, w_down, norm_ple, w_ple_gate, w_ple_proj, norm_f):
    assert DEPTH == 2 and PAST_LEN >= POOL_BUF
    shared = (norm_mix, pool_scale, jnp.swapaxes(conv_w, 0, 1), conv_b, norm_mlp, norm_ple,
              norm_f.reshape(1, D_MODEL), pool_w, w_ple_proj)
    big_f32 = (w_in, w_out, w_up, w_down, w_ple_gate)

    n_big = len(big_f32)
    h, pool_p, conv_p, *big_bf16 = _prompt_layer(0, x_prompt, p_prompt, shared, (), big_f32, big_f32)
    weights = [_Weights(*shared, *big_bf16[n_big:]), _Weights(*shared, *big_bf16[:n_big])]
    y_prompt, pool_p, conv_p = _prompt_layer(1, h, p_prompt, weights[1], (pool_p, conv_p), ())

    assert x_sample.shape[1] == SUBLANES
    pool_hist = jnp.swapaxes(state_pool, 1, 2)
    h, pool_s, conv_s = _sample_layer(0, x_sample, p_sample, pool_hist, state_conv, weights[0], ())
    y_sample, pool_s, conv_s = _sample_layer(1, h, p_sample, pool_hist, state_conv, weights[1],
                                             (pool_s, conv_s))
    return (y_prompt, y_sample, jnp.swapaxes(pool_p, 1, 2), conv_p, jnp.swapaxes(pool_s, 1, 2), conv_s)
```

```python
import functools
from typing import NamedTuple

import jax
import jax.numpy as jnp
from jax import lax
from jax.experimental import pallas as pl
from jax.experimental.pallas import tpu as pltpu

D_MODEL = 1024
DEPTH = 2
PAST_LEN = 16384
POOL_WIDTH = 512
CONV_WIDTH = 512
POOL_WINDOWS = (2, 4, 8, 16)
POOL_GC = 128
POOL_BUF = 15
CONV_K = 3
CONV_BUF = 2
D_FF = 4 * D_MODEL
PLE_DIM = 256
IN_COLS = POOL_WIDTH + 3 * CONV_WIDTH
EPS = 1e-6

SUBLANES = 8
BF16_ROWS = 16
POOL_HALO = POOL_BUF + 1
TOKEN_TILE = 512
PROMPT_FF_CHUNK = 512
SAMPLE_FF_CHUNK = 1024
VMEM_LIMIT_BYTES = 56 * 1024 * 1024

_BF16 = jnp.bfloat16
_F32 = jnp.float32


class _Weights(NamedTuple):
    norm_mix: jax.Array
    pool_scale: jax.Array
    conv_w: jax.Array
    conv_b: jax.Array
    norm_mlp: jax.Array
    norm_ple: jax.Array
    norm_f: jax.Array
    pool_w: jax.Array
    w_proj: jax.Array
    w_in: jax.Array
    w_out: jax.Array
    w_up: jax.Array
    w_down: jax.Array
    w_gate: jax.Array


_BIG_FIELDS = _Weights._fields[-5:]
CAST_STAGE = (512, 1024)


def _rmsnorm(x, g):
    ms = jnp.mean(x * x, axis=-1, keepdims=True)
    return (x * lax.rsqrt(ms + EPS)) * g


def _dot(a, b):
    return jnp.dot(a, b, preferred_element_type=_F32)


def _norm_split(x, g):
    scale = lax.rsqrt(jnp.mean(x * x, axis=-1, keepdims=True) + EPS)
    return (x * g).astype(_BF16), scale


def _in_proj(h, w, layer):
    hg, scale = _norm_split(h, w.norm_mix[layer:layer + 1, :])
    proj = _dot(hg, w.w_in[...])
    u = proj[:, :POOL_WIDTH] * scale
    bg = proj[:, POOL_WIDTH:POOL_WIDTH + CONV_WIDTH] * scale
    cg = proj[:, POOL_WIDTH + CONV_WIDTH:POOL_WIDTH + 2 * CONV_WIDTH] * scale
    v = proj[:, POOL_WIDTH + 2 * CONV_WIDTH:] * scale
    return u, bg, cg * v


def _pool_out(ds, w, layer):
    outs = []
    for k in range(0, len(POOL_WINDOWS), 2):
        a = w.pool_w[layer, k].astype(_BF16)
        b = w.pool_w[layer, k + 1].astype(_BF16)
        zero = jnp.zeros_like(a)
        both = jnp.concatenate([jnp.concatenate([a, zero], axis=1),
                                jnp.concatenate([zero, b], axis=1)], axis=0)
        outs.append(_dot(jnp.concatenate(ds[k:k + 2], axis=-1), both))
    return jnp.concatenate(outs, axis=-1) * w.pool_scale[layer:layer + 1, :]


def _conv_out(w, layer, back2, back1, cur):
    tap = lambda k: w.conv_w[k, layer:layer + 1, :]
    return w.conv_b[layer:layer + 1, :] + (tap(0) * back2 + tap(1) * back1 + tap(2) * cur)


def _out_proj(h, y_pool, y_conv, w):
    mix = jnp.concatenate([y_pool, y_conv], axis=-1).astype(_BF16)
    return h + _dot(mix, w.w_out[...])


def _mlp(h, w, layer, chunk):
    hg, scale = _norm_split(h, w.norm_mlp[layer:layer + 1, :])
    acc = h
    for c in range(D_FF // chunk):
        cs = slice(c * chunk, (c + 1) * chunk)
        f = jnp.maximum(_dot(hg, w.w_up[:, cs]) * scale, 0.0)
        acc = acc + _dot((f * f).astype(_BF16), w.w_down[cs, :])
    return acc


def _gated_embed(h, p, w, layer):
    hg, scale = _norm_split(h, w.norm_ple[layer:layer + 1, :])
    gate = jax.nn.sigmoid(_dot(hg, w.w_gate[...]) * scale)
    h = h + gate * _dot(p.astype(_BF16), w.w_proj[layer].astype(_BF16))
    if layer == DEPTH - 1:
        h = _rmsnorm(h, w.norm_f[...])
    return h


def _prompt_kernel(*refs, layer, tile, n_tiles, n_cast, own_cast):
    n_big = len(_BIG_FIELDS)
    n_w = len(_Weights._fields) - (n_big if own_cast else 0)
    x_ref, p_ref = refs[:2]
    pos = 2 + n_w
    prev = refs[pos:pos + 2 * layer]
    pos += 2 * layer
    cast_in = refs[pos:pos + n_cast]
    pos += n_cast
    if own_cast:
        big_hbm = refs[pos:pos + n_big]
        pos += n_big
    out_ref, npool_ref, nconv_ref = refs[pos:pos + 3]
    cast_out = refs[pos + 3:pos + 3 + n_cast]
    pos += 3 + n_cast
    if own_cast:
        big_out = refs[pos:pos + n_big]
        pool_ext, conv_ext = refs[pos + n_big:pos + n_big + 2]
        big_vmem = refs[pos + n_big + 2:pos + 2 * n_big + 2]
        stage, fetch_sem, publish_sem = refs[pos + 2 * n_big + 2:]
        w = _Weights(*refs[2:2 + n_w], *big_vmem)
    else:
        pool_ext, conv_ext = refs[pos:]
        w = _Weights(*refs[2:2 + n_w])
    i = pl.program_id(1)
    halo = POOL_HALO
    chalo = SUBLANES

    if own_cast:
        first = (pl.program_id(0) == 0) & (i == 0)
        last = (pl.program_id(0) == pl.num_programs(0) - 1) & (i == n_tiles - 1)
        rows, cols = stage.shape[1:]
        chunks = [(m, r0, c0) for m in range(n_big)
                  for r0 in range(0, big_vmem[m].shape[0], rows)
                  for c0 in range(0, big_vmem[m].shape[1], cols)]

        def fetch(k):
            m, r0, c0 = chunks[k]
            return pltpu.make_async_copy(big_hbm[m].at[layer, pl.ds(r0, rows), pl.ds(c0, cols)],
                                         stage.at[k % 2], fetch_sem.at[k % 2])

        def publish(m):
            return pltpu.make_async_copy(big_vmem[m], big_out[m], publish_sem.at[m])

        @pl.when(first)
        def _():
            fetch(0).start()
            for k, (m, r0, c0) in enumerate(chunks):
                if k + 1 < len(chunks):
                    fetch(k + 1).start()
                fetch(k).wait()
                big_vmem[m][r0:r0 + rows, c0:c0 + cols] = stage[k % 2].astype(_BF16)
            for m in range(n_big):
                publish(m).start()

        @pl.when(last)
        def _():
            for m in range(n_big):
                publish(m).wait()

    @pl.when(i == 0)
    def _():
        pool_ext[0:halo, :] = jnp.zeros((halo, POOL_WIDTH), _F32)
        conv_ext[0:chalo, :] = jnp.zeros((chalo, CONV_WIDTH), _F32)

    h = x_ref[...]
    u, bg, cv = _in_proj(h, w, layer)
    pool_ext[halo:halo + tile, :] = u
    conv_ext[chalo:chalo + tile, :] = cv

    for src, dst in zip(cast_in, cast_out):
        dst[...] = src[...].astype(_BF16)

    position = i * tile + lax.broadcasted_iota(jnp.int32, (tile, 1), 0)
    ds = []
    for g, win in enumerate(POOL_WINDOWS):
        e = pool_ext[:, g * POOL_GC:(g + 1) * POOL_GC]
        s = e
        span = 1
        while span < win:
            s = s + pltpu.roll(s, span, axis=0)
            span *= 2
        cnt = jnp.minimum(win, position + 1).astype(_F32)
        ds.append((s[halo:, :] / cnt - e[halo:, :]).astype(_BF16))
    y_pool = _pool_out(ds, w, layer)
    cvx = conv_ext[...]
    z = _conv_out(w, layer, pltpu.roll(cvx, 2, axis=0)[chalo:, :],
                  pltpu.roll(cvx, 1, axis=0)[chalo:, :], cvx[chalo:, :])

    h = _mlp(_out_proj(h, y_pool, bg * z, w), w, layer, PROMPT_FF_CHUNK)
    out_ref[...] = _gated_embed(h, p_ref[...], w, layer)

    @pl.when(i == n_tiles - 1)
    def _():
        seq = pl.ds(pl.program_id(0), 1)
        new_pool = pool_ext[tile + halo - POOL_BUF:tile + halo, :][:, None, :]
        new_conv = conv_ext[tile + chalo - CONV_BUF:tile + chalo, :]
        if layer == 0:
            npool_ref[:, seq, :] = new_pool
            nconv_ref[...] = new_conv
        else:
            for k in range(layer):
                npool_ref[k] = prev[2 * k][...]
                nconv_ref[k] = prev[2 * k + 1][...]
            npool_ref[layer, :, seq, :] = new_pool
            nconv_ref[layer] = new_conv

    @pl.when(i < n_tiles - 1)
    def _():
        pool_ext[0:halo, :] = pool_ext[tile:tile + halo, :]
        conv_ext[0:chalo, :] = conv_ext[tile:tile + chalo, :]


def _sample_kernel(*refs, layer, tile):
    n_w = len(_Weights._fields)
    x_ref, p_ref, pool_hist, conv_hist = refs[:4]
    w = _Weights(*refs[4:4 + n_w])
    prev = refs[4 + n_w:4 + n_w + 2 * layer]
    out_ref, npool_ref, nconv_ref, pool_ext, conv_ext = refs[4 + n_w + 2 * layer:]
    seqs = tile // SUBLANES
    halo = POOL_HALO * seqs
    chalo = CONV_BUF * seqs
    positions = range(SUBLANES)
    by_position = lambda ref: jnp.concatenate([ref[:, t, :] for t in positions], axis=0)

    h = by_position(x_ref) if layer == 0 else x_ref[...].reshape(tile, D_MODEL)
    u, bg, cv = _in_proj(h, w, layer)

    pool_ext[0:seqs, :] = pool_hist[0]
    pool_ext[seqs:halo, :] = pool_hist[...].reshape(POOL_BUF * seqs, POOL_WIDTH)
    pool_ext[halo:halo + tile, :] = u
    for k in range(CONV_BUF):
        conv_ext[k * seqs:(k + 1) * seqs, :] = conv_hist[:, k, :]
    conv_ext[chalo:chalo + tile, :] = cv

    ds = []
    for g, win in enumerate(POOL_WINDOWS):
        sl = slice(g * POOL_GC, (g + 1) * POOL_GC)
        s = pool_ext[halo - (win - 1) * seqs:halo + tile, sl]
        span = 1
        while span < win:
            s = s[span * seqs:, :] + s[:-span * seqs, :]
            span *= 2
        ds.append((s * (1.0 / win) - u[:, sl]).astype(_BF16))
    y_pool = _pool_out(ds, w, layer)

    z = _conv_out(w, layer, conv_ext[0:tile, :], conv_ext[seqs:seqs + tile, :], cv)

    h = _mlp(_out_proj(h, y_pool, bg * z, w), w, layer, SAMPLE_FF_CHUNK)
    h = _gated_embed(h, by_position(p_ref), w, layer)
    if layer == DEPTH - 1:
        for t in positions:
            out_ref[:, t, :] = h[t * seqs:(t + 1) * seqs, :]
    else:
        out_ref[...] = h.reshape(SUBLANES, seqs, D_MODEL)

    pool_dst = npool_ref if layer == 0 else npool_ref.at[layer]
    conv_dst = nconv_ref if layer == 0 else nconv_ref.at[layer]
    pool_dst[...] = pool_ext[halo + tile - POOL_BUF * seqs:halo + tile, :].reshape(
        POOL_BUF, seqs, POOL_WIDTH)
    for k in range(CONV_BUF):
        conv_dst[:, k, :] = conv_ext[tile + k * seqs:tile + (k + 1) * seqs, :]
    for k in range(layer):
        npool_ref[k] = prev[2 * k][...]
        nconv_ref[k] = prev[2 * k + 1][...]


def _weight_specs(weights):
    return [pl.BlockSpec(a.shape, lambda *_, nd=a.ndim: (0,) * nd, pipeline_mode=pl.Buffered(1))
            for a in weights]


def _prompt_layer(layer, x, p, weights, prev_state, cast_next, cast_own=()):
    batch, seq, _ = x.shape
    tile = TOKEN_TILE
    n_tiles = seq // tile
    steps = batch * n_tiles
    assert seq == n_tiles * tile and tile >= POOL_HALO and len(prev_state) == 2 * layer
    for a in cast_next:
        assert a.shape[1] % (steps * BF16_ROWS) == 0
    for a in cast_own:
        assert a.shape[1] % CAST_STAGE[0] == 0 and a.shape[2] % CAST_STAGE[1] == 0
    assert len(weights) + len(cast_own) == len(_Weights._fields)
    hbm = pl.BlockSpec(memory_space=pl.ANY)
    whole = lambda shape: pl.BlockSpec(shape, lambda b, i, nd=len(shape): (0,) * nd)
    per_batch = pl.BlockSpec((None, CONV_BUF, CONV_WIDTH), lambda b, i: (b, 0, 0))
    pool_shape = (POOL_BUF, batch, POOL_WIDTH)
    in_specs = [
        pl.BlockSpec((None, tile, D_MODEL), lambda b, i: (b, i, 0)),
        pl.BlockSpec((None, None, tile, PLE_DIM), lambda b, i: (layer, b, i, 0)),
    ] + _weight_specs(weights) + [whole(pool_shape), per_batch] * layer + [
        pl.BlockSpec((None, a.shape[1] // steps, a.shape[2]),
                     lambda b, i: (layer + 1, b * n_tiles + i, 0)) for a in cast_next] + [
        hbm] * len(cast_own)
    out_specs = [pl.BlockSpec((None, tile, D_MODEL), lambda b, i: (b, i, 0))]
    out_shape = [jax.ShapeDtypeStruct((batch, seq, D_MODEL), _F32)]
    if layer == 0:
        out_specs += [whole(pool_shape), per_batch]
        out_shape += [jax.ShapeDtypeStruct(pool_shape, _F32),
                      jax.ShapeDtypeStruct((batch, CONV_BUF, CONV_WIDTH), _F32)]
    else:
        out_specs += [whole((layer + 1,) + pool_shape),
                      pl.BlockSpec((layer + 1, None, CONV_BUF, CONV_WIDTH), lambda b, i: (0, b, 0, 0))]
        out_shape += [jax.ShapeDtypeStruct((layer + 1,) + pool_shape, _F32),
                      jax.ShapeDtypeStruct((layer + 1, batch, CONV_BUF, CONV_WIDTH), _F32)]
    out_specs += [pl.BlockSpec((a.shape[1] // steps, a.shape[2]), lambda b, i: (b * n_tiles + i, 0))
                  for a in cast_next]
    out_shape += [jax.ShapeDtypeStruct(a.shape[1:], _BF16) for a in cast_next]
    out_specs += [hbm] * len(cast_own)
    out_shape += [jax.ShapeDtypeStruct(a.shape[1:], _BF16) for a in cast_own]
    scratch_shapes = [pltpu.VMEM((POOL_HALO + tile, POOL_WIDTH), _F32),
                      pltpu.VMEM((SUBLANES + tile, CONV_WIDTH), _F32)]
    if cast_own:
        scratch_shapes += [pltpu.VMEM(a.shape[1:], _BF16) for a in cast_own] + [
            pltpu.VMEM((2,) + CAST_STAGE, _F32),
            pltpu.SemaphoreType.DMA((2,)), pltpu.SemaphoreType.DMA((len(cast_own),))]
    return pl.pallas_call(
        functools.partial(_prompt_kernel, layer=layer, tile=tile, n_tiles=n_tiles,
                          n_cast=len(cast_next), own_cast=bool(cast_own)),
        grid=(batch, n_tiles),
        in_specs=in_specs,
        out_specs=out_specs,
        out_shape=out_shape,
        scratch_shapes=scratch_shapes,
        compiler_params=pltpu.CompilerParams(
            dimension_semantics=("arbitrary", "arbitrary"),
            vmem_limit_bytes=VMEM_LIMIT_BYTES),
        name=f"prompt_layer{layer}",
    )(x, p, *weights, *prev_state, *cast_next, *cast_own)


def _sample_layer(layer, x, p, state_pool, state_conv, weights, prev_state):
    n_seq = p.shape[1]
    seqs = TOKEN_TILE // SUBLANES
    tile = seqs * SUBLANES
    n_tiles = n_seq // seqs
    assert n_seq == n_tiles * seqs and p.shape[2] == SUBLANES and len(prev_state) == 2 * layer
    once = pl.Buffered(1)
    pool_rows = lambda **kw: pl.BlockSpec((POOL_BUF, seqs, POOL_WIDTH), lambda i: (0, i, 0), **kw)
    conv_rows = lambda **kw: pl.BlockSpec((seqs, CONV_BUF, CONV_WIDTH), lambda i: (i, 0, 0), **kw)
    by_sequence = pl.BlockSpec((seqs, SUBLANES, D_MODEL), lambda i: (i, 0, 0))
    by_position = pl.BlockSpec((SUBLANES, seqs, D_MODEL), lambda i: (0, i, 0))
    last = layer == DEPTH - 1
    in_specs = [
        by_sequence if layer == 0 else by_position,
        pl.BlockSpec((None, seqs, SUBLANES, PLE_DIM), lambda i: (layer, i, 0, 0)),
        pl.BlockSpec((None, POOL_BUF, seqs, POOL_WIDTH), lambda i: (layer, 0, i, 0), pipeline_mode=once),
        pl.BlockSpec((None, seqs, CONV_BUF, CONV_WIDTH), lambda i: (layer, i, 0, 0), pipeline_mode=once),
    ] + _weight_specs(weights) + [pool_rows(pipeline_mode=once), conv_rows(pipeline_mode=once)] * layer
    out_specs = [by_sequence if last else by_position]
    out_shape = [jax.ShapeDtypeStruct((n_seq, SUBLANES, D_MODEL) if last else
                                      (SUBLANES, n_seq, D_MODEL), _F32)]
    if layer == 0:
        out_specs += [pool_rows(), conv_rows()]
        out_shape += [jax.ShapeDtypeStruct((POOL_BUF, n_seq, POOL_WIDTH), _F32),
                      jax.ShapeDtypeStruct((n_seq, CONV_BUF, CONV_WIDTH), _F32)]
    else:
        out_specs += [pl.BlockSpec((layer + 1, POOL_BUF, seqs, POOL_WIDTH), lambda i: (0, 0, i, 0)),
                      pl.BlockSpec((layer + 1, seqs, CONV_BUF, CONV_WIDTH), lambda i: (0, i, 0, 0))]
        out_shape += [jax.ShapeDtypeStruct((layer + 1, POOL_BUF, n_seq, POOL_WIDTH), _F32),
                      jax.ShapeDtypeStruct((layer + 1, n_seq, CONV_BUF, CONV_WIDTH), _F32)]
    return pl.pallas_call(
        functools.partial(_sample_kernel, layer=layer, tile=tile),
        grid=(n_tiles,),
        in_specs=in_specs,
        out_specs=out_specs,
        out_shape=out_shape,
        scratch_shapes=[pltpu.VMEM(((POOL_HALO + SUBLANES) * seqs, POOL_WIDTH), _F32),
                        pltpu.VMEM(((CONV_BUF + SUBLANES) * seqs, CONV_WIDTH), _F32)],
        compiler_params=pltpu.CompilerParams(
            dimension_semantics=("arbitrary",),
            vmem_limit_bytes=VMEM_LIMIT_BYTES),
        name=f"sample_layer{layer}",
    )(x, p, state_pool, state_conv, *weights, *prev_state)


def kernel(x_prompt, x_sample, state_pool, state_conv, p_prompt, p_sample, norm_mix, w_in, pool_w, pool_scale, conv_w, conv_b, w_out, norm_mlp, w_up, w_down, norm_ple, w_ple_gate, w_ple_proj, norm_f):
    assert DEPTH == 2 and PAST_LEN >= POOL_BUF
    shared = (norm_mix, pool_scale, jnp.swapaxes(conv_w, 0, 1), conv_b, norm_mlp, norm_ple,
              norm_f.reshape(1, D_MODEL), pool_w, w_ple_proj)
    big_f32 = (w_in, w_out, w_up, w_down, w_ple_gate)

    n_big = len(big_f32)
    h, pool_p, conv_p, *big_bf16 = _prompt_layer(0, x_prompt, p_prompt, shared, (), big_f32, big_f32)
    weights = [_Weights(*shared, *big_bf16[n_big:]), _Weights(*shared, *big_bf16[:n_big])]
    y_prompt, pool_p, conv_p = _prompt_layer(1, h, p_prompt, weights[1], (pool_p, conv_p), ())

    assert x_sample.shape[1] == SUBLANES
    pool_hist = jnp.swapaxes(state_pool, 1, 2)
    h, pool_s, conv_s = _sample_layer(0, x_sample, p_sample, pool_hist, state_conv, weights[0], ())
    y_sample, pool_s, conv_s = _sample_layer(1, h, p_sample, pool_hist, state_conv, weights[1],
                                             (pool_s, conv_s))
    return (y_prompt, y_sample, jnp.swapaxes(pool_p, 1, 2), conv_p, jnp.swapaxes(pool_s, 1, 2), conv_s)
```

```python
import functools
from typing import NamedTuple

import jax
import jax.numpy as jnp
from jax import lax
from jax.experimental import pallas as pl
from jax.experimental.pallas import tpu as pltpu

D_MODEL = 1024
DEPTH = 2
PAST_LEN = 16384
POOL_WIDTH = 512
CONV_WIDTH = 512
POOL_WINDOWS = (2, 4, 8, 16)
POOL_GC = 128
POOL_BUF = 15
CONV_K = 3
CONV_BUF = 2
D_FF = 4 * D_MODEL
PLE_DIM = 256
IN_COLS = POOL_WIDTH + 3 * CONV_WIDTH
EPS = 1e-6

SUBLANES = 8
BF16_ROWS = 16
POOL_HALO = POOL_BUF + 1
TOKEN_TILE = 512
PROMPT_FF_CHUNK = 1024
SAMPLE_FF_CHUNK = 1024
VMEM_LIMIT_BYTES = 56 * 1024 * 1024

_BF16 = jnp.bfloat16
_F32 = jnp.float32


class _Weights(NamedTuple):
    norm_mix: jax.Array
    pool_scale: jax.Array
    conv_w: jax.Array
    conv_b: jax.Array
    norm_mlp: jax.Array
    norm_ple: jax.Array
    norm_f: jax.Array
    pool_w: jax.Array
    w_proj: jax.Array
    w_in: jax.Array
    w_out: jax.Array
    w_up: jax.Array
    w_down: jax.Array
    w_gate: jax.Array


_BIG_FIELDS = _Weights._fields[-5:]
CAST_STAGE = (512, 1024)


def _rmsnorm(x, g):
    ms = jnp.mean(x * x, axis=-1, keepdims=True)
    return (x * lax.rsqrt(ms + EPS)) * g


def _dot(a, b):
    return jnp.dot(a, b, preferred_element_type=_F32)


def _norm_split(x, g):
    scale = lax.rsqrt(jnp.mean(x * x, axis=-1, keepdims=True) + EPS)
    return (x * g).astype(_BF16), scale


def _in_proj(h, w, layer):
    hg, scale = _norm_split(h, w.norm_mix[layer:layer + 1, :])
    proj = _dot(hg, w.w_in[...])
    u = proj[:, :POOL_WIDTH] * scale
    bg = proj[:, POOL_WIDTH:POOL_WIDTH + CONV_WIDTH] * scale
    cg = proj[:, POOL_WIDTH + CONV_WIDTH:POOL_WIDTH + 2 * CONV_WIDTH] * scale
    v = proj[:, POOL_WIDTH + 2 * CONV_WIDTH:] * scale
    return u, bg, cg * v


def _pool_out(ds, w, layer):
    outs = []
    for k in range(0, len(POOL_WINDOWS), 2):
        a = w.pool_w[layer, k].astype(_BF16)
        b = w.pool_w[layer, k + 1].astype(_BF16)
        zero = jnp.zeros_like(a)
        both = jnp.concatenate([jnp.concatenate([a, zero], axis=1),
                                jnp.concatenate([zero, b], axis=1)], axis=0)
        outs.append(_dot(jnp.concatenate(ds[k:k + 2], axis=-1), both))
    return jnp.concatenate(outs, axis=-1) * w.pool_scale[layer:layer + 1, :]


def _conv_out(w, layer, back2, back1, cur):
    tap = lambda k: w.conv_w[k, layer:layer + 1, :]
    return w.conv_b[layer:layer + 1, :] + (tap(0) * back2 + tap(1) * back1 + tap(2) * cur)


def _out_proj(h, y_pool, y_conv, w):
    mix = jnp.concatenate([y_pool, y_conv], axis=-1).astype(_BF16)
    return h + _dot(mix, w.w_out[...])


def _mlp(h, w, layer, chunk):
    hg, scale = _norm_split(h, w.norm_mlp[layer:layer + 1, :])
    acc = h
    for c in range(D_FF // chunk):
        cs = slice(c * chunk, (c + 1) * chunk)
        f = jnp.maximum(_dot(hg, w.w_up[:, cs]) * scale, 0.0)
        acc = acc + _dot((f * f).astype(_BF16), w.w_down[cs, :])
    return acc


def _gated_embed(h, p, w, layer):
    hg, scale = _norm_split(h, w.norm_ple[layer:layer + 1, :])
    gate = jax.nn.sigmoid(_dot(hg, w.w_gate[...]) * scale)
    h = h + gate * _dot(p.astype(_BF16), w.w_proj[layer].astype(_BF16))
    if layer == DEPTH - 1:
        h = _rmsnorm(h, w.norm_f[...])
    return h


def _prompt_kernel(*refs, layer, tile, n_tiles, n_cast, own_cast):
    n_big = len(_BIG_FIELDS)
    n_w = len(_Weights._fields) - (n_big if own_cast else 0)
    x_ref, p_ref = refs[:2]
    pos = 2 + n_w
    prev = refs[pos:pos + 2 * layer]
    pos += 2 * layer
    cast_in = refs[pos:pos + n_cast]
    pos += n_cast
    if own_cast:
        big_hbm = refs[pos:pos + n_big]
        pos += n_big
    out_ref, npool_ref, nconv_ref = refs[pos:pos + 3]
    cast_out = refs[pos + 3:pos + 3 + n_cast]
    pos += 3 + n_cast
    if own_cast:
        big_out = refs[pos:pos + n_big]
        pool_ext, conv_ext = refs[pos + n_big:pos + n_big + 2]
        big_vmem = refs[pos + n_big + 2:pos + 2 * n_big + 2]
        stage, fetch_sem, publish_sem = refs[pos + 2 * n_big + 2:]
        w = _Weights(*refs[2:2 + n_w], *big_vmem)
    else:
        pool_ext, conv_ext = refs[pos:]
        w = _Weights(*refs[2:2 + n_w])
    i = pl.program_id(1)
    halo = POOL_HALO
    chalo = SUBLANES

    if own_cast:
        first = (pl.program_id(0) == 0) & (i == 0)
        last = (pl.program_id(0) == pl.num_programs(0) - 1) & (i == n_tiles - 1)
        rows, cols = stage.shape[1:]
        chunks = [(m, r0, c0) for m in range(n_big)
                  for r0 in range(0, big_vmem[m].shape[0], rows)
                  for c0 in range(0, big_vmem[m].shape[1], cols)]

        def fetch(k):
            m, r0, c0 = chunks[k]
            return pltpu.make_async_copy(big_hbm[m].at[layer, pl.ds(r0, rows), pl.ds(c0, cols)],
                                         stage.at[k % 2], fetch_sem.at[k % 2])

        def publish(m):
            return pltpu.make_async_copy(big_vmem[m], big_out[m], publish_sem.at[m])

        @pl.when(first)
        def _():
            fetch(0).start()
            for k, (m, r0, c0) in enumerate(chunks):
                if k + 1 < len(chunks):
                    fetch(k + 1).start()
                fetch(k).wait()
                big_vmem[m][r0:r0 + rows, c0:c0 + cols] = stage[k % 2].astype(_BF16)
            for m in range(n_big):
                publish(m).start()

        @pl.when(last)
        def _():
            for m in range(n_big):
                publish(m).wait()

    @pl.when(i == 0)
    def _():
        pool_ext[0:halo, :] = jnp.zeros((halo, POOL_WIDTH), _F32)
        conv_ext[0:chalo, :] = jnp.zeros((chalo, CONV_WIDTH), _F32)

    h = x_ref[...]
    u, bg, cv = _in_proj(h, w, layer)
    pool_ext[halo:halo + tile, :] = u
    conv_ext[chalo:chalo + tile, :] = cv

    for src, dst in zip(cast_in, cast_out):
        dst[...] = src[...].astype(_BF16)

    position = i * tile + lax.broadcasted_iota(jnp.int32, (tile, 1), 0)
    ds = []
    for g, win in enumerate(POOL_WINDOWS):
        e = pool_ext[:, g * POOL_GC:(g + 1) * POOL_GC]
        s = e
        span = 1
        while span < win:
            s = s + pltpu.roll(s, span, axis=0)
            span *= 2
        cnt = jnp.minimum(win, position + 1).astype(_F32)
        ds.append((s[halo:, :] / cnt - e[halo:, :]).astype(_BF16))
    y_pool = _pool_out(ds, w, layer)
    cvx = conv_ext[...]
    z = _conv_out(w, layer, pltpu.roll(cvx, 2, axis=0)[chalo:, :],
                  pltpu.roll(cvx, 1, axis=0)[chalo:, :], cvx[chalo:, :])

    h = _mlp(_out_proj(h, y_pool, bg * z, w), w, layer, PROMPT_FF_CHUNK)
    out_ref[...] = _gated_embed(h, p_ref[...], w, layer)

    @pl.when(i == n_tiles - 1)
    def _():
        seq = pl.ds(pl.program_id(0), 1)
        new_pool = pool_ext[tile + halo - POOL_BUF:tile + halo, :][:, None, :]
        new_conv = conv_ext[tile + chalo - CONV_BUF:tile + chalo, :]
        if layer == 0:
            npool_ref[:, seq, :] = new_pool
            nconv_ref[...] = new_conv
        else:
            for k in range(layer):
                npool_ref[k] = prev[2 * k][...]
                nconv_ref[k] = prev[2 * k + 1][...]
            npool_ref[layer, :, seq, :] = new_pool
            nconv_ref[layer] = new_conv

    @pl.when(i < n_tiles - 1)
    def _():
        pool_ext[0:halo, :] = pool_ext[tile:tile + halo, :]
        conv_ext[0:chalo, :] = conv_ext[tile:tile + chalo, :]


def _sample_kernel(*refs, layer, tile):
    n_w = len(_Weights._fields)
    x_ref, p_ref, pool_hist, conv_hist = refs[:4]
    w = _Weights(*refs[4:4 + n_w])
    prev = refs[4 + n_w:4 + n_w + 2 * layer]
    out_ref, npool_ref, nconv_ref, pool_ext, conv_ext = refs[4 + n_w + 2 * layer:]
    seqs = tile // SUBLANES
    halo = POOL_HALO * seqs
    chalo = CONV_BUF * seqs
    positions = range(SUBLANES)
    by_position = lambda ref: jnp.concatenate([ref[:, t, :] for t in positions], axis=0)

    h = by_position(x_ref) if layer == 0 else x_ref[...].reshape(tile, D_MODEL)
    u, bg, cv = _in_proj(h, w, layer)

    pool_ext[0:seqs, :] = pool_hist[0]
    pool_ext[seqs:halo, :] = pool_hist[...].reshape(POOL_BUF * seqs, POOL_WIDTH)
    pool_ext[halo:halo + tile, :] = u
    for k in range(CONV_BUF):
        conv_ext[k * seqs:(k + 1) * seqs, :] = conv_hist[:, k, :]
    conv_ext[chalo:chalo + tile, :] = cv

    ds = []
    for g, win in enumerate(POOL_WINDOWS):
        sl = slice(g * POOL_GC, (g + 1) * POOL_GC)
        s = pool_ext[halo - (win - 1) * seqs:halo + tile, sl]
        span = 1
        while span < win:
            s = s[span * seqs:, :] + s[:-span * seqs, :]
            span *= 2
        ds.append((s * (1.0 / win) - u[:, sl]).astype(_BF16))
    y_pool = _pool_out(ds, w, layer)

    z = _conv_out(w, layer, conv_ext[0:tile, :], conv_ext[seqs:seqs + tile, :], cv)

    h = _mlp(_out_proj(h, y_pool, bg * z, w), w, layer, SAMPLE_FF_CHUNK)
    h = _gated_embed(h, by_position(p_ref), w, layer)
    if layer == DEPTH - 1:
        for t in positions:
            out_ref[:, t, :] = h[t * seqs:(t + 1) * seqs, :]
    else:
        out_ref[...] = h.reshape(SUBLANES, seqs, D_MODEL)

    pool_dst = npool_ref if layer == 0 else npool_ref.at[layer]
    conv_dst = nconv_ref if layer == 0 else nconv_ref.at[layer]
    pool_dst[...] = pool_ext[halo + tile - POOL_BUF * seqs:halo + tile, :].reshape(
        POOL_BUF, seqs, POOL_WIDTH)
    for k in range(CONV_BUF):
        conv_dst[:, k, :] = conv_ext[tile + k * seqs:tile + (k + 1) * seqs, :]
    for k in range(layer):
        npool_ref[k] = prev[2 * k][...]
        nconv_ref[k] = prev[2 * k + 1][...]


def _weight_specs(weights):
    return [pl.BlockSpec(a.shape, lambda *_, nd=a.ndim: (0,) * nd, pipeline_mode=pl.Buffered(1))
            for a in weights]


def _prompt_layer(layer, x, p, weights, prev_state, cast_next, cast_own=()):
    batch, seq, _ = x.shape
    tile = TOKEN_TILE
    n_tiles = seq // tile
    steps = batch * n_tiles
    assert seq == n_tiles * tile and tile >= POOL_HALO and len(prev_state) == 2 * layer
    for a in cast_next:
        assert a.shape[1] % (steps * BF16_ROWS) == 0
    for a in cast_own:
        assert a.shape[1] % CAST_STAGE[0] == 0 and a.shape[2] % CAST_STAGE[1] == 0
    assert len(weights) + len(cast_own) == len(_Weights._fields)
    hbm = pl.BlockSpec(memory_space=pl.ANY)
    whole = lambda shape: pl.BlockSpec(shape, lambda b, i, nd=len(shape): (0,) * nd)
    per_batch = pl.BlockSpec((None, CONV_BUF, CONV_WIDTH), lambda b, i: (b, 0, 0))
    pool_shape = (POOL_BUF, batch, POOL_WIDTH)
    in_specs = [
        pl.BlockSpec((None, tile, D_MODEL), lambda b, i: (b, i, 0)),
        pl.BlockSpec((None, None, tile, PLE_DIM), lambda b, i: (layer, b, i, 0)),
    ] + _weight_specs(weights) + [whole(pool_shape), per_batch] * layer + [
        pl.BlockSpec((None, a.shape[1] // steps, a.shape[2]),
                     lambda b, i: (layer + 1, b * n_tiles + i, 0)) for a in cast_next] + [
        hbm] * len(cast_own)
    out_specs = [pl.BlockSpec((None, tile, D_MODEL), lambda b, i: (b, i, 0))]
    out_shape = [jax.ShapeDtypeStruct((batch, seq, D_MODEL), _F32)]
    if layer == 0:
        out_specs += [whole(pool_shape), per_batch]
        out_shape += [jax.ShapeDtypeStruct(pool_shape, _F32),
                      jax.ShapeDtypeStruct((batch, CONV_BUF, CONV_WIDTH), _F32)]
    else:
        out_specs += [whole((layer + 1,) + pool_shape),
                      pl.BlockSpec((layer + 1, None, CONV_BUF, CONV_WIDTH), lambda b, i: (0, b, 0, 0))]
        out_shape += [jax.ShapeDtypeStruct((layer + 1,) + pool_shape, _F32),
                      jax.ShapeDtypeStruct((layer + 1, batch, CONV_BUF, CONV_WIDTH), _F32)]
    out_specs += [pl.BlockSpec((a.shape[1] // steps, a.shape[2]), lambda b, i: (b * n_tiles + i, 0))
                  for a in cast_next]
    out_shape += [jax.ShapeDtypeStruct(a.shape[1:], _BF16) for a in cast_next]
    out_specs += [hbm] * len(cast_own)
    out_shape += [jax.ShapeDtypeStruct(a.shape[1:], _BF16) for a in cast_own]
    scratch_shapes = [pltpu.VMEM((POOL_HALO + tile, POOL_WIDTH), _F32),
                      pltpu.VMEM((SUBLANES + tile, CONV_WIDTH), _F32)]
    if cast_own:
        scratch_shapes += [pltpu.VMEM(a.shape[1:], _BF16) for a in cast_own] + [
            pltpu.VMEM((2,) + CAST_STAGE, _F32),
            pltpu.SemaphoreType.DMA((2,)), pltpu.SemaphoreType.DMA((len(cast_own),))]
    return pl.pallas_call(
        functools.partial(_prompt_kernel, layer=layer, tile=tile, n_tiles=n_tiles,
                          n_cast=len(cast_next), own_cast=bool(cast_own)),
        grid=(batch, n_tiles),
        in_specs=in_specs,
        out_specs=out_specs,
        out_shape=out_shape,
        scratch_shapes=scratch_shapes,
        compiler_params=pltpu.CompilerParams(
            dimension_semantics=("arbitrary", "arbitrary"),
            vmem_limit_bytes=VMEM_LIMIT_BYTES),
        name=f"prompt_layer{layer}",
    )(x, p, *weights, *prev_state, *cast_next, *cast_own)


def _sample_layer(layer, x, p, state_pool, state_conv, weights, prev_state):
    n_seq = p.shape[1]
    seqs = TOKEN_TILE // SUBLANES
    tile = seqs * SUBLANES
    n_tiles = n_seq // seqs
    assert n_seq == n_tiles * seqs and p.shape[2] == SUBLANES and len(prev_state) == 2 * layer
    once = pl.Buffered(1)
    pool_rows = lambda **kw: pl.BlockSpec((POOL_BUF, seqs, POOL_WIDTH), lambda i: (0, i, 0), **kw)
    conv_rows = lambda **kw: pl.BlockSpec((seqs, CONV_BUF, CONV_WIDTH), lambda i: (i, 0, 0), **kw)
    by_sequence = pl.BlockSpec((seqs, SUBLANES, D_MODEL), lambda i: (i, 0, 0))
    by_position = pl.BlockSpec((SUBLANES, seqs, D_MODEL), lambda i: (0, i, 0))
    last = layer == DEPTH - 1
    in_specs = [
        by_sequence if layer == 0 else by_position,
        pl.BlockSpec((None, seqs, SUBLANES, PLE_DIM), lambda i: (layer, i, 0, 0)),
        pl.BlockSpec((None, POOL_BUF, seqs, POOL_WIDTH), lambda i: (layer, 0, i, 0), pipeline_mode=once),
        pl.BlockSpec((None, seqs, CONV_BUF, CONV_WIDTH), lambda i: (layer, i, 0, 0), pipeline_mode=once),
    ] + _weight_specs(weights) + [pool_rows(pipeline_mode=once), conv_rows(pipeline_mode=once)] * layer
    out_specs = [by_sequence if last else by_position]
    out_shape = [jax.ShapeDtypeStruct((n_seq, SUBLANES, D_MODEL) if last else
                                      (SUBLANES, n_seq, D_MODEL), _F32)]
    if layer == 0:
        out_specs += [pool_rows(), conv_rows()]
        out_shape += [jax.ShapeDtypeStruct((POOL_BUF, n_seq, POOL_WIDTH), _F32),
                      jax.ShapeDtypeStruct((n_seq, CONV_BUF, CONV_WIDTH), _F32)]
    else:
        out_specs += [pl.BlockSpec((layer + 1, POOL_BUF, seqs, POOL_WIDTH), lambda i: (0, 0, i, 0)),
                      pl.BlockSpec((layer + 1, seqs, CONV_BUF, CONV_WIDTH), lambda i: (0, i, 0, 0))]
        out_shape += [jax.ShapeDtypeStruct((layer + 1, POOL_BUF, n_seq, POOL_WIDTH), _F32),
                      jax.ShapeDtypeStruct((layer + 1, n_seq, CONV_BUF, CONV_WIDTH), _F32)]
    return pl.pallas_call(
        functools.partial(_sample_kernel, layer=layer, tile=tile),
        grid=(n_tiles,),
        in_specs=in_specs,
        out_specs=out_specs,
        out_shape=out_shape,
        scratch_shapes=[pltpu.VMEM(((POOL_HALO + SUBLANES) * seqs, POOL_WIDTH), _F32),
                        pltpu.VMEM(((CONV_BUF + SUBLANES) * seqs, CONV_WIDTH), _F32)],
        compiler_params=pltpu.CompilerParams(
            dimension_semantics=("arbitrary",),
            vmem_limit_bytes=VMEM_LIMIT_BYTES),
        name=f"sample_layer{layer}",
    )(x, p, state_pool, state_conv, *weights, *prev_state)


def kernel(x_prompt, x_sample, state_pool, state_conv, p_prompt, p_sample, norm_mix, w_in, pool_w, pool_scale, conv_w, conv_b, w_out, norm_mlp, w_up, w_down, norm_ple, w_ple_gate, w_ple_proj, norm_f):
    assert DEPTH == 2 and PAST_LEN >= POOL_BUF
    shared = (norm_mix, pool_scale, jnp.swapaxes(conv_w, 0, 1), conv_b, norm_mlp, norm_ple,
              norm_f.reshape(1, D_MODEL), pool_w, w_ple_proj)
    big_f32 = (w_in, w_out, w_up, w_down, w_ple_gate)

    n_big = len(big_f32)
    h, pool_p, conv_p, *big_bf16 = _prompt_layer(0, x_prompt, p_prompt, shared, (), big_f32, big_f32)
    weights = [_Weights(*shared, *big_bf16[n_big:]), _Weights(*shared, *big_bf16[:n_big])]
    y_prompt, pool_p, conv_p = _prompt_layer(1, h, p_prompt, weights[1], (pool_p, conv_p), ())

    assert x_sample.shape[1] == SUBLANES
    pool_hist = jnp.swapaxes(state_pool, 1, 2)
    h, pool_s, conv_s = _sample_layer(0, x_sample, p_sample, pool_hist, state_conv, weights[0], ())
    y_sample, pool_s, conv_s = _sample_layer(1, h, p_sample, pool_hist, state_conv, weights[1],
                                             (pool_s, conv_s))
    return (y_prompt, y_sample, jnp.swapaxes(pool_p, 1, 2), conv_p, jnp.swapaxes(pool_s, 1, 2), conv_s)
```

```python
import functools
from typing import NamedTuple

import jax
import jax.numpy as jnp
from jax import lax
from jax.experimental import pallas as pl
from jax.experimental.pallas import tpu as pltpu

D_MODEL = 1024
DEPTH = 2
PAST_LEN = 16384
POOL_WIDTH = 512
CONV_WIDTH = 512
POOL_WINDOWS = (2, 4, 8, 16)
POOL_GC = 128
POOL_BUF = 15
CONV_K = 3
CONV_BUF = 2
D_FF = 4 * D_MODEL
PLE_DIM = 256
IN_COLS = POOL_WIDTH + 3 * CONV_WIDTH
EPS = 1e-6

SUBLANES = 8
BF16_ROWS = 16
POOL_HALO = POOL_BUF + 1
TOKEN_TILE = 512
PROMPT_FF_CHUNK = 2048
SAMPLE_FF_CHUNK = 1024
VMEM_LIMIT_BYTES = 56 * 1024 * 1024

_BF16 = jnp.bfloat16
_F32 = jnp.float32


class _Weights(NamedTuple):
    norm_mix: jax.Array
    pool_scale: jax.Array
    conv_w: jax.Array
    conv_b: jax.Array
    norm_mlp: jax.Array
    norm_ple: jax.Array
    norm_f: jax.Array
    pool_w: jax.Array
    w_proj: jax.Array
    w_in: jax.Array
    w_out: jax.Array
    w_up: jax.Array
    w_down: jax.Array
    w_gate: jax.Array


_BIG_FIELDS = _Weights._fields[-5:]
CAST_STAGE = (512, 1024)


def _rmsnorm(x, g):
    ms = jnp.mean(x * x, axis=-1, keepdims=True)
    return (x * lax.rsqrt(ms + EPS)) * g


def _dot(a, b):
    return jnp.dot(a, b, preferred_element_type=_F32)


def _norm_split(x, g):
    scale = lax.rsqrt(jnp.mean(x * x, axis=-1, keepdims=True) + EPS)
    return (x * g).astype(_BF16), scale


def _in_proj(h, w, layer):
    hg, scale = _norm_split(h, w.norm_mix[layer:layer + 1, :])
    proj = _dot(hg, w.w_in[...])
    u = proj[:, :POOL_WIDTH] * scale
    bg = proj[:, POOL_WIDTH:POOL_WIDTH + CONV_WIDTH] * scale
    cg = proj[:, POOL_WIDTH + CONV_WIDTH:POOL_WIDTH + 2 * CONV_WIDTH] * scale
    v = proj[:, POOL_WIDTH + 2 * CONV_WIDTH:] * scale
    return u, bg, cg * v


def _pool_out(ds, w, layer):
    outs = []
    for k in range(0, len(POOL_WINDOWS), 2):
        a = w.pool_w[layer, k].astype(_BF16)
        b = w.pool_w[layer, k + 1].astype(_BF16)
        zero = jnp.zeros_like(a)
        both = jnp.concatenate([jnp.concatenate([a, zero], axis=1),
                                jnp.concatenate([zero, b], axis=1)], axis=0)
        outs.append(_dot(jnp.concatenate(ds[k:k + 2], axis=-1), both))
    return jnp.concatenate(outs, axis=-1) * w.pool_scale[layer:layer + 1, :]


def _conv_out(w, layer, back2, back1, cur):
    tap = lambda k: w.conv_w[k, layer:layer + 1, :]
    return w.conv_b[layer:layer + 1, :] + (tap(0) * back2 + tap(1) * back1 + tap(2) * cur)


def _out_proj(h, y_pool, y_conv, w):
    mix = jnp.concatenate([y_pool, y_conv], axis=-1).astype(_BF16)
    return h + _dot(mix, w.w_out[...])


def _mlp(h, w, layer, chunk):
    hg, scale = _norm_split(h, w.norm_mlp[layer:layer + 1, :])
    acc = h
    for c in range(D_FF // chunk):
        cs = slice(c * chunk, (c + 1) * chunk)
        f = jnp.maximum(_dot(hg, w.w_up[:, cs]) * scale, 0.0)
        acc = acc + _dot((f * f).astype(_BF16), w.w_down[cs, :])
    return acc


def _gated_embed(h, p, w, layer):
    hg, scale = _norm_split(h, w.norm_ple[layer:layer + 1, :])
    gate = jax.nn.sigmoid(_dot(hg, w.w_gate[...]) * scale)
    h = h + gate * _dot(p.astype(_BF16), w.w_proj[layer].astype(_BF16))
    if layer == DEPTH - 1:
        h = _rmsnorm(h, w.norm_f[...])
    return h


def _prompt_kernel(*refs, layer, tile, n_tiles, n_cast, own_cast):
    n_big = len(_BIG_FIELDS)
    n_w = len(_Weights._fields) - (n_big if own_cast else 0)
    x_ref, p_ref = refs[:2]
    pos = 2 + n_w
    prev = refs[pos:pos + 2 * layer]
    pos += 2 * layer
    cast_in = refs[pos:pos + n_cast]
    pos += n_cast
    if own_cast:
        big_hbm = refs[pos:pos + n_big]
        pos += n_big
    out_ref, npool_ref, nconv_ref = refs[pos:pos + 3]
    cast_out = refs[pos + 3:pos + 3 + n_cast]
    pos += 3 + n_cast
    if own_cast:
        big_out = refs[pos:pos + n_big]
        pool_ext, conv_ext = refs[pos + n_big:pos + n_big + 2]
        big_vmem = refs[pos + n_big + 2:pos + 2 * n_big + 2]
        stage, fetch_sem, publish_sem = refs[pos + 2 * n_big + 2:]
        w = _Weights(*refs[2:2 + n_w], *big_vmem)
    else:
        pool_ext, conv_ext = refs[pos:]
        w = _Weights(*refs[2:2 + n_w])
    i = pl.program_id(1)
    halo = POOL_HALO
    chalo = SUBLANES

    if own_cast:
        first = (pl.program_id(0) == 0) & (i == 0)
        last = (pl.program_id(0) == pl.num_programs(0) - 1) & (i == n_tiles - 1)
        rows, cols = stage.shape[1:]
        chunks = [(m, r0, c0) for m in range(n_big)
                  for r0 in range(0, big_vmem[m].shape[0], rows)
                  for c0 in range(0, big_vmem[m].shape[1], cols)]

        def fetch(k):
            m, r0, c0 = chunks[k]
            return pltpu.make_async_copy(big_hbm[m].at[layer, pl.ds(r0, rows), pl.ds(c0, cols)],
                                         stage.at[k % 2], fetch_sem.at[k % 2])

        def publish(m):
            return pltpu.make_async_copy(big_vmem[m], big_out[m], publish_sem.at[m])

        @pl.when(first)
        def _():
            fetch(0).start()
            for k, (m, r0, c0) in enumerate(chunks):
                if k + 1 < len(chunks):
                    fetch(k + 1).start()
                fetch(k).wait()
                big_vmem[m][r0:r0 + rows, c0:c0 + cols] = stage[k % 2].astype(_BF16)
            for m in range(n_big):
                publish(m).start()

        @pl.when(last)
        def _():
            for m in range(n_big):
                publish(m).wait()

    @pl.when(i == 0)
    def _():
        pool_ext[0:halo, :] = jnp.zeros((halo, POOL_WIDTH), _F32)
        conv_ext[0:chalo, :] = jnp.zeros((chalo, CONV_WIDTH), _F32)

    h = x_ref[...]
    u, bg, cv = _in_proj(h, w, layer)
    pool_ext[halo:halo + tile, :] = u
    conv_ext[chalo:chalo + tile, :] = cv

    for src, dst in zip(cast_in, cast_out):
        dst[...] = src[...].astype(_BF16)

    position = i * tile + lax.broadcasted_iota(jnp.int32, (tile, 1), 0)
    ds = []
    for g, win in enumerate(POOL_WINDOWS):
        e = pool_ext[:, g * POOL_GC:(g + 1) * POOL_GC]
        s = e
        span = 1
        while span < win:
            s = s + pltpu.roll(s, span, axis=0)
            span *= 2
        cnt = jnp.minimum(win, position + 1).astype(_F32)
        ds.append((s[halo:, :] / cnt - e[halo:, :]).astype(_BF16))
    y_pool = _pool_out(ds, w, layer)
    cvx = conv_ext[...]
    z = _conv_out(w, layer, pltpu.roll(cvx, 2, axis=0)[chalo:, :],
                  pltpu.roll(cvx, 1, axis=0)[chalo:, :], cvx[chalo:, :])

    h = _mlp(_out_proj(h, y_pool, bg * z, w), w, layer, PROMPT_FF_CHUNK)
    out_ref[...] = _gated_embed(h, p_ref[...], w, layer)

    @pl.when(i == n_tiles - 1)
    def _():
        seq = pl.ds(pl.program_id(0), 1)
        new_pool = pool_ext[tile + halo - POOL_BUF:tile + halo, :][:, None, :]
        new_conv = conv_ext[tile + chalo - CONV_BUF:tile + chalo, :]
        if layer == 0:
            npool_ref[:, seq, :] = new_pool
            nconv_ref[...] = new_conv
        else:
            for k in range(layer):
                npool_ref[k] = prev[2 * k][...]
                nconv_ref[k] = prev[2 * k + 1][...]
            npool_ref[layer, :, seq, :] = new_pool
            nconv_ref[layer] = new_conv

    @pl.when(i < n_tiles - 1)
    def _():
        pool_ext[0:halo, :] = pool_ext[tile:tile + halo, :]
        conv_ext[0:chalo, :] = conv_ext[tile:tile + chalo, :]


def _sample_kernel(*refs, layer, tile):
    n_w = len(_Weights._fields)
    x_ref, p_ref, pool_hist, conv_hist = refs[:4]
    w = _Weights(*refs[4:4 + n_w])
    prev = refs[4 + n_w:4 + n_w + 2 * layer]
    out_ref, npool_ref, nconv_ref, pool_ext, conv_ext = refs[4 + n_w + 2 * layer:]
    seqs = tile // SUBLANES
    halo = POOL_HALO * seqs
    chalo = CONV_BUF * seqs
    positions = range(SUBLANES)
    by_position = lambda ref: jnp.concatenate([ref[:, t, :] for t in positions], axis=0)

    h = by_position(x_ref) if layer == 0 else x_ref[...].reshape(tile, D_MODEL)
    u, bg, cv = _in_proj(h, w, layer)

    pool_ext[0:seqs, :] = pool_hist[0]
    pool_ext[seqs:halo, :] = pool_hist[...].reshape(POOL_BUF * seqs, POOL_WIDTH)
    pool_ext[halo:halo + tile, :] = u
    for k in range(CONV_BUF):
        conv_ext[k * seqs:(k + 1) * seqs, :] = conv_hist[:, k, :]
    conv_ext[chalo:chalo + tile, :] = cv

    ds = []
    for g, win in enumerate(POOL_WINDOWS):
        sl = slice(g * POOL_GC, (g + 1) * POOL_GC)
        s = pool_ext[halo - (win - 1) * seqs:halo + tile, sl]
        span = 1
        while span < win:
            s = s[span * seqs:, :] + s[:-span * seqs, :]
            span *= 2
        ds.append((s * (1.0 / win) - u[:, sl]).astype(_BF16))
    y_pool = _pool_out(ds, w, layer)

    z = _conv_out(w, layer, conv_ext[0:tile, :], conv_ext[seqs:seqs + tile, :], cv)

    h = _mlp(_out_proj(h, y_pool, bg * z, w), w, layer, SAMPLE_FF_CHUNK)
    h = _gated_embed(h, by_position(p_ref), w, layer)
    if layer == DEPTH - 1:
        for t in positions:
            out_ref[:, t, :] = h[t * seqs:(t + 1) * seqs, :]
    else:
        out_ref[...] = h.reshape(SUBLANES, seqs, D_MODEL)

    pool_dst = npool_ref if layer == 0 else npool_ref.at[layer]
    conv_dst = nconv_ref if layer == 0 else nconv_ref.at[layer]
    pool_dst[...] = pool_ext[halo + tile - POOL_BUF * seqs:halo + tile, :].reshape(
        POOL_BUF, seqs, POOL_WIDTH)
    for k in range(CONV_BUF):
        conv_dst[:, k, :] = conv_ext[tile + k * seqs:tile + (k + 1) * seqs, :]
    for k in range(layer):
        npool_ref[k] = prev[2 * k][...]
        nconv_ref[k] = prev[2 * k + 1][...]


def _weight_specs(weights):
    return [pl.BlockSpec(a.shape, lambda *_, nd=a.ndim: (0,) * nd, pipeline_mode=pl.Buffered(1))
            for a in weights]


def _prompt_layer(layer, x, p, weights, prev_state, cast_next, cast_own=()):
    batch, seq, _ = x.shape
    tile = TOKEN_TILE
    n_tiles = seq // tile
    steps = batch * n_tiles
    assert seq == n_tiles * tile and tile >= POOL_HALO and len(prev_state) == 2 * layer
    for a in cast_next:
        assert a.shape[1] % (steps * BF16_ROWS) == 0
    for a in cast_own:
        assert a.shape[1] % CAST_STAGE[0] == 0 and a.shape[2] % CAST_STAGE[1] == 0
    assert len(weights) + len(cast_own) == len(_Weights._fields)
    hbm = pl.BlockSpec(memory_space=pl.ANY)
    whole = lambda shape: pl.BlockSpec(shape, lambda b, i, nd=len(shape): (0,) * nd)
    per_batch = pl.BlockSpec((None, CONV_BUF, CONV_WIDTH), lambda b, i: (b, 0, 0))
    pool_shape = (POOL_BUF, batch, POOL_WIDTH)
    in_specs = [
        pl.BlockSpec((None, tile, D_MODEL), lambda b, i: (b, i, 0)),
        pl.BlockSpec((None, None, tile, PLE_DIM), lambda b, i: (layer, b, i, 0)),
    ] + _weight_specs(weights) + [whole(pool_shape), per_batch] * layer + [
        pl.BlockSpec((None, a.shape[1] // steps, a.shape[2]),
                     lambda b, i: (layer + 1, b * n_tiles + i, 0)) for a in cast_next] + [
        hbm] * len(cast_own)
    out_specs = [pl.BlockSpec((None, tile, D_MODEL), lambda b, i: (b, i, 0))]
    out_shape = [jax.ShapeDtypeStruct((batch, seq, D_MODEL), _F32)]
    if layer == 0:
        out_specs += [whole(pool_shape), per_batch]
        out_shape += [jax.ShapeDtypeStruct(pool_shape, _F32),
                      jax.ShapeDtypeStruct((batch, CONV_BUF, CONV_WIDTH), _F32)]
    else:
        out_specs += [whole((layer + 1,) + pool_shape),
                      pl.BlockSpec((layer + 1, None, CONV_BUF, CONV_WIDTH), lambda b, i: (0, b, 0, 0))]
        out_shape += [jax.ShapeDtypeStruct((layer + 1,) + pool_shape, _F32),
                      jax.ShapeDtypeStruct((layer + 1, batch, CONV_BUF, CONV_WIDTH), _F32)]
    out_specs += [pl.BlockSpec((a.shape[1] // steps, a.shape[2]), lambda b, i: (b * n_tiles + i, 0))
                  for a in cast_next]
    out_shape += [jax.ShapeDtypeStruct(a.shape[1:], _BF16) for a in cast_next]
    out_specs += [hbm] * len(cast_own)
    out_shape += [jax.ShapeDtypeStruct(a.shape[1:], _BF16) for a in cast_own]
    scratch_shapes = [pltpu.VMEM((POOL_HALO + tile, POOL_WIDTH), _F32),
                      pltpu.VMEM((SUBLANES + tile, CONV_WIDTH), _F32)]
    if cast_own:
        scratch_shapes += [pltpu.VMEM(a.shape[1:], _BF16) for a in cast_own] + [
            pltpu.VMEM((2,) + CAST_STAGE, _F32),
            pltpu.SemaphoreType.DMA((2,)), pltpu.SemaphoreType.DMA((len(cast_own),))]
    return pl.pallas_call(
        functools.partial(_prompt_kernel, layer=layer, tile=tile, n_tiles=n_tiles,
                          n_cast=len(cast_next), own_cast=bool(cast_own)),
        grid=(batch, n_tiles),
        in_specs=in_specs,
        out_specs=out_specs,
        out_shape=out_shape,
        scratch_shapes=scratch_shapes,
        compiler_params=pltpu.CompilerParams(
            dimension_semantics=("arbitrary", "arbitrary"),
            vmem_limit_bytes=VMEM_LIMIT_BYTES),
        name=f"prompt_layer{layer}",
    )(x, p, *weights, *prev_state, *cast_next, *cast_own)


def _sample_layer(layer, x, p, state_pool, state_conv, weights, prev_state):
    n_seq = p.shape[1]
    seqs = TOKEN_TILE // SUBLANES
    tile = seqs * SUBLANES
    n_tiles = n_seq // seqs
    assert n_seq == n_tiles * seqs and p.shape[2] == SUBLANES and len(prev_state) == 2 * layer
    once = pl.Buffered(1)
    pool_rows = lambda **kw: pl.BlockSpec((POOL_BUF, seqs, POOL_WIDTH), lambda i: (0, i, 0), **kw)
    conv_rows = lambda **kw: pl.BlockSpec((seqs, CONV_BUF, CONV_WIDTH), lambda i: (i, 0, 0), **kw)
    by_sequence = pl.BlockSpec((seqs, SUBLANES, D_MODEL), lambda i: (i, 0, 0))
    by_position = pl.BlockSpec((SUBLANES, seqs, D_MODEL), lambda i: (0, i, 0))
    last = layer == DEPTH - 1
    in_specs = [
        by_sequence if layer == 0 else by_position,
        pl.BlockSpec((None, seqs, SUBLANES, PLE_DIM), lambda i: (layer, i, 0, 0)),
        pl.BlockSpec((None, POOL_BUF, seqs, POOL_WIDTH), lambda i: (layer, 0, i, 0), pipeline_mode=once),
        pl.BlockSpec((None, seqs, CONV_BUF, CONV_WIDTH), lambda i: (layer, i, 0, 0), pipeline_mode=once),
    ] + _weight_specs(weights) + [pool_rows(pipeline_mode=once), conv_rows(pipeline_mode=once)] * layer
    out_specs = [by_sequence if last else by_position]
    out_shape = [jax.ShapeDtypeStruct((n_seq, SUBLANES, D_MODEL) if last else
                                      (SUBLANES, n_seq, D_MODEL), _F32)]
    if layer == 0:
        out_specs += [pool_rows(), conv_rows()]
        out_shape += [jax.ShapeDtypeStruct((POOL_BUF, n_seq, POOL_WIDTH), _F32),
                      jax.ShapeDtypeStruct((n_seq, CONV_BUF, CONV_WIDTH), _F32)]
    else:
        out_specs += [pl.BlockSpec((layer + 1, POOL_BUF, seqs, POOL_WIDTH), lambda i: (0, 0, i, 0)),
                      pl.BlockSpec((layer + 1, seqs, CONV_BUF, CONV_WIDTH), lambda i: (0, i, 0, 0))]
        out_shape += [jax.ShapeDtypeStruct((layer + 1, POOL_BUF, n_seq, POOL_WIDTH), _F32),
                      jax.ShapeDtypeStruct((layer + 1, n_seq, CONV_BUF, CONV_WIDTH), _F32)]
    return pl.pallas_call(
        functools.partial(_sample_kernel, layer=layer, tile=tile),
        grid=(n_tiles,),
        in_specs=in_specs,
        out_specs=out_specs,
        out_shape=out_shape,
        scratch_shapes=[pltpu.VMEM(((POOL_HALO + SUBLANES) * seqs, POOL_WIDTH), _F32),
                        pltpu.VMEM(((CONV_BUF + SUBLANES) * seqs, CONV_WIDTH), _F32)],
        compiler_params=pltpu.CompilerParams(
            dimension_semantics=("arbitrary",),
            vmem_limit_bytes=VMEM_LIMIT_BYTES),
        name=f"sample_layer{layer}",
    )(x, p, state_pool, state_conv, *weights, *prev_state)


def kernel(x_prompt, x_sample, state_pool, state_conv, p_prompt, p_sample, norm_mix, w_in, pool_w, pool_scale, conv_w, conv_b, w_out, norm_mlp, w_up, w_down, norm_ple, w_ple_gate, w_ple_proj, norm_f):
    assert DEPTH == 2 and PAST_LEN >= POOL_BUF
    shared = (norm_mix, pool_scale, jnp.swapaxes(conv_w, 0, 1), conv_b, norm_mlp, norm_ple,
              norm_f.reshape(1, D_MODEL), pool_w, w_ple_proj)
    big_f32 = (w_in, w_out, w_up, w_down, w_ple_gate)

    n_big = len(big_f32)
    h, pool_p, conv_p, *big_bf16 = _prompt_layer(0, x_prompt, p_prompt, shared, (), big_f32, big_f32)
    weights = [_Weights(*shared, *big_bf16[n_big:]), _Weights(*shared, *big_bf16[:n_big])]
    y_prompt, pool_p, conv_p = _prompt_layer(1, h, p_prompt, weights[1], (pool_p, conv_p), ())

    assert x_sample.shape[1] == SUBLANES
    pool_hist = jnp.swapaxes(state_pool, 1, 2)
    h, pool_s, conv_s = _sample_layer(0, x_sample, p_sample, pool_hist, state_conv, weights[0], ())
    y_sample, pool_s, conv_s = _sample_layer(1, h, p_sample, pool_hist, state_conv, weights[1],
                                             (pool_s, conv_s))
    return (y_prompt, y_sample, jnp.swapaxes(pool_p, 1, 2), conv_p, jnp.swapaxes(pool_s, 1, 2), conv_s)
```

```python
import functools
from typing import NamedTuple

import jax
import jax.numpy as jnp
from jax import lax
from jax.experimental import pallas as pl
from jax.experimental.pallas import tpu as pltpu

D_MODEL = 1024
DEPTH = 2
PAST_LEN = 16384
POOL_WIDTH = 512
CONV_WIDTH = 512
POOL_WINDOWS = (2, 4, 8, 16)
POOL_GC = 128
POOL_BUF = 15
CONV_K = 3
CONV_BUF = 2
D_FF = 4 * D_MODEL
PLE_DIM = 256
IN_COLS = POOL_WIDTH + 3 * CONV_WIDTH
EPS = 1e-6

SUBLANES = 8
BF16_ROWS = 16
POOL_HALO = POOL_BUF + 1
TOKEN_TILE = 512
PROMPT_FF_CHUNK = 1024
SAMPLE_FF_CHUNK = 1024
VMEM_LIMIT_BYTES = 56 * 1024 * 1024

_BF16 = jnp.bfloat16
_F32 = jnp.float32


class _Weights(NamedTuple):
    norm_mix: jax.Array
    conv_w: jax.Array
    conv_b: jax.Array
    norm_mlp: jax.Array
    norm_ple: jax.Array
    norm_f: jax.Array
    w_proj: jax.Array
    w_mix: jax.Array
    w_in: jax.Array
    w_up: jax.Array
    w_down: jax.Array
    w_gate: jax.Array


_BIG_FIELDS = _Weights._fields[-4:]
_MIX_FIELD = _Weights._fields.index("w_mix")
CAST_STAGE = (512, 1024)


def _rmsnorm(x, g):
    ms = jnp.mean(x * x, axis=-1, keepdims=True)
    return (x * lax.rsqrt(ms + EPS)) * g


def _dot(a, b):
    return jnp.dot(a, b, preferred_element_type=_F32)


def _norm_split(x, g):
    scale = lax.rsqrt(jnp.mean(x * x, axis=-1, keepdims=True) + EPS)
    return (x * g).astype(_BF16), scale


def _in_proj(h, w, layer):
    hg, scale = _norm_split(h, w.norm_mix[layer:layer + 1, :])
    proj = _dot(hg, w.w_in[...])
    u = proj[:, :POOL_WIDTH] * scale
    bg = proj[:, POOL_WIDTH:POOL_WIDTH + CONV_WIDTH] * scale
    cg = proj[:, POOL_WIDTH + CONV_WIDTH:POOL_WIDTH + 2 * CONV_WIDTH] * scale
    v = proj[:, POOL_WIDTH + 2 * CONV_WIDTH:] * scale
    return u, bg, cg * v


def _fold_kernel(pool_w_ref, pool_scale_ref, w_out_ref, w_mix_ref):
    for layer in range(DEPTH):
        @pl.when(pl.program_id(0) == layer)
        def _():
            for g in range(len(POOL_WINDOWS)):
                sl = slice(g * POOL_GC, (g + 1) * POOL_GC)
                scaled = pool_w_ref[g] * pool_scale_ref[layer:layer + 1, sl]
                w_mix_ref[sl, :] = jnp.dot(
                    scaled, w_out_ref[sl, :], precision=lax.Precision.HIGHEST,
                    preferred_element_type=_F32).astype(_BF16)
    w_mix_ref[POOL_WIDTH:, :] = w_out_ref[POOL_WIDTH:, :].astype(_BF16)


def _fold_pool(pool_w, pool_scale, w_out):
    return pl.pallas_call(
        _fold_kernel,
        grid=(DEPTH,),
        in_specs=[pl.BlockSpec((None,) + pool_w.shape[1:], lambda l: (l, 0, 0, 0)),
                  pl.BlockSpec(pool_scale.shape, lambda l: (0, 0)),
                  pl.BlockSpec((None,) + w_out.shape[1:], lambda l: (l, 0, 0))],
        out_specs=pl.BlockSpec((None,) + w_out.shape[1:], lambda l: (l, 0, 0)),
        out_shape=jax.ShapeDtypeStruct(w_out.shape, _BF16),
        compiler_params=pltpu.CompilerParams(dimension_semantics=("arbitrary",)),
        name="fold_pool_maps",
    )(pool_w, pool_scale, w_out)


def _conv_out(w, layer, back2, back1, cur):
    tap = lambda k: w.conv_w[k, layer:layer + 1, :]
    return w.conv_b[layer:layer + 1, :] + (tap(0) * back2 + tap(1) * back1 + tap(2) * cur)


def _out_proj(h, ds, y_conv, w):
    return h + _dot(jnp.concatenate(ds + [y_conv.astype(_BF16)], axis=-1), w.w_mix[...])


def _mlp(h, w, layer, chunk):
    hg, scale = _norm_split(h, w.norm_mlp[layer:layer + 1, :])
    acc = h
    for c in range(D_FF // chunk):
        cs = slice(c * chunk, (c + 1) * chunk)
        f = jnp.maximum(_dot(hg, w.w_up[:, cs]) * scale, 0.0)
        acc = acc + _dot((f * f).astype(_BF16), w.w_down[cs, :])
    return acc


def _gated_embed(h, p, w, layer):
    hg, scale = _norm_split(h, w.norm_ple[layer:layer + 1, :])
    gate = jax.nn.sigmoid(_dot(hg, w.w_gate[...]) * scale)
    h = h + gate * _dot(p.astype(_BF16), w.w_proj[layer].astype(_BF16))
    if layer == DEPTH - 1:
        h = _rmsnorm(h, w.norm_f[...])
    return h


def _prompt_kernel(*refs, layer, tile, n_tiles, n_cast, own_cast):
    n_big = len(_BIG_FIELDS)
    n_w = len(_Weights._fields) - (n_big if own_cast else 0)
    x_ref, p_ref = refs[:2]
    pos = 2 + n_w
    prev = refs[pos:pos + 2 * layer]
    pos += 2 * layer
    cast_in = refs[pos:pos + n_cast]
    pos += n_cast
    if own_cast:
        big_hbm = refs[pos:pos + n_big]
        pos += n_big
    out_ref, npool_ref, nconv_ref = refs[pos:pos + 3]
    cast_out = refs[pos + 3:pos + 3 + n_cast]
    pos += 3 + n_cast
    if own_cast:
        big_out = refs[pos:pos + n_big]
        pool_ext, conv_ext = refs[pos + n_big:pos + n_big + 2]
        big_vmem = refs[pos + n_big + 2:pos + 2 * n_big + 2]
        stage, fetch_sem, publish_sem = refs[pos + 2 * n_big + 2:]
        w = _Weights(*refs[2:2 + n_w], *big_vmem)
    else:
        pool_ext, conv_ext = refs[pos:]
        w = _Weights(*refs[2:2 + n_w])
    i = pl.program_id(1)
    halo = POOL_HALO
    chalo = SUBLANES

    if own_cast:
        first = (pl.program_id(0) == 0) & (i == 0)
        last = (pl.program_id(0) == pl.num_programs(0) - 1) & (i == n_tiles - 1)
        rows, cols = stage.shape[1:]
        chunks = [(m, r0, c0) for m in range(n_big)
                  for r0 in range(0, big_vmem[m].shape[0], rows)
                  for c0 in range(0, big_vmem[m].shape[1], cols)]

        def fetch(k):
            m, r0, c0 = chunks[k]
            return pltpu.make_async_copy(big_hbm[m].at[layer, pl.ds(r0, rows), pl.ds(c0, cols)],
                                         stage.at[k % 2], fetch_sem.at[k % 2])

        def publish(m):
            return pltpu.make_async_copy(big_vmem[m], big_out[m], publish_sem.at[m])

        @pl.when(first)
        def _():
            fetch(0).start()
            for k, (m, r0, c0) in enumerate(chunks):
                if k + 1 < len(chunks):
                    fetch(k + 1).start()
                fetch(k).wait()
                big_vmem[m][r0:r0 + rows, c0:c0 + cols] = stage[k % 2].astype(_BF16)
            for m in range(n_big):
                publish(m).start()

        @pl.when(last)
        def _():
            for m in range(n_big):
                publish(m).wait()

    @pl.when(i == 0)
    def _():
        pool_ext[0:halo, :] = jnp.zeros((halo, POOL_WIDTH), _F32)
        conv_ext[0:chalo, :] = jnp.zeros((chalo, CONV_WIDTH), _F32)

    h = x_ref[...]
    u, bg, cv = _in_proj(h, w, layer)
    pool_ext[halo:halo + tile, :] = u
    conv_ext[chalo:chalo + tile, :] = cv

    for src, dst in zip(cast_in, cast_out):
        dst[...] = src[...].astype(_BF16)

    position = i * tile + lax.broadcasted_iota(jnp.int32, (tile, 1), 0)
    ds = []
    for g, win in enumerate(POOL_WINDOWS):
        e = pool_ext[:, g * POOL_GC:(g + 1) * POOL_GC]
        s = e
        span = 1
        while span < win:
            s = s + pltpu.roll(s, span, axis=0)
            span *= 2
        cnt = jnp.minimum(win, position + 1).astype(_F32)
        ds.append((s[halo:, :] / cnt - e[halo:, :]).astype(_BF16))
    cvx = conv_ext[...]
    z = _conv_out(w, layer, pltpu.roll(cvx, 2, axis=0)[chalo:, :],
                  pltpu.roll(cvx, 1, axis=0)[chalo:, :], cvx[chalo:, :])

    h = _mlp(_out_proj(h, ds, bg * z, w), w, layer, PROMPT_FF_CHUNK)
    out_ref[...] = _gated_embed(h, p_ref[...], w, layer)

    @pl.when(i == n_tiles - 1)
    def _():
        seq = pl.ds(pl.program_id(0), 1)
        new_pool = pool_ext[tile + halo - POOL_BUF:tile + halo, :][:, None, :]
        new_conv = conv_ext[tile + chalo - CONV_BUF:tile + chalo, :]
        if layer == 0:
            npool_ref[:, seq, :] = new_pool
            nconv_ref[...] = new_conv
        else:
            for k in range(layer):
                npool_ref[k] = prev[2 * k][...]
                nconv_ref[k] = prev[2 * k + 1][...]
            npool_ref[layer, :, seq, :] = new_pool
            nconv_ref[layer] = new_conv

    @pl.when(i < n_tiles - 1)
    def _():
        pool_ext[0:halo, :] = pool_ext[tile:tile + halo, :]
        conv_ext[0:chalo, :] = conv_ext[tile:tile + chalo, :]


def _sample_kernel(*refs, layer, tile):
    n_w = len(_Weights._fields)
    x_ref, p_ref, pool_hist, conv_hist = refs[:4]
    w = _Weights(*refs[4:4 + n_w])
    prev = refs[4 + n_w:4 + n_w + 2 * layer]
    out_ref, npool_ref, nconv_ref, pool_ext, conv_ext = refs[4 + n_w + 2 * layer:]
    seqs = tile // SUBLANES
    halo = POOL_HALO * seqs
    chalo = CONV_BUF * seqs
    positions = range(SUBLANES)
    by_position = lambda ref: jnp.concatenate([ref[:, t, :] for t in positions], axis=0)

    h = by_position(x_ref) if layer == 0 else x_ref[...].reshape(tile, D_MODEL)
    u, bg, cv = _in_proj(h, w, layer)

    pool_ext[0:seqs, :] = pool_hist[0]
    pool_ext[seqs:halo, :] = pool_hist[...].reshape(POOL_BUF * seqs, POOL_WIDTH)
    pool_ext[halo:halo + tile, :] = u
    for k in range(CONV_BUF):
        conv_ext[k * seqs:(k + 1) * seqs, :] = conv_hist[:, k, :]
    conv_ext[chalo:chalo + tile, :] = cv

    ds = []
    for g, win in enumerate(POOL_WINDOWS):
        sl = slice(g * POOL_GC, (g + 1) * POOL_GC)
        s = pool_ext[halo - (win - 1) * seqs:halo + tile, sl]
        span = 1
        while span < win:
            s = s[span * seqs:, :] + s[:-span * seqs, :]
            span *= 2
        ds.append((s * (1.0 / win) - u[:, sl]).astype(_BF16))
    z = _conv_out(w, layer, conv_ext[0:tile, :], conv_ext[seqs:seqs + tile, :], cv)

    h = _mlp(_out_proj(h, ds, bg * z, w), w, layer, SAMPLE_FF_CHUNK)
    h = _gated_embed(h, by_position(p_ref), w, layer)
    if layer == DEPTH - 1:
        for t in positions:
            out_ref[:, t, :] = h[t * seqs:(t + 1) * seqs, :]
    else:
        out_ref[...] = h.reshape(SUBLANES, seqs, D_MODEL)

    pool_dst = npool_ref if layer == 0 else npool_ref.at[layer]
    conv_dst = nconv_ref if layer == 0 else nconv_ref.at[layer]
    pool_dst[...] = pool_ext[halo + tile - POOL_BUF * seqs:halo + tile, :].reshape(
        POOL_BUF, seqs, POOL_WIDTH)
    for k in range(CONV_BUF):
        conv_dst[:, k, :] = conv_ext[tile + k * seqs:tile + (k + 1) * seqs, :]
    for k in range(layer):
        npool_ref[k] = prev[2 * k][...]
        nconv_ref[k] = prev[2 * k + 1][...]


def _weight_specs(weights, layer):
    once = pl.Buffered(1)
    specs = [pl.BlockSpec(a.shape, lambda *_, nd=a.ndim: (0,) * nd, pipeline_mode=once)
             for a in weights]
    specs[_MIX_FIELD] = pl.BlockSpec((None,) + weights[_MIX_FIELD].shape[1:],
                                     lambda *_: (layer, 0, 0), pipeline_mode=once)
    return specs


def _prompt_layer(layer, x, p, weights, prev_state, cast_next, cast_own=()):
    batch, seq, _ = x.shape
    tile = TOKEN_TILE
    n_tiles = seq // tile
    steps = batch * n_tiles
    assert seq == n_tiles * tile and tile >= POOL_HALO and len(prev_state) == 2 * layer
    for a in cast_next:
        assert a.shape[1] % (steps * BF16_ROWS) == 0
    for a in cast_own:
        assert a.shape[1] % CAST_STAGE[0] == 0 and a.shape[2] % CAST_STAGE[1] == 0
    assert len(weights) + len(cast_own) == len(_Weights._fields)
    hbm = pl.BlockSpec(memory_space=pl.ANY)
    whole = lambda shape: pl.BlockSpec(shape, lambda b, i, nd=len(shape): (0,) * nd)
    per_batch = pl.BlockSpec((None, CONV_BUF, CONV_WIDTH), lambda b, i: (b, 0, 0))
    pool_shape = (POOL_BUF, batch, POOL_WIDTH)
    in_specs = [
        pl.BlockSpec((None, tile, D_MODEL), lambda b, i: (b, i, 0)),
        pl.BlockSpec((None, None, tile, PLE_DIM), lambda b, i: (layer, b, i, 0)),
    ] + _weight_specs(weights, layer) + [whole(pool_shape), per_batch] * layer + [
        pl.BlockSpec((None, a.shape[1] // steps, a.shape[2]),
                     lambda b, i: (layer + 1, b * n_tiles + i, 0)) for a in cast_next] + [
        hbm] * len(cast_own)
    out_specs = [pl.BlockSpec((None, tile, D_MODEL), lambda b, i: (b, i, 0))]
    out_shape = [jax.ShapeDtypeStruct((batch, seq, D_MODEL), _F32)]
    if layer == 0:
        out_specs += [whole(pool_shape), per_batch]
        out_shape += [jax.ShapeDtypeStruct(pool_shape, _F32),
                      jax.ShapeDtypeStruct((batch, CONV_BUF, CONV_WIDTH), _F32)]
    else:
        out_specs += [whole((layer + 1,) + pool_shape),
                      pl.BlockSpec((layer + 1, None, CONV_BUF, CONV_WIDTH), lambda b, i: (0, b, 0, 0))]
        out_shape += [jax.ShapeDtypeStruct((layer + 1,) + pool_shape, _F32),
                      jax.ShapeDtypeStruct((layer + 1, batch, CONV_BUF, CONV_WIDTH), _F32)]
    out_specs += [pl.BlockSpec((a.shape[1] // steps, a.shape[2]), lambda b, i: (b * n_tiles + i, 0))
                  for a in cast_next]
    out_shape += [jax.ShapeDtypeStruct(a.shape[1:], _BF16) for a in cast_next]
    scratch_shapes = [pltpu.VMEM((POOL_HALO + tile, POOL_WIDTH), _F32),
                      pltpu.VMEM((SUBLANES + tile, CONV_WIDTH), _F32)]
    if cast_own:
        out_specs += [hbm] * len(cast_own)
        out_shape += [jax.ShapeDtypeStruct(a.shape[1:], _BF16) for a in cast_own]
        scratch_shapes += [pltpu.VMEM(a.shape[1:], _BF16) for a in cast_own] + [
            pltpu.VMEM((2,) + CAST_STAGE, _F32),
            pltpu.SemaphoreType.DMA((2,)), pltpu.SemaphoreType.DMA((len(cast_own),))]
    return pl.pallas_call(
        functools.partial(_prompt_kernel, layer=layer, tile=tile, n_tiles=n_tiles,
                          n_cast=len(cast_next), own_cast=bool(cast_own)),
        grid=(batch, n_tiles),
        in_specs=in_specs,
        out_specs=out_specs,
        out_shape=out_shape,
        scratch_shapes=scratch_shapes,
        compiler_params=pltpu.CompilerParams(
            dimension_semantics=("arbitrary", "arbitrary"),
            vmem_limit_bytes=VMEM_LIMIT_BYTES),
        name=f"prompt_layer{layer}",
    )(x, p, *weights, *prev_state, *cast_next, *cast_own)


def _sample_layer(layer, x, p, state_pool, state_conv, weights, prev_state):
    n_seq = p.shape[1]
    seqs = TOKEN_TILE // SUBLANES
    tile = seqs * SUBLANES
    n_tiles = n_seq // seqs
    assert n_seq == n_tiles * seqs and p.shape[2] == SUBLANES and len(prev_state) == 2 * layer
    once = pl.Buffered(1)
    pool_rows = lambda **kw: pl.BlockSpec((POOL_BUF, seqs, POOL_WIDTH), lambda i: (0, i, 0), **kw)
    conv_rows = lambda **kw: pl.BlockSpec((seqs, CONV_BUF, CONV_WIDTH), lambda i: (i, 0, 0), **kw)
    by_sequence = pl.BlockSpec((seqs, SUBLANES, D_MODEL), lambda i: (i, 0, 0))
    by_position = pl.BlockSpec((SUBLANES, seqs, D_MODEL), lambda i: (0, i, 0))
    last = layer == DEPTH - 1
    in_specs = [
        by_sequence if layer == 0 else by_position,
        pl.BlockSpec((None, seqs, SUBLANES, PLE_DIM), lambda i: (layer, i, 0, 0)),
        pl.BlockSpec((None, POOL_BUF, seqs, POOL_WIDTH), lambda i: (layer, 0, i, 0), pipeline_mode=once),
        pl.BlockSpec((None, seqs, CONV_BUF, CONV_WIDTH), lambda i: (layer, i, 0, 0), pipeline_mode=once),
    ] + _weight_specs(weights, layer) + [
        pool_rows(pipeline_mode=once), conv_rows(pipeline_mode=once)] * layer
    out_specs = [by_sequence if last else by_position]
    out_shape = [jax.ShapeDtypeStruct((n_seq, SUBLANES, D_MODEL) if last else
                                      (SUBLANES, n_seq, D_MODEL), _F32)]
    if layer == 0:
        out_specs += [pool_rows(), conv_rows()]
        out_shape += [jax.ShapeDtypeStruct((POOL_BUF, n_seq, POOL_WIDTH), _F32),
                      jax.ShapeDtypeStruct((n_seq, CONV_BUF, CONV_WIDTH), _F32)]
    else:
        out_specs += [pl.BlockSpec((layer + 1, POOL_BUF, seqs, POOL_WIDTH), lambda i: (0, 0, i, 0)),
                      pl.BlockSpec((layer + 1, seqs, CONV_BUF, CONV_WIDTH), lambda i: (0, i, 0, 0))]
        out_shape += [jax.ShapeDtypeStruct((layer + 1, POOL_BUF, n_seq, POOL_WIDTH), _F32),
                      jax.ShapeDtypeStruct((layer + 1, n_seq, CONV_BUF, CONV_WIDTH), _F32)]
    return pl.pallas_call(
        functools.partial(_sample_kernel, layer=layer, tile=tile),
        grid=(n_tiles,),
        in_specs=in_specs,
        out_specs=out_specs,
        out_shape=out_shape,
        scratch_shapes=[pltpu.VMEM(((POOL_HALO + SUBLANES) * seqs, POOL_WIDTH), _F32),
                        pltpu.VMEM(((CONV_BUF + SUBLANES) * seqs, CONV_WIDTH), _F32)],
        compiler_params=pltpu.CompilerParams(
            dimension_semantics=("arbitrary",),
            vmem_limit_bytes=VMEM_LIMIT_BYTES),
        name=f"sample_layer{layer}",
    )(x, p, state_pool, state_conv, *weights, *prev_state)


def kernel(x_prompt, x_sample, state_pool, state_conv, p_prompt, p_sample, norm_mix, w_in, pool_w, pool_scale, conv_w, conv_b, w_out, norm_mlp, w_up, w_down, norm_ple, w_ple_gate, w_ple_proj, norm_f):
    assert DEPTH == 2 and PAST_LEN >= POOL_BUF
    shared = (norm_mix, jnp.swapaxes(conv_w, 0, 1), conv_b, norm_mlp, norm_ple,
              norm_f.reshape(1, D_MODEL), w_ple_proj)
    big_f32 = (w_in, w_up, w_down, w_ple_gate)

    shared += (_fold_pool(pool_w, pool_scale, w_out),)

    n_big = len(big_f32)
    h, pool_p, conv_p, *big_bf16 = _prompt_layer(0, x_prompt, p_prompt, shared, (), big_f32, big_f32)
    weights = [_Weights(*shared, *big_bf16[n_big:]), _Weights(*shared, *big_bf16[:n_big])]
    y_prompt, pool_p, conv_p = _prompt_layer(1, h, p_prompt, weights[1], (pool_p, conv_p), ())

    assert x_sample.shape[1] == SUBLANES
    pool_hist = jnp.swapaxes(state_pool, 1, 2)
    h, pool_s, conv_s = _sample_layer(0, x_sample, p_sample, pool_hist, state_conv, weights[0], ())
    y_sample, pool_s, conv_s = _sample_layer(1, h, p_sample, pool_hist, state_conv, weights[1],
                                             (pool_s, conv_s))
    return (y_prompt, y_sample, jnp.swapaxes(pool_p, 1, 2), conv_p, jnp.swapaxes(pool_s, 1, 2), conv_s)
```

```python
import functools
from typing import NamedTuple

import jax
import jax.numpy as jnp
from jax import lax
from jax.experimental import pallas as pl
from jax.experimental.pallas import tpu as pltpu

D_MODEL = 1024
DEPTH = 2
PAST_LEN = 16384
POOL_WIDTH = 512
CONV_WIDTH = 512
POOL_WINDOWS = (2, 4, 8, 16)
POOL_GC = 128
POOL_BUF = 15
CONV_K = 3
CONV_BUF = 2
D_FF = 4 * D_MODEL
PLE_DIM = 256
IN_COLS = POOL_WIDTH + 3 * CONV_WIDTH
EPS = 1e-6

SUBLANES = 8
BF16_ROWS = 16
POOL_HALO = POOL_BUF + 1
TOKEN_TILE = 512
PROMPT_FF_CHUNK = 1024
SAMPLE_FF_CHUNK = 1024
VMEM_LIMIT_BYTES = 56 * 1024 * 1024

_BF16 = jnp.bfloat16
_F32 = jnp.float32


class _Weights(NamedTuple):
    norm_mix: jax.Array
    conv_w: jax.Array
    conv_b: jax.Array
    norm_mlp: jax.Array
    norm_ple: jax.Array
    norm_f: jax.Array
    w_proj: jax.Array
    w_mix: jax.Array
    w_in: jax.Array
    w_up: jax.Array
    w_down: jax.Array
    w_gate: jax.Array


_BIG_FIELDS = _Weights._fields[-4:]
_MIX_FIELD = _Weights._fields.index("w_mix")
CAST_STAGE = (512, 1024)


def _rmsnorm(x, g):
    ms = jnp.mean(x * x, axis=-1, keepdims=True)
    return (x * lax.rsqrt(ms + EPS)) * g


def _dot(a, b):
    return jnp.dot(a, b, preferred_element_type=_F32)


def _norm_split(x, g):
    scale = lax.rsqrt(jnp.mean(x * x, axis=-1, keepdims=True) + EPS)
    return (x * g).astype(_BF16), scale


def _in_proj(h, w, layer):
    hg, scale = _norm_split(h, w.norm_mix[layer:layer + 1, :])
    proj = _dot(hg, w.w_in[...])
    u = proj[:, :POOL_WIDTH] * scale
    bg = proj[:, POOL_WIDTH:POOL_WIDTH + CONV_WIDTH] * scale
    cg = proj[:, POOL_WIDTH + CONV_WIDTH:POOL_WIDTH + 2 * CONV_WIDTH] * scale
    v = proj[:, POOL_WIDTH + 2 * CONV_WIDTH:] * scale
    return u, bg, cg * v


def _fold_kernel(pool_w_ref, pool_scale_ref, w_out_ref, w_mix_ref):
    def split(a):
        head = a.astype(_BF16)
        return head, (a - head.astype(_F32)).astype(_BF16)

    pooling_half = pl.program_id(1) == 0
    for layer in range(DEPTH):
        @pl.when(pooling_half & (pl.program_id(0) == layer))
        def _():
            for g in range(len(POOL_WINDOWS)):
                sl = slice(g * POOL_GC, (g + 1) * POOL_GC)
                a_head, a_rem = split(pool_w_ref[g] * pool_scale_ref[layer:layer + 1, sl])
                b_head, b_rem = split(w_out_ref[sl, :])
                w_mix_ref[sl, :] = _dot(jnp.concatenate([a_head, a_head, a_rem], axis=1),
                                        jnp.concatenate([b_head, b_rem, b_head], axis=0)).astype(_BF16)

    @pl.when(jnp.logical_not(pooling_half))
    def _():
        w_mix_ref[...] = w_out_ref[...].astype(_BF16)


def _fold_pool(pool_w, pool_scale, w_out):
    assert w_out.shape[1] == 2 * POOL_WIDTH
    half = (None, POOL_WIDTH, w_out.shape[2])
    return pl.pallas_call(
        _fold_kernel,
        grid=(DEPTH, 2),
        in_specs=[pl.BlockSpec((None,) + pool_w.shape[1:], lambda l, k: (l, 0, 0, 0)),
                  pl.BlockSpec(pool_scale.shape, lambda l, k: (0, 0)),
                  pl.BlockSpec(half, lambda l, k: (l, k, 0))],
        out_specs=pl.BlockSpec(half, lambda l, k: (l, k, 0)),
        out_shape=jax.ShapeDtypeStruct(w_out.shape, _BF16),
        compiler_params=pltpu.CompilerParams(dimension_semantics=("arbitrary", "arbitrary")),
        name="fold_pool_maps",
    )(pool_w, pool_scale, w_out)


def _conv_out(w, layer, back2, back1, cur):
    tap = lambda k: w.conv_w[k, layer:layer + 1, :]
    return w.conv_b[layer:layer + 1, :] + (tap(0) * back2 + tap(1) * back1 + tap(2) * cur)


def _out_proj(h, ds, y_conv, w):
    return h + _dot(jnp.concatenate(ds + [y_conv.astype(_BF16)], axis=-1), w.w_mix[...])


def _mlp(h, w, layer, chunk):
    hg, scale = _norm_split(h, w.norm_mlp[layer:layer + 1, :])
    acc = h
    for c in range(D_FF // chunk):
        cs = slice(c * chunk, (c + 1) * chunk)
        f = jnp.maximum(_dot(hg, w.w_up[:, cs]) * scale, 0.0)
        acc = acc + _dot((f * f).astype(_BF16), w.w_down[cs, :])
    return acc


def _gated_embed(h, p, w, layer):
    hg, scale = _norm_split(h, w.norm_ple[layer:layer + 1, :])
    gate = jax.nn.sigmoid(_dot(hg, w.w_gate[...]) * scale)
    h = h + gate * _dot(p.astype(_BF16), w.w_proj[layer].astype(_BF16))
    if layer == DEPTH - 1:
        h = _rmsnorm(h, w.norm_f[...])
    return h


def _prompt_kernel(*refs, layer, tile, n_tiles, n_cast, own_cast):
    n_big = len(_BIG_FIELDS)
    n_w = len(_Weights._fields) - (n_big if own_cast else 0)
    x_ref, p_ref = refs[:2]
    pos = 2 + n_w
    prev = refs[pos:pos + 2 * layer]
    pos += 2 * layer
    cast_in = refs[pos:pos + n_cast]
    pos += n_cast
    if own_cast:
        big_hbm = refs[pos:pos + n_big]
        pos += n_big
    out_ref, npool_ref, nconv_ref = refs[pos:pos + 3]
    cast_out = refs[pos + 3:pos + 3 + n_cast]
    pos += 3 + n_cast
    if own_cast:
        big_out = refs[pos:pos + n_big]
        pool_ext, conv_ext = refs[pos + n_big:pos + n_big + 2]
        big_vmem = refs[pos + n_big + 2:pos + 2 * n_big + 2]
        stage, fetch_sem, publish_sem = refs[pos + 2 * n_big + 2:]
        w = _Weights(*refs[2:2 + n_w], *big_vmem)
    else:
        pool_ext, conv_ext = refs[pos:]
        w = _Weights(*refs[2:2 + n_w])
    i = pl.program_id(1)
    halo = POOL_HALO
    chalo = SUBLANES

    if own_cast:
        first = (pl.program_id(0) == 0) & (i == 0)
        last = (pl.program_id(0) == pl.num_programs(0) - 1) & (i == n_tiles - 1)
        rows, cols = stage.shape[1:]
        chunks = [(m, r0, c0) for m in range(n_big)
                  for r0 in range(0, big_vmem[m].shape[0], rows)
                  for c0 in range(0, big_vmem[m].shape[1], cols)]

        def fetch(k):
            m, r0, c0 = chunks[k]
            return pltpu.make_async_copy(big_hbm[m].at[layer, pl.ds(r0, rows), pl.ds(c0, cols)],
                                         stage.at[k % 2], fetch_sem.at[k % 2])

        def publish(m):
            return pltpu.make_async_copy(big_vmem[m], big_out[m], publish_sem.at[m])

        @pl.when(first)
        def _():
            fetch(0).start()
            for k, (m, r0, c0) in enumerate(chunks):
                if k + 1 < len(chunks):
                    fetch(k + 1).start()
                fetch(k).wait()
                big_vmem[m][r0:r0 + rows, c0:c0 + cols] = stage[k % 2].astype(_BF16)
            for m in range(n_big):
                publish(m).start()

        @pl.when(last)
        def _():
            for m in range(n_big):
                publish(m).wait()

    @pl.when(i == 0)
    def _():
        pool_ext[0:halo, :] = jnp.zeros((halo, POOL_WIDTH), _F32)
        conv_ext[0:chalo, :] = jnp.zeros((chalo, CONV_WIDTH), _F32)

    h = x_ref[...]
    u, bg, cv = _in_proj(h, w, layer)
    pool_ext[halo:halo + tile, :] = u
    conv_ext[chalo:chalo + tile, :] = cv

    for src, dst in zip(cast_in, cast_out):
        dst[...] = src[...].astype(_BF16)

    position = i * tile + lax.broadcasted_iota(jnp.int32, (tile, 1), 0)
    ds = []
    for g, win in enumerate(POOL_WINDOWS):
        e = pool_ext[:, g * POOL_GC:(g + 1) * POOL_GC]
        s = e
        span = 1
        while span < win:
            s = s + pltpu.roll(s, span, axis=0)
            span *= 2
        cnt = jnp.minimum(win, position + 1).astype(_F32)
        ds.append((s[halo:, :] / cnt - e[halo:, :]).astype(_BF16))
    cvx = conv_ext[...]
    z = _conv_out(w, layer, pltpu.roll(cvx, 2, axis=0)[chalo:, :],
                  pltpu.roll(cvx, 1, axis=0)[chalo:, :], cvx[chalo:, :])

    h = _mlp(_out_proj(h, ds, bg * z, w), w, layer, PROMPT_FF_CHUNK)
    out_ref[...] = _gated_embed(h, p_ref[...], w, layer)

    @pl.when(i == n_tiles - 1)
    def _():
        seq = pl.ds(pl.program_id(0), 1)
        new_pool = pool_ext[tile + halo - POOL_BUF:tile + halo, :][:, None, :]
        new_conv = conv_ext[tile + chalo - CONV_BUF:tile + chalo, :]
        if layer == 0:
            npool_ref[:, seq, :] = new_pool
            nconv_ref[...] = new_conv
        else:
            for k in range(layer):
                npool_ref[k] = prev[2 * k][...]
                nconv_ref[k] = prev[2 * k + 1][...]
            npool_ref[layer, :, seq, :] = new_pool
            nconv_ref[layer] = new_conv

    @pl.when(i < n_tiles - 1)
    def _():
        pool_ext[0:halo, :] = pool_ext[tile:tile + halo, :]
        conv_ext[0:chalo, :] = conv_ext[tile:tile + chalo, :]


def _sample_kernel(*refs, layer, tile):
    n_w = len(_Weights._fields)
    x_ref, p_ref, pool_hist, conv_hist = refs[:4]
    w = _Weights(*refs[4:4 + n_w])
    prev = refs[4 + n_w:4 + n_w + 2 * layer]
    out_ref, npool_ref, nconv_ref, pool_ext, conv_ext = refs[4 + n_w + 2 * layer:]
    seqs = tile // SUBLANES
    halo = POOL_HALO * seqs
    chalo = CONV_BUF * seqs
    positions = range(SUBLANES)
    by_position = lambda ref: jnp.concatenate([ref[:, t, :] for t in positions], axis=0)

    h = by_position(x_ref) if layer == 0 else x_ref[...].reshape(tile, D_MODEL)
    u, bg, cv = _in_proj(h, w, layer)

    pool_ext[0:seqs, :] = pool_hist[0]
    pool_ext[seqs:halo, :] = pool_hist[...].reshape(POOL_BUF * seqs, POOL_WIDTH)
    pool_ext[halo:halo + tile, :] = u
    for k in range(CONV_BUF):
        conv_ext[k * seqs:(k + 1) * seqs, :] = conv_hist[:, k, :]
    conv_ext[chalo:chalo + tile, :] = cv

    ds = []
    for g, win in enumerate(POOL_WINDOWS):
        sl = slice(g * POOL_GC, (g + 1) * POOL_GC)
        s = pool_ext[halo - (win - 1) * seqs:halo + tile, sl]
        span = 1
        while span < win:
            s = s[span * seqs:, :] + s[:-span * seqs, :]
            span *= 2
        ds.append((s * (1.0 / win) - u[:, sl]).astype(_BF16))
    z = _conv_out(w, layer, conv_ext[0:tile, :], conv_ext[seqs:seqs + tile, :], cv)

    h = _mlp(_out_proj(h, ds, bg * z, w), w, layer, SAMPLE_FF_CHUNK)
    h = _gated_embed(h, by_position(p_ref), w, layer)
    if layer == DEPTH - 1:
        for t in positions:
            out_ref[:, t, :] = h[t * seqs:(t + 1) * seqs, :]
    else:
        out_ref[...] = h.reshape(SUBLANES, seqs, D_MODEL)

    pool_dst = npool_ref if layer == 0 else npool_ref.at[layer]
    conv_dst = nconv_ref if layer == 0 else nconv_ref.at[layer]
    pool_dst[...] = pool_ext[halo + tile - POOL_BUF * seqs:halo + tile, :].reshape(
        POOL_BUF, seqs, POOL_WIDTH)
    for k in range(CONV_BUF):
        conv_dst[:, k, :] = conv_ext[tile + k * seqs:tile + (k + 1) * seqs, :]
    for k in range(layer):
        npool_ref[k] = prev[2 * k][...]
        nconv_ref[k] = prev[2 * k + 1][...]


def _weight_specs(weights, layer):
    once = pl.Buffered(1)
    specs = [pl.BlockSpec(a.shape, lambda *_, nd=a.ndim: (0,) * nd, pipeline_mode=once)
             for a in weights]
    specs[_MIX_FIELD] = pl.BlockSpec((None,) + weights[_MIX_FIELD].shape[1:],
                                     lambda *_: (layer, 0, 0), pipeline_mode=once)
    return specs


def _prompt_layer(layer, x, p, weights, prev_state, cast_next, cast_own=()):
    batch, seq, _ = x.shape
    tile = TOKEN_TILE
    n_tiles = seq // tile
    steps = batch * n_tiles
    assert seq == n_tiles * tile and tile >= POOL_HALO and len(prev_state) == 2 * layer
    for a in cast_next:
        assert a.shape[1] % (steps * BF16_ROWS) == 0
    for a in cast_own:
        assert a.shape[1] % CAST_STAGE[0] == 0 and a.shape[2] % CAST_STAGE[1] == 0
    assert len(weights) + len(cast_own) == len(_Weights._fields)
    hbm = pl.BlockSpec(memory_space=pl.ANY)
    whole = lambda shape: pl.BlockSpec(shape, lambda b, i, nd=len(shape): (0,) * nd)
    per_batch = pl.BlockSpec((None, CONV_BUF, CONV_WIDTH), lambda b, i: (b, 0, 0))
    pool_shape = (POOL_BUF, batch, POOL_WIDTH)
    in_specs = [
        pl.BlockSpec((None, tile, D_MODEL), lambda b, i: (b, i, 0)),
        pl.BlockSpec((None, None, tile, PLE_DIM), lambda b, i: (layer, b, i, 0)),
    ] + _weight_specs(weights, layer) + [whole(pool_shape), per_batch] * layer + [
        pl.BlockSpec((None, a.shape[1] // steps, a.shape[2]),
                     lambda b, i: (layer + 1, b * n_tiles + i, 0)) for a in cast_next] + [
        hbm] * len(cast_own)
    out_specs = [pl.BlockSpec((None, tile, D_MODEL), lambda b, i: (b, i, 0))]
    out_shape = [jax.ShapeDtypeStruct((batch, seq, D_MODEL), _F32)]
    if layer == 0:
        out_specs += [whole(pool_shape), per_batch]
        out_shape += [jax.ShapeDtypeStruct(pool_shape, _F32),
                      jax.ShapeDtypeStruct((batch, CONV_BUF, CONV_WIDTH), _F32)]
    else:
        out_specs += [whole((layer + 1,) + pool_shape),
                      pl.BlockSpec((layer + 1, None, CONV_BUF, CONV_WIDTH), lambda b, i: (0, b, 0, 0))]
        out_shape += [jax.ShapeDtypeStruct((layer + 1,) + pool_shape, _F32),
                      jax.ShapeDtypeStruct((layer + 1, batch, CONV_BUF, CONV_WIDTH), _F32)]
    out_specs += [pl.BlockSpec((a.shape[1] // steps, a.shape[2]), lambda b, i: (b * n_tiles + i, 0))
                  for a in cast_next]
    out_shape += [jax.ShapeDtypeStruct(a.shape[1:], _BF16) for a in cast_next]
    scratch_shapes = [pltpu.VMEM((POOL_HALO + tile, POOL_WIDTH), _F32),
                      pltpu.VMEM((SUBLANES + tile, CONV_WIDTH), _F32)]
    if cast_own:
        out_specs += [hbm] * len(cast_own)
        out_shape += [jax.ShapeDtypeStruct(a.shape[1:], _BF16) for a in cast_own]
        scratch_shapes += [pltpu.VMEM(a.shape[1:], _BF16) for a in cast_own] + [
            pltpu.VMEM((2,) + CAST_STAGE, _F32),
            pltpu.SemaphoreType.DMA((2,)), pltpu.SemaphoreType.DMA((len(cast_own),))]
    return pl.pallas_call(
        functools.partial(_prompt_kernel, layer=layer, tile=tile, n_tiles=n_tiles,
                          n_cast=len(cast_next), own_cast=bool(cast_own)),
        grid=(batch, n_tiles),
        in_specs=in_specs,
        out_specs=out_specs,
        out_shape=out_shape,
        scratch_shapes=scratch_shapes,
        compiler_params=pltpu.CompilerParams(
            dimension_semantics=("arbitrary", "arbitrary"),
            vmem_limit_bytes=VMEM_LIMIT_BYTES),
        name=f"prompt_layer{layer}",
    )(x, p, *weights, *prev_state, *cast_next, *cast_own)


def _sample_layer(layer, x, p, state_pool, state_conv, weights, prev_state):
    n_seq = p.shape[1]
    seqs = TOKEN_TILE // SUBLANES
    tile = seqs * SUBLANES
    n_tiles = n_seq // seqs
    assert n_seq == n_tiles * seqs and p.shape[2] == SUBLANES and len(prev_state) == 2 * layer
    once = pl.Buffered(1)
    pool_rows = lambda **kw: pl.BlockSpec((POOL_BUF, seqs, POOL_WIDTH), lambda i: (0, i, 0), **kw)
    conv_rows = lambda **kw: pl.BlockSpec((seqs, CONV_BUF, CONV_WIDTH), lambda i: (i, 0, 0), **kw)
    by_sequence = pl.BlockSpec((seqs, SUBLANES, D_MODEL), lambda i: (i, 0, 0))
    by_position = pl.BlockSpec((SUBLANES, seqs, D_MODEL), lambda i: (0, i, 0))
    last = layer == DEPTH - 1
    in_specs = [
        by_sequence if layer == 0 else by_position,
        pl.BlockSpec((None, seqs, SUBLANES, PLE_DIM), lambda i: (layer, i, 0, 0)),
        pl.BlockSpec((None, POOL_BUF, seqs, POOL_WIDTH), lambda i: (layer, 0, i, 0), pipeline_mode=once),
        pl.BlockSpec((None, seqs, CONV_BUF, CONV_WIDTH), lambda i: (layer, i, 0, 0), pipeline_mode=once),
    ] + _weight_specs(weights, layer) + [
        pool_rows(pipeline_mode=once), conv_rows(pipeline_mode=once)] * layer
    out_specs = [by_sequence if last else by_position]
    out_shape = [jax.ShapeDtypeStruct((n_seq, SUBLANES, D_MODEL) if last else
                                      (SUBLANES, n_seq, D_MODEL), _F32)]
    if layer == 0:
        out_specs += [pool_rows(), conv_rows()]
        out_shape += [jax.ShapeDtypeStruct((POOL_BUF, n_seq, POOL_WIDTH), _F32),
                      jax.ShapeDtypeStruct((n_seq, CONV_BUF, CONV_WIDTH), _F32)]
    else:
        out_specs += [pl.BlockSpec((layer + 1, POOL_BUF, seqs, POOL_WIDTH), lambda i: (0, 0, i, 0)),
                      pl.BlockSpec((layer + 1, seqs, CONV_BUF, CONV_WIDTH), lambda i: (0, i, 0, 0))]
        out_shape += [jax.ShapeDtypeStruct((layer + 1, POOL_BUF, n_seq, POOL_WIDTH), _F32),
                      jax.ShapeDtypeStruct((layer + 1, n_seq, CONV_BUF, CONV_WIDTH), _F32)]
    return pl.pallas_call(
        functools.partial(_sample_kernel, layer=layer, tile=tile),
        grid=(n_tiles,),
        in_specs=in_specs,
        out_specs=out_specs,
        out_shape=out_shape,
        scratch_shapes=[pltpu.VMEM(((POOL_HALO + SUBLANES) * seqs, POOL_WIDTH), _F32),
                        pltpu.VMEM(((CONV_BUF + SUBLANES) * seqs, CONV_WIDTH), _F32)],
        compiler_params=pltpu.CompilerParams(
            dimension_semantics=("arbitrary",),
            vmem_limit_bytes=VMEM_LIMIT_BYTES),
        name=f"sample_layer{layer}",
    )(x, p, state_pool, state_conv, *weights, *prev_state)


def kernel(x_prompt, x_sample, state_pool, state_conv, p_prompt, p_sample, norm_mix, w_in, pool_w, pool_scale, conv_w, conv_b, w_out, norm_mlp, w_up, w_down, norm_ple, w_ple_gate, w_ple_proj, norm_f):
    assert DEPTH == 2 and PAST_LEN >= POOL_BUF
    shared = (norm_mix, jnp.swapaxes(conv_w, 0, 1), conv_b, norm_mlp, norm_ple,
              norm_f.reshape(1, D_MODEL), w_ple_proj)
    big_f32 = (w_in, w_up, w_down, w_ple_gate)

    shared += (_fold_pool(pool_w, pool_scale, w_out),)

    n_big = len(big_f32)
    h, pool_p, conv_p, *big_bf16 = _prompt_layer(0, x_prompt, p_prompt, shared, (), big_f32, big_f32)
    weights = [_Weights(*shared, *big_bf16[n_big:]), _Weights(*shared, *big_bf16[:n_big])]
    y_prompt, pool_p, conv_p = _prompt_layer(1, h, p_prompt, weights[1], (pool_p, conv_p), ())

    assert x_sample.shape[1] == SUBLANES
    pool_hist = jnp.swapaxes(state_pool, 1, 2)
    h, pool_s, conv_s = _sample_layer(0, x_sample, p_sample, pool_hist, state_conv, weights[0], ())
    y_sample, pool_s, conv_s = _sample_layer(1, h, p_sample, pool_hist, state_conv, weights[1],
                                             (pool_s, conv_s))
    return (y_prompt, y_sample, jnp.swapaxes(pool_p, 1, 2), conv_p, jnp.swapaxes(pool_s, 1, 2), conv_s)
```

```python
import functools
from typing import NamedTuple

import jax
import jax.numpy as jnp
from jax import lax
from jax.experimental import pallas as pl
from jax.experimental.pallas import tpu as pltpu

D_MODEL = 1024
DEPTH = 2
PAST_LEN = 16384
POOL_WIDTH = 512
CONV_WIDTH = 512
POOL_WINDOWS = (2, 4, 8, 16)
POOL_GC = 128
POOL_BUF = 15
CONV_K = 3
CONV_BUF = 2
D_FF = 4 * D_MODEL
PLE_DIM = 256
IN_COLS = POOL_WIDTH + 3 * CONV_WIDTH
EPS = 1e-6

SUBLANES = 8
BF16_ROWS = 16
POOL_HALO = POOL_BUF + 1
TOKEN_TILE = 512
PROMPT_FF_CHUNK = 1024
SAMPLE_FF_CHUNK = 1024
VMEM_LIMIT_BYTES = 56 * 1024 * 1024

_BF16 = jnp.bfloat16
_F32 = jnp.float32


class _Weights(NamedTuple):
    norm_mix: jax.Array
    conv_w: jax.Array
    conv_b: jax.Array
    norm_mlp: jax.Array
    norm_ple: jax.Array
    norm_f: jax.Array
    w_proj: jax.Array
    w_mix: jax.Array
    w_in: jax.Array
    w_up: jax.Array
    w_down: jax.Array
    w_gate: jax.Array


_BIG_FIELDS = _Weights._fields[-4:]
_MIX_FIELD = _Weights._fields.index("w_mix")
CAST_STAGE = (512, 1024)


def _rmsnorm(x, g):
    ms = jnp.mean(x * x, axis=-1, keepdims=True)
    return (x * lax.rsqrt(ms + EPS)) * g


def _dot(a, b):
    return jnp.dot(a, b, preferred_element_type=_F32)


def _norm_split(x, g):
    scale = lax.rsqrt(jnp.mean(x * x, axis=-1, keepdims=True) + EPS)
    return (x * g).astype(_BF16), scale


def _in_proj(h, w, layer):
    hg, scale = _norm_split(h, w.norm_mix[layer:layer + 1, :])
    proj = _dot(hg, w.w_in[...])
    u = proj[:, :POOL_WIDTH] * scale
    bg = proj[:, POOL_WIDTH:POOL_WIDTH + CONV_WIDTH] * scale
    cg = proj[:, POOL_WIDTH + CONV_WIDTH:POOL_WIDTH + 2 * CONV_WIDTH] * scale
    v = proj[:, POOL_WIDTH + 2 * CONV_WIDTH:] * scale
    return u, bg, cg * v


def _fold_kernel(pool_w_ref, pool_scale_ref, w_out_ref, w_mix_ref):
    def split(a):
        head = a.astype(_BF16)
        return head, (a - head.astype(_F32)).astype(_BF16)

    pooling_half = pl.program_id(1) == 0
    for layer in range(DEPTH):
        @pl.when(pooling_half & (pl.program_id(0) == layer))
        def _():
            for g in range(len(POOL_WINDOWS)):
                sl = slice(g * POOL_GC, (g + 1) * POOL_GC)
                a_head, a_rem = split(pool_w_ref[g] * pool_scale_ref[layer:layer + 1, sl])
                b_head, b_rem = split(w_out_ref[sl, :])
                w_mix_ref[sl, :] = _dot(jnp.concatenate([a_head, a_head, a_rem], axis=1),
                                        jnp.concatenate([b_head, b_rem, b_head], axis=0)).astype(_BF16)

    @pl.when(jnp.logical_not(pooling_half))
    def _():
        w_mix_ref[...] = w_out_ref[...].astype(_BF16)


def _fold_pool(pool_w, pool_scale, w_out):
    assert w_out.shape[1] == 2 * POOL_WIDTH
    half = (None, POOL_WIDTH, w_out.shape[2])
    return pl.pallas_call(
        _fold_kernel,
        grid=(DEPTH, 2),
        in_specs=[pl.BlockSpec((None,) + pool_w.shape[1:], lambda l, k: (l, 0, 0, 0)),
                  pl.BlockSpec(pool_scale.shape, lambda l, k: (0, 0)),
                  pl.BlockSpec(half, lambda l, k: (l, k, 0))],
        out_specs=pl.BlockSpec(half, lambda l, k: (l, k, 0)),
        out_shape=jax.ShapeDtypeStruct(w_out.shape, _BF16),
        compiler_params=pltpu.CompilerParams(dimension_semantics=("arbitrary", "arbitrary")),
        name="fold_pool_maps",
    )(pool_w, pool_scale, w_out)


def _conv_out(w, layer, back2, back1, cur):
    tap = lambda k: w.conv_w[k, layer:layer + 1, :]
    return w.conv_b[layer:layer + 1, :] + (tap(0) * back2 + tap(1) * back1 + tap(2) * cur)


def _out_proj(h, ds, y_conv, w):
    return h + _dot(jnp.concatenate(ds + [y_conv.astype(_BF16)], axis=-1), w.w_mix[...])


def _mlp(h, w, layer, chunk):
    hg, scale = _norm_split(h, w.norm_mlp[layer:layer + 1, :])
    acc = h
    for c in range(D_FF // chunk):
        cs = slice(c * chunk, (c + 1) * chunk)
        f = jnp.maximum(_dot(hg, w.w_up[:, cs]) * scale, 0.0)
        acc = acc + _dot((f * f).astype(_BF16), w.w_down[cs, :])
    return acc


def _gated_embed(h, p, w, layer):
    hg, scale = _norm_split(h, w.norm_ple[layer:layer + 1, :])
    gate = jax.nn.sigmoid(_dot(hg, w.w_gate[...]) * scale)
    return h + gate * _dot(p.astype(_BF16), w.w_proj[layer].astype(_BF16))


def _prompt_kernel(*refs, layer, tile, n_tiles, n_cast, own_cast):
    n_big = len(_BIG_FIELDS)
    n_w = len(_Weights._fields) - (n_big if own_cast else 0)
    x_ref, p_ref = refs[:2]
    pos = 2 + n_w
    prev = refs[pos:pos + 2 * layer]
    pos += 2 * layer
    cast_in = refs[pos:pos + n_cast]
    pos += n_cast
    if own_cast:
        big_hbm = refs[pos:pos + n_big]
        pos += n_big
    out_ref, npool_ref, nconv_ref = refs[pos:pos + 3]
    cast_out = refs[pos + 3:pos + 3 + n_cast]
    pos += 3 + n_cast
    if own_cast:
        big_out = refs[pos:pos + n_big]
        pool_ext, conv_ext = refs[pos + n_big:pos + n_big + 2]
        big_vmem = refs[pos + n_big + 2:pos + 2 * n_big + 2]
        stage, fetch_sem, publish_sem = refs[pos + 2 * n_big + 2:]
        w = _Weights(*refs[2:2 + n_w], *big_vmem)
    else:
        pool_ext, conv_ext, *deferred = refs[pos:]
        w = _Weights(*refs[2:2 + n_w])
    halo = POOL_HALO
    chalo = SUBLANES
    defer = layer == DEPTH - 1
    if defer:
        assert not own_cast
        pending, = deferred
        step = pl.program_id(0)
        n_steps = pl.num_programs(0) - 1
        tile_index = jnp.minimum(step, n_steps - 1)
        b, i = tile_index // n_tiles, tile_index % n_tiles

        @pl.when(step == 0)
        def _():
            pending[...] = jnp.zeros(pending.shape, _F32)
    else:
        b, i = pl.program_id(0), pl.program_id(1)

    if own_cast:
        first = (b == 0) & (i == 0)
        last = (b == pl.num_programs(0) - 1) & (i == n_tiles - 1)
        rows, cols = stage.shape[1:]
        chunks = [(m, r0, c0) for m in range(n_big)
                  for r0 in range(0, big_vmem[m].shape[0], rows)
                  for c0 in range(0, big_vmem[m].shape[1], cols)]

        def fetch(k):
            m, r0, c0 = chunks[k]
            return pltpu.make_async_copy(big_hbm[m].at[layer, pl.ds(r0, rows), pl.ds(c0, cols)],
                                         stage.at[k % 2], fetch_sem.at[k % 2])

        def publish(m):
            return pltpu.make_async_copy(big_vmem[m], big_out[m], publish_sem.at[m])

        @pl.when(first)
        def _():
            fetch(0).start()
            for k, (m, r0, c0) in enumerate(chunks):
                if k + 1 < len(chunks):
                    fetch(k + 1).start()
                fetch(k).wait()
                big_vmem[m][r0:r0 + rows, c0:c0 + cols] = stage[k % 2].astype(_BF16)
            for m in range(n_big):
                publish(m).start()

        @pl.when(last)
        def _():
            for m in range(n_big):
                publish(m).wait()

    def tile_body():
        @pl.when(i == 0)
        def _():
            pool_ext[0:halo, :] = jnp.zeros((halo, POOL_WIDTH), _F32)
            conv_ext[0:chalo, :] = jnp.zeros((chalo, CONV_WIDTH), _F32)

        if defer:
            out_ref[...] = _rmsnorm(pending[...], w.norm_f[...])

        h = x_ref[...]
        u, bg, cv = _in_proj(h, w, layer)
        pool_ext[halo:halo + tile, :] = u
        conv_ext[chalo:chalo + tile, :] = cv

        for src, dst in zip(cast_in, cast_out):
            dst[...] = src[...].astype(_BF16)

        position = i * tile + lax.broadcasted_iota(jnp.int32, (tile, 1), 0)
        ds = []
        for g, win in enumerate(POOL_WINDOWS):
            e = pool_ext[:, g * POOL_GC:(g + 1) * POOL_GC]
            s = e
            span = 1
            while span < win:
                s = s + pltpu.roll(s, span, axis=0)
                span *= 2
            cnt = jnp.minimum(win, position + 1).astype(_F32)
            ds.append((s[halo:, :] / cnt - e[halo:, :]).astype(_BF16))
        cvx = conv_ext[...]
        z = _conv_out(w, layer, pltpu.roll(cvx, 2, axis=0)[chalo:, :],
                      pltpu.roll(cvx, 1, axis=0)[chalo:, :], cvx[chalo:, :])

        h = _mlp(_out_proj(h, ds, bg * z, w), w, layer, PROMPT_FF_CHUNK)
        h = _gated_embed(h, p_ref[...], w, layer)
        if defer:
            pending[...] = h
        else:
            out_ref[...] = h

        @pl.when(i == n_tiles - 1)
        def _():
            seq = pl.ds(b, 1)
            new_pool = pool_ext[tile + halo - POOL_BUF:tile + halo, :][:, None, :]
            new_conv = conv_ext[tile + chalo - CONV_BUF:tile + chalo, :]
            if layer == 0:
                npool_ref[:, seq, :] = new_pool
                nconv_ref[...] = new_conv
            else:
                for k in range(layer):
                    npool_ref[k] = prev[2 * k][...]
                    nconv_ref[k] = prev[2 * k + 1][...]
                npool_ref[layer, :, seq, :] = new_pool
                nconv_ref[layer] = new_conv

        @pl.when(i < n_tiles - 1)
        def _():
            pool_ext[0:halo, :] = pool_ext[tile:tile + halo, :]
            conv_ext[0:chalo, :] = conv_ext[tile:tile + chalo, :]

    if defer:
        pl.when(step < n_steps)(tile_body)

        @pl.when(step == n_steps)
        def _():
            out_ref[...] = _rmsnorm(pending[...], w.norm_f[...])
    else:
        tile_body()


def _sample_kernel(*refs, layer, tile):
    n_w = len(_Weights._fields)
    x_ref, p_ref, pool_hist, conv_hist = refs[:4]
    w = _Weights(*refs[4:4 + n_w])
    prev = refs[4 + n_w:4 + n_w + 2 * layer]
    out_ref, npool_ref, nconv_ref, pool_ext, conv_ext = refs[4 + n_w + 2 * layer:]
    seqs = tile // SUBLANES
    halo = POOL_HALO * seqs
    chalo = CONV_BUF * seqs
    positions = range(SUBLANES)
    by_position = lambda ref: jnp.concatenate([ref[:, t, :] for t in positions], axis=0)

    h = by_position(x_ref) if layer == 0 else x_ref[...].reshape(tile, D_MODEL)
    u, bg, cv = _in_proj(h, w, layer)

    pool_ext[0:seqs, :] = pool_hist[0]
    pool_ext[seqs:halo, :] = pool_hist[...].reshape(POOL_BUF * seqs, POOL_WIDTH)
    pool_ext[halo:halo + tile, :] = u
    for k in range(CONV_BUF):
        conv_ext[k * seqs:(k + 1) * seqs, :] = conv_hist[:, k, :]
    conv_ext[chalo:chalo + tile, :] = cv

    ds = []
    for g, win in enumerate(POOL_WINDOWS):
        sl = slice(g * POOL_GC, (g + 1) * POOL_GC)
        s = pool_ext[halo - (win - 1) * seqs:halo + tile, sl]
        span = 1
        while span < win:
            s = s[span * seqs:, :] + s[:-span * seqs, :]
            span *= 2
        ds.append((s * (1.0 / win) - u[:, sl]).astype(_BF16))
    z = _conv_out(w, layer, conv_ext[0:tile, :], conv_ext[seqs:seqs + tile, :], cv)

    h = _mlp(_out_proj(h, ds, bg * z, w), w, layer, SAMPLE_FF_CHUNK)
    h = _gated_embed(h, by_position(p_ref), w, layer)
    if layer == DEPTH - 1:
        h = _rmsnorm(h, w.norm_f[...])
        for t in positions:
            out_ref[:, t, :] = h[t * seqs:(t + 1) * seqs, :]
    else:
        out_ref[...] = h.reshape(SUBLANES, seqs, D_MODEL)

    pool_dst = npool_ref if layer == 0 else npool_ref.at[layer]
    conv_dst = nconv_ref if layer == 0 else nconv_ref.at[layer]
    pool_dst[...] = pool_ext[halo + tile - POOL_BUF * seqs:halo + tile, :].reshape(
        POOL_BUF, seqs, POOL_WIDTH)
    for k in range(CONV_BUF):
        conv_dst[:, k, :] = conv_ext[tile + k * seqs:tile + (k + 1) * seqs, :]
    for k in range(layer):
        npool_ref[k] = prev[2 * k][...]
        nconv_ref[k] = prev[2 * k + 1][...]


def _weight_specs(weights, layer):
    once = pl.Buffered(1)
    specs = [pl.BlockSpec(a.shape, lambda *_, nd=a.ndim: (0,) * nd, pipeline_mode=once)
             for a in weights]
    specs[_MIX_FIELD] = pl.BlockSpec((None,) + weights[_MIX_FIELD].shape[1:],
                                     lambda *_: (layer, 0, 0), pipeline_mode=once)
    return specs


def _prompt_layer(layer, x, p, weights, prev_state, cast_next, cast_own=()):
    batch, seq, _ = x.shape
    tile = TOKEN_TILE
    n_tiles = seq // tile
    steps = batch * n_tiles
    assert seq == n_tiles * tile and tile >= POOL_HALO and len(prev_state) == 2 * layer
    for a in cast_next:
        assert a.shape[1] % (steps * BF16_ROWS) == 0
    for a in cast_own:
        assert a.shape[1] % CAST_STAGE[0] == 0 and a.shape[2] % CAST_STAGE[1] == 0
    assert len(weights) + len(cast_own) == len(_Weights._fields)
    hbm = pl.BlockSpec(memory_space=pl.ANY)
    defer = layer == DEPTH - 1
    if defer:
        assert not cast_next and not cast_own
        grid = (steps + 1,)
        at = lambda fn: lambda s: fn(*divmod(jnp.minimum(s, steps - 1), n_tiles))
        at_prev = lambda fn: lambda s: fn(*divmod(jnp.maximum(s - 1, 0), n_tiles))
    else:
        grid = (batch, n_tiles)
        at = at_prev = lambda fn: fn
    whole = lambda shape: pl.BlockSpec(shape, lambda *_, nd=len(shape): (0,) * nd)
    per_batch = pl.BlockSpec((None, CONV_BUF, CONV_WIDTH), at(lambda b, i: (b, 0, 0)))
    pool_shape = (POOL_BUF, batch, POOL_WIDTH)
    in_specs = [
        pl.BlockSpec((None, tile, D_MODEL), at(lambda b, i: (b, i, 0))),
        pl.BlockSpec((None, None, tile, PLE_DIM), at(lambda b, i: (layer, b, i, 0))),
    ] + _weight_specs(weights, layer) + [whole(pool_shape), per_batch] * layer + [
        pl.BlockSpec((None, a.shape[1] // steps, a.shape[2]),
                     lambda b, i: (layer + 1, b * n_tiles + i, 0)) for a in cast_next] + [
        hbm] * len(cast_own)
    out_specs = [pl.BlockSpec((None, tile, D_MODEL), at_prev(lambda b, i: (b, i, 0)))]
    out_shape = [jax.ShapeDtypeStruct((batch, seq, D_MODEL), _F32)]
    if layer == 0:
        out_specs += [whole(pool_shape), per_batch]
        out_shape += [jax.ShapeDtypeStruct(pool_shape, _F32),
                      jax.ShapeDtypeStruct((batch, CONV_BUF, CONV_WIDTH), _F32)]
    else:
        out_specs += [whole((layer + 1,) + pool_shape),
                      pl.BlockSpec((layer + 1, None, CONV_BUF, CONV_WIDTH),
                                   at(lambda b, i: (0, b, 0, 0)))]
        out_shape += [jax.ShapeDtypeStruct((layer + 1,) + pool_shape, _F32),
                      jax.ShapeDtypeStruct((layer + 1, batch, CONV_BUF, CONV_WIDTH), _F32)]
    out_specs += [pl.BlockSpec((a.shape[1] // steps, a.shape[2]), lambda b, i: (b * n_tiles + i, 0))
                  for a in cast_next]
    out_shape += [jax.ShapeDtypeStruct(a.shape[1:], _BF16) for a in cast_next]
    scratch_shapes = [pltpu.VMEM((POOL_HALO + tile, POOL_WIDTH), _F32),
                      pltpu.VMEM((SUBLANES + tile, CONV_WIDTH), _F32)]
    if defer:
        scratch_shapes += [pltpu.VMEM((tile, D_MODEL), _F32)]
    if cast_own:
        out_specs += [hbm] * len(cast_own)
        out_shape += [jax.ShapeDtypeStruct(a.shape[1:], _BF16) for a in cast_own]
        scratch_shapes += [pltpu.VMEM(a.shape[1:], _BF16) for a in cast_own] + [
            pltpu.VMEM((2,) + CAST_STAGE, _F32),
            pltpu.SemaphoreType.DMA((2,)), pltpu.SemaphoreType.DMA((len(cast_own),))]
    return pl.pallas_call(
        functools.partial(_prompt_kernel, layer=layer, tile=tile, n_tiles=n_tiles,
                          n_cast=len(cast_next), own_cast=bool(cast_own)),
        grid=grid,
        in_specs=in_specs,
        out_specs=out_specs,
        out_shape=out_shape,
        scratch_shapes=scratch_shapes,
        compiler_params=pltpu.CompilerParams(
            dimension_semantics=("arbitrary",) * len(grid),
            vmem_limit_bytes=VMEM_LIMIT_BYTES),
        name=f"prompt_layer{layer}",
    )(x, p, *weights, *prev_state, *cast_next, *cast_own)


def _sample_layer(layer, x, p, state_pool, state_conv, weights, prev_state):
    n_seq = p.shape[1]
    seqs = TOKEN_TILE // SUBLANES
    tile = seqs * SUBLANES
    n_tiles = n_seq // seqs
    assert n_seq == n_tiles * seqs and p.shape[2] == SUBLANES and len(prev_state) == 2 * layer
    once = pl.Buffered(1)
    pool_rows = lambda **kw: pl.BlockSpec((POOL_BUF, seqs, POOL_WIDTH), lambda i: (0, i, 0), **kw)
    conv_rows = lambda **kw: pl.BlockSpec((seqs, CONV_BUF, CONV_WIDTH), lambda i: (i, 0, 0), **kw)
    by_sequence = pl.BlockSpec((seqs, SUBLANES, D_MODEL), lambda i: (i, 0, 0))
    by_position = pl.BlockSpec((SUBLANES, seqs, D_MODEL), lambda i: (0, i, 0))
    last = layer == DEPTH - 1
    in_specs = [
        by_sequence if layer == 0 else by_position,
        pl.BlockSpec((None, seqs, SUBLANES, PLE_DIM), lambda i: (layer, i, 0, 0)),
        pl.BlockSpec((None, POOL_BUF, seqs, POOL_WIDTH), lambda i: (layer, 0, i, 0), pipeline_mode=once),
        pl.BlockSpec((None, seqs, CONV_BUF, CONV_WIDTH), lambda i: (layer, i, 0, 0), pipeline_mode=once),
    ] + _weight_specs(weights, layer) + [
        pool_rows(pipeline_mode=once), conv_rows(pipeline_mode=once)] * layer
    out_specs = [by_sequence if last else by_position]
    out_shape = [jax.ShapeDtypeStruct((n_seq, SUBLANES, D_MODEL) if last else
                                      (SUBLANES, n_seq, D_MODEL), _F32)]
    if layer == 0:
        out_specs += [pool_rows(), conv_rows()]
        out_shape += [jax.ShapeDtypeStruct((POOL_BUF, n_seq, POOL_WIDTH), _F32),
                      jax.ShapeDtypeStruct((n_seq, CONV_BUF, CONV_WIDTH), _F32)]
    else:
        out_specs += [pl.BlockSpec((layer + 1, POOL_BUF, seqs, POOL_WIDTH), lambda i: (0, 0, i, 0)),
                      pl.BlockSpec((layer + 1, seqs, CONV_BUF, CONV_WIDTH), lambda i: (0, i, 0, 0))]
        out_shape += [jax.ShapeDtypeStruct((layer + 1, POOL_BUF, n_seq, POOL_WIDTH), _F32),
                      jax.ShapeDtypeStruct((layer + 1, n_seq, CONV_BUF, CONV_WIDTH), _F32)]
    return pl.pallas_call(
        functools.partial(_sample_kernel, layer=layer, tile=tile),
        grid=(n_tiles,),
        in_specs=in_specs,
        out_specs=out_specs,
        out_shape=out_shape,
        scratch_shapes=[pltpu.VMEM(((POOL_HALO + SUBLANES) * seqs, POOL_WIDTH), _F32),
                        pltpu.VMEM(((CONV_BUF + SUBLANES) * seqs, CONV_WIDTH), _F32)],
        compiler_params=pltpu.CompilerParams(
            dimension_semantics=("arbitrary",),
            vmem_limit_bytes=VMEM_LIMIT_BYTES),
        name=f"sample_layer{layer}",
    )(x, p, state_pool, state_conv, *weights, *prev_state)


def kernel(x_prompt, x_sample, state_pool, state_conv, p_prompt, p_sample, norm_mix, w_in, pool_w, pool_scale, conv_w, conv_b, w_out, norm_mlp, w_up, w_down, norm_ple, w_ple_gate, w_ple_proj, norm_f):
    assert DEPTH == 2 and PAST_LEN >= POOL_BUF
    shared = (norm_mix, jnp.swapaxes(conv_w, 0, 1), conv_b, norm_mlp, norm_ple,
              norm_f.reshape(1, D_MODEL), w_ple_proj)
    big_f32 = (w_in, w_up, w_down, w_ple_gate)

    shared += (_fold_pool(pool_w, pool_scale, w_out),)

    n_big = len(big_f32)
    h, pool_p, conv_p, *big_bf16 = _prompt_layer(0, x_prompt, p_prompt, shared, (), big_f32, big_f32)
    weights = [_Weights(*shared, *big_bf16[n_big:]), _Weights(*shared, *big_bf16[:n_big])]
    y_prompt, pool_p, conv_p = _prompt_layer(1, h, p_prompt, weights[1], (pool_p, conv_p), ())

    assert x_sample.shape[1] == SUBLANES
    pool_hist = jnp.swapaxes(state_pool, 1, 2)
    h, pool_s, conv_s = _sample_layer(0, x_sample, p_sample, pool_hist, state_conv, weights[0], ())
    y_sample, pool_s, conv_s = _sample_layer(1, h, p_sample, pool_hist, state_conv, weights[1],
                                             (pool_s, conv_s))
    return (y_prompt, y_sample, jnp.swapaxes(pool_p, 1, 2), conv_p, jnp.swapaxes(pool_s, 1, 2), conv_s)
```

```python
import functools
from typing import NamedTuple

import jax
import jax.numpy as jnp
from jax import lax
from jax.experimental import pallas as pl
from jax.experimental.pallas import tpu as pltpu

D_MODEL = 1024
DEPTH = 2
PAST_LEN = 16384
POOL_WIDTH = 512
CONV_WIDTH = 512
POOL_WINDOWS = (2, 4, 8, 16)
POOL_GC = 128
POOL_BUF = 15
CONV_BUF = 2
D_FF = 4 * D_MODEL
PLE_DIM = 256
IN_COLS = POOL_WIDTH + 3 * CONV_WIDTH
EPS = 1e-6

SUBLANES = 8
BF16_ROWS = 16
POOL_HALO = POOL_BUF + 1
TOKEN_TILE = 512
PROMPT_FF_CHUNK = 1024
SAMPLE_FF_CHUNK = 512
V7X_VMEM_BYTES = 64 * 1024 * 1024
VMEM_LIMIT_BYTES = V7X_VMEM_BYTES * 7 // 8

_BF16 = jnp.bfloat16
_F32 = jnp.float32


class _Weights(NamedTuple):
    norm_mix: jax.Array
    conv_w: jax.Array
    conv_b: jax.Array
    norm_mlp: jax.Array
    norm_ple: jax.Array
    norm_f: jax.Array
    w_proj: jax.Array
    w_mix: jax.Array
    w_in: jax.Array
    w_up: jax.Array
    w_down: jax.Array
    w_gate: jax.Array


_BIG_FIELDS = _Weights._fields[-4:]
_MIX_FIELD = _Weights._fields.index("w_mix")
CAST_STAGE = (512, 1024)


def _rmsnorm(x, g):
    ms = jnp.mean(x * x, axis=-1, keepdims=True)
    return (x * lax.rsqrt(ms + EPS)) * g


def _dot(a, b):
    return jnp.dot(a, b, preferred_element_type=_F32)


def _norm_split(x, g):
    scale = lax.rsqrt(jnp.mean(x * x, axis=-1, keepdims=True) + EPS)
    return (x * g).astype(_BF16), scale


def _row(vec, layer):
    if isinstance(layer, int):
        return vec[layer:layer + 1, :]
    row = vec[0:1, :]
    for k in range(1, DEPTH):
        row = jnp.where(layer == k, vec[k:k + 1, :], row)
    return row


def _in_proj(h, w, layer):
    hg, scale = _norm_split(h, _row(w.norm_mix[...], layer))
    proj = _dot(hg, w.w_in[...])
    u = proj[:, :POOL_WIDTH] * scale
    bg = proj[:, POOL_WIDTH:POOL_WIDTH + CONV_WIDTH] * scale
    cg = proj[:, POOL_WIDTH + CONV_WIDTH:POOL_WIDTH + 2 * CONV_WIDTH] * scale
    v = proj[:, POOL_WIDTH + 2 * CONV_WIDTH:] * scale
    return u, bg, cg * v


def _fold_kernel(pool_w_ref, pool_scale_ref, w_out_ref, w_mix_ref):
    def split(a):
        head = a.astype(_BF16)
        return head, (a - head.astype(_F32)).astype(_BF16)

    pooling_half = pl.program_id(1) == 0
    for layer in range(DEPTH):
        @pl.when(pooling_half & (pl.program_id(0) == layer))
        def _():
            for g in range(len(POOL_WINDOWS)):
                sl = slice(g * POOL_GC, (g + 1) * POOL_GC)
                a_head, a_rem = split(pool_w_ref[g] * pool_scale_ref[layer:layer + 1, sl])
                b_head, b_rem = split(w_out_ref[sl, :])
                w_mix_ref[sl, :] = _dot(jnp.concatenate([a_head, a_head, a_rem], axis=1),
                                        jnp.concatenate([b_head, b_rem, b_head], axis=0)).astype(_BF16)

    @pl.when(jnp.logical_not(pooling_half))
    def _():
        w_mix_ref[...] = w_out_ref[...].astype(_BF16)


def _fold_pool(pool_w, pool_scale, w_out):
    assert w_out.shape[1] == 2 * POOL_WIDTH
    half = (None, POOL_WIDTH, w_out.shape[2])
    return pl.pallas_call(
        _fold_kernel,
        grid=(DEPTH, 2),
        in_specs=[pl.BlockSpec((None,) + pool_w.shape[1:], lambda l, k: (l, 0, 0, 0)),
                  pl.BlockSpec(pool_scale.shape, lambda l, k: (0, 0)),
                  pl.BlockSpec(half, lambda l, k: (l, k, 0))],
        out_specs=pl.BlockSpec(half, lambda l, k: (l, k, 0)),
        out_shape=jax.ShapeDtypeStruct(w_out.shape, _BF16),
        compiler_params=pltpu.CompilerParams(dimension_semantics=("arbitrary", "arbitrary")),
        name="fold_pool_maps",
    )(pool_w, pool_scale, w_out)


def _conv_out(w, layer, back2, back1, cur):
    tap = lambda k: _row(w.conv_w[k], layer)
    return _row(w.conv_b[...], layer) + (tap(0) * back2 + tap(1) * back1 + tap(2) * cur)


def _out_proj(h, ds, y_conv, w):
    return h + _dot(jnp.concatenate(ds + [y_conv.astype(_BF16)], axis=-1), w.w_mix[...])


def _mlp(h, w, layer, chunk):
    hg, scale = _norm_split(h, _row(w.norm_mlp[...], layer))
    acc = h
    for c in range(D_FF // chunk):
        cs = slice(c * chunk, (c + 1) * chunk)
        f = jnp.maximum(_dot(hg, w.w_up[:, cs]) * scale, 0.0)
        acc = acc + _dot((f * f).astype(_BF16), w.w_down[cs, :])
    return acc


def _gated_embed(h, p, w, layer):
    hg, scale = _norm_split(h, _row(w.norm_ple[...], layer))
    gate = jax.nn.sigmoid(_dot(hg, w.w_gate[...]) * scale)
    return h + gate * _dot(p.astype(_BF16), w.w_proj[layer].astype(_BF16))


def _prompt_kernel(*refs, layer, tile, n_tiles, n_cast, own_cast):
    n_big = len(_BIG_FIELDS)
    n_w = len(_Weights._fields) - (n_big if own_cast else 0)
    x_ref, p_ref = refs[:2]
    pos = 2 + n_w
    prev = refs[pos:pos + 2 * layer]
    pos += 2 * layer
    cast_in = refs[pos:pos + n_cast]
    pos += n_cast
    if own_cast:
        big_hbm = refs[pos:pos + n_big]
        pos += n_big
    out_ref, npool_ref, nconv_ref = refs[pos:pos + 3]
    cast_out = refs[pos + 3:pos + 3 + n_cast]
    pos += 3 + n_cast
    if own_cast:
        big_out = refs[pos:pos + n_big]
        pool_ext, conv_ext = refs[pos + n_big:pos + n_big + 2]
        big_vmem = refs[pos + n_big + 2:pos + 2 * n_big + 2]
        stage, fetch_sem, publish_sem = refs[pos + 2 * n_big + 2:]
        w = _Weights(*refs[2:2 + n_w], *big_vmem)
    else:
        pool_ext, conv_ext = refs[pos:]
        w = _Weights(*refs[2:2 + n_w])
    i = pl.program_id(1)
    halo = POOL_HALO
    chalo = SUBLANES

    if own_cast:
        first = (pl.program_id(0) == 0) & (i == 0)
        last = (pl.program_id(0) == pl.num_programs(0) - 1) & (i == n_tiles - 1)
        rows, cols = stage.shape[1:]
        chunks = [(m, r0, c0) for m in range(n_big)
                  for r0 in range(0, big_vmem[m].shape[0], rows)
                  for c0 in range(0, big_vmem[m].shape[1], cols)]

        def fetch(k):
            m, r0, c0 = chunks[k]
            return pltpu.make_async_copy(big_hbm[m].at[layer, pl.ds(r0, rows), pl.ds(c0, cols)],
                                         stage.at[k % 2], fetch_sem.at[k % 2])

        def publish(m):
            return pltpu.make_async_copy(big_vmem[m], big_out[m], publish_sem.at[m])

        @pl.when(first)
        def _():
            fetch(0).start()
            for k, (m, r0, c0) in enumerate(chunks):
                if k + 1 < len(chunks):
                    fetch(k + 1).start()
                fetch(k).wait()
                big_vmem[m][r0:r0 + rows, c0:c0 + cols] = stage[k % 2].astype(_BF16)
            for m in range(n_big):
                publish(m).start()

        @pl.when(last)
        def _():
            for m in range(n_big):
                publish(m).wait()

    @pl.when(i == 0)
    def _():
        pool_ext[0:halo, :] = jnp.zeros((halo, POOL_WIDTH), _F32)
        conv_ext[0:chalo, :] = jnp.zeros((chalo, CONV_WIDTH), _F32)

    h = x_ref[...]
    u, bg, cv = _in_proj(h, w, layer)
    pool_ext[halo:halo + tile, :] = u
    conv_ext[chalo:chalo + tile, :] = cv

    for src, dst in zip(cast_in, cast_out):
        dst[...] = src[...].astype(_BF16)

    position = i * tile + lax.broadcasted_iota(jnp.int32, (tile, 1), 0)
    ds = []
    for g, win in enumerate(POOL_WINDOWS):
        e = pool_ext[:, g * POOL_GC:(g + 1) * POOL_GC]
        s = e
        span = 1
        while span < win:
            s = s + pltpu.roll(s, span, axis=0)
            span *= 2
        cnt = jnp.minimum(win, position + 1).astype(_F32)
        ds.append((s[halo:, :] / cnt - e[halo:, :]).astype(_BF16))
    cvx = conv_ext[...]
    z = _conv_out(w, layer, pltpu.roll(cvx, 2, axis=0)[chalo:, :],
                  pltpu.roll(cvx, 1, axis=0)[chalo:, :], cvx[chalo:, :])

    h = _mlp(_out_proj(h, ds, bg * z, w), w, layer, PROMPT_FF_CHUNK)
    h = _gated_embed(h, p_ref[...], w, layer)
    out_ref[...] = _rmsnorm(h, w.norm_f[...]) if layer == DEPTH - 1 else h

    @pl.when(i == n_tiles - 1)
    def _():
        seq = pl.ds(pl.program_id(0), 1)
        new_pool = pool_ext[tile + halo - POOL_BUF:tile + halo, :][:, None, :]
        new_conv = conv_ext[tile + chalo - CONV_BUF:tile + chalo, :]
        if layer == 0:
            npool_ref[:, seq, :] = new_pool
            nconv_ref[...] = new_conv
        else:
            for k in range(layer):
                npool_ref[k] = prev[2 * k][...]
                nconv_ref[k] = prev[2 * k + 1][...]
            npool_ref[layer, :, seq, :] = new_pool
            nconv_ref[layer] = new_conv

    @pl.when(i < n_tiles - 1)
    def _():
        pool_ext[0:halo, :] = pool_ext[tile:tile + halo, :]
        conv_ext[0:chalo, :] = conv_ext[tile:tile + chalo, :]


def _sample_kernel(*refs, tile, n_tiles):
    n_big = len(_BIG_FIELDS)
    x_ref, p_ref, pool_hist, conv_hist = refs[:4]
    shared = refs[4:4 + _MIX_FIELD]
    w_mix_hbm = refs[4 + _MIX_FIELD]
    pos = 5 + _MIX_FIELD
    per_layer = [refs[pos + k * n_big:pos + (k + 1) * n_big] for k in range(DEPTH)]
    pos += DEPTH * n_big
    out_ref, npool_ref, nconv_ref = refs[pos:pos + 3]
    buffers = refs[pos + 3:pos + 4 + n_big]
    h_mid, pool_ext, conv_ext, sem = refs[pos + 4 + n_big:]
    w_in_buf, w_mix_buf, w_up_buf, w_down_buf, w_gate_buf = buffers
    w = _Weights(*shared, w_mix=w_mix_buf, w_in=w_in_buf, w_up=w_up_buf, w_down=w_down_buf,
                 w_gate=w_gate_buf)
    early, late = (0, 1), (2, 3, 4)

    def load(k, m):
        w_in_k, w_up_k, w_down_k, w_gate_k = per_layer[k]
        source = (w_in_k, w_mix_hbm.at[k], w_up_k, w_down_k, w_gate_k)[m]
        return pltpu.make_async_copy(source, buffers[m], sem.at[m])

    step = pl.program_id(0)
    layer = step // n_tiles
    tile_index = step % n_tiles
    seqs = tile // SUBLANES
    halo = POOL_HALO * seqs
    chalo = CONV_BUF * seqs
    positions = range(SUBLANES)
    by_position = lambda ref: jnp.concatenate([ref[:, t, :] for t in positions], axis=0)

    for k in range(DEPTH):
        @pl.when(step == k * n_tiles)
        def _():
            if k == 0:
                for m in early + late:
                    load(0, m).start()
            for m in early:
                load(k, m).wait()

    @pl.when(layer == 0)
    def _():
        h_mid[tile_index] = by_position(x_ref)

    h = h_mid[tile_index]
    u, bg, cv = _in_proj(h, w, layer)

    pool_ext[0:seqs, :] = pool_hist[0]
    pool_ext[seqs:halo, :] = pool_hist[...].reshape(POOL_BUF * seqs, POOL_WIDTH)
    pool_ext[halo:halo + tile, :] = u
    for k in range(CONV_BUF):
        conv_ext[k * seqs:(k + 1) * seqs, :] = conv_hist[:, k, :]
    conv_ext[chalo:chalo + tile, :] = cv

    ds = []
    for g, win in enumerate(POOL_WINDOWS):
        sl = slice(g * POOL_GC, (g + 1) * POOL_GC)
        s = pool_ext[halo - (win - 1) * seqs:halo + tile, sl]
        span = 1
        while span < win:
            s = s[span * seqs:, :] + s[:-span * seqs, :]
            span *= 2
        ds.append((s * (1.0 / win) - u[:, sl]).astype(_BF16))
    z = _conv_out(w, layer, conv_ext[0:tile, :], conv_ext[seqs:seqs + tile, :], cv)

    h = _out_proj(h, ds, bg * z, w)

    for k in range(DEPTH):
        @pl.when(step == k * n_tiles)
        def _():
            for m in late:
                load(k, m).wait()

        if k + 1 < DEPTH:
            @pl.when(step == (k + 1) * n_tiles - 1)
            def _():
                for m in early:
                    load(k + 1, m).start()

    h = _mlp(h, w, layer, SAMPLE_FF_CHUNK)
    h = _gated_embed(h, by_position(p_ref), w, layer)

    @pl.when(layer < DEPTH - 1)
    def _():
        h_mid[tile_index] = h

    @pl.when(layer == DEPTH - 1)
    def _():
        y = _rmsnorm(h, w.norm_f[...])
        for t in positions:
            out_ref[:, t, :] = y[t * seqs:(t + 1) * seqs, :]

    for k in range(DEPTH - 1):
        @pl.when(step == (k + 1) * n_tiles - 1)
        def _():
            for m in late:
                load(k + 1, m).start()

    npool_ref[...] = pool_ext[halo + tile - POOL_BUF * seqs:halo + tile, :].reshape(
        POOL_BUF, seqs, POOL_WIDTH)
    for k in range(CONV_BUF):
        nconv_ref[:, k, :] = conv_ext[tile + k * seqs:tile + (k + 1) * seqs, :]


def _weight_specs(weights, layer):
    once = pl.Buffered(1)
    specs = [pl.BlockSpec(a.shape, lambda *_, nd=a.ndim: (0,) * nd, pipeline_mode=once)
             for a in weights]
    specs[_MIX_FIELD] = pl.BlockSpec((None,) + weights[_MIX_FIELD].shape[1:],
                                     lambda *_: (layer, 0, 0), pipeline_mode=once)
    return specs


def _prompt_layer(layer, x, p, weights, prev_state, cast_next, cast_own=()):
    batch, seq, _ = x.shape
    tile = TOKEN_TILE
    n_tiles = seq // tile
    steps = batch * n_tiles
    assert seq == n_tiles * tile and tile >= POOL_HALO and len(prev_state) == 2 * layer
    for a in cast_next:
        assert a.shape[1] % (steps * BF16_ROWS) == 0
    for a in cast_own:
        assert a.shape[1] % CAST_STAGE[0] == 0 and a.shape[2] % CAST_STAGE[1] == 0
    assert len(weights) + len(cast_own) == len(_Weights._fields)
    hbm = pl.BlockSpec(memory_space=pl.ANY)
    whole = lambda shape: pl.BlockSpec(shape, lambda b, i, nd=len(shape): (0,) * nd)
    per_batch = pl.BlockSpec((None, CONV_BUF, CONV_WIDTH), lambda b, i: (b, 0, 0))
    pool_shape = (POOL_BUF, batch, POOL_WIDTH)
    in_specs = [
        pl.BlockSpec((None, tile, D_MODEL), lambda b, i: (b, i, 0)),
        pl.BlockSpec((None, None, tile, PLE_DIM), lambda b, i: (layer, b, i, 0)),
    ] + _weight_specs(weights, layer) + [whole(pool_shape), per_batch] * layer + [
        pl.BlockSpec((None, a.shape[1] // steps, a.shape[2]),
                     lambda b, i: (layer + 1, b * n_tiles + i, 0)) for a in cast_next] + [
        hbm] * len(cast_own)
    out_specs = [pl.BlockSpec((None, tile, D_MODEL), lambda b, i: (b, i, 0))]
    out_shape = [jax.ShapeDtypeStruct((batch, seq, D_MODEL), _F32)]
    if layer == 0:
        out_specs += [whole(pool_shape), per_batch]
        out_shape += [jax.ShapeDtypeStruct(pool_shape, _F32),
                      jax.ShapeDtypeStruct((batch, CONV_BUF, CONV_WIDTH), _F32)]
    else:
        out_specs += [whole((layer + 1,) + pool_shape),
                      pl.BlockSpec((layer + 1, None, CONV_BUF, CONV_WIDTH), lambda b, i: (0, b, 0, 0))]
        out_shape += [jax.ShapeDtypeStruct((layer + 1,) + pool_shape, _F32),
                      jax.ShapeDtypeStruct((layer + 1, batch, CONV_BUF, CONV_WIDTH), _F32)]
    out_specs += [pl.BlockSpec((a.shape[1] // steps, a.shape[2]), lambda b, i: (b * n_tiles + i, 0))
                  for a in cast_next]
    out_shape += [jax.ShapeDtypeStruct(a.shape[1:], _BF16) for a in cast_next]
    scratch_shapes = [pltpu.VMEM((POOL_HALO + tile, POOL_WIDTH), _F32),
                      pltpu.VMEM((SUBLANES + tile, CONV_WIDTH), _F32)]
    if cast_own:
        out_specs += [hbm] * len(cast_own)
        out_shape += [jax.ShapeDtypeStruct(a.shape[1:], _BF16) for a in cast_own]
        scratch_shapes += [pltpu.VMEM(a.shape[1:], _BF16) for a in cast_own] + [
            pltpu.VMEM((2,) + CAST_STAGE, _F32),
            pltpu.SemaphoreType.DMA((2,)), pltpu.SemaphoreType.DMA((len(cast_own),))]
    return pl.pallas_call(
        functools.partial(_prompt_kernel, layer=layer, tile=tile, n_tiles=n_tiles,
                          n_cast=len(cast_next), own_cast=bool(cast_own)),
        grid=(batch, n_tiles),
        in_specs=in_specs,
        out_specs=out_specs,
        out_shape=out_shape,
        scratch_shapes=scratch_shapes,
        compiler_params=pltpu.CompilerParams(
            dimension_semantics=("arbitrary", "arbitrary"),
            vmem_limit_bytes=VMEM_LIMIT_BYTES),
        name=f"prompt_layer{layer}",
    )(x, p, *weights, *prev_state, *cast_next, *cast_own)


def _sample_trunk(x, p, state_pool, state_conv, shared, w_mix, per_layer):
    n_seq = x.shape[0]
    seqs = TOKEN_TILE // SUBLANES
    tile = seqs * SUBLANES
    n_tiles = n_seq // seqs
    assert n_seq == n_tiles * seqs and x.shape[1] == SUBLANES and len(shared) == _MIX_FIELD
    assert len(per_layer) == DEPTH and all(len(big) == len(_BIG_FIELDS) for big in per_layer)
    at = lambda fn: lambda s: fn(s // n_tiles, s % n_tiles)
    last_layer_tile = lambda s: (jnp.maximum(s - (DEPTH - 1) * n_tiles, 0), 0, 0)
    hbm = pl.BlockSpec(memory_space=pl.ANY)
    big = [a for layer_big in per_layer for a in layer_big]
    in_specs = [
        pl.BlockSpec((seqs, SUBLANES, D_MODEL), at(lambda l, i: (i, 0, 0))),
        pl.BlockSpec((None, seqs, SUBLANES, PLE_DIM), at(lambda l, i: (l, i, 0, 0))),
        pl.BlockSpec((None, POOL_BUF, seqs, POOL_WIDTH), at(lambda l, i: (l, 0, i, 0)),
                     pipeline_mode=pl.Buffered(1)),
        pl.BlockSpec((None, seqs, CONV_BUF, CONV_WIDTH), at(lambda l, i: (l, i, 0, 0)),
                     pipeline_mode=pl.Buffered(1)),
    ] + [pl.BlockSpec(a.shape, lambda s, nd=a.ndim: (0,) * nd, pipeline_mode=pl.Buffered(1))
         for a in shared] + [hbm] * (1 + len(big))
    out_specs = [
        pl.BlockSpec((seqs, SUBLANES, D_MODEL), last_layer_tile),
        pl.BlockSpec((None, POOL_BUF, seqs, POOL_WIDTH), at(lambda l, i: (l, 0, i, 0))),
        pl.BlockSpec((None, seqs, CONV_BUF, CONV_WIDTH), at(lambda l, i: (l, i, 0, 0))),
    ]
    out_shape = [jax.ShapeDtypeStruct((n_seq, SUBLANES, D_MODEL), _F32),
                 jax.ShapeDtypeStruct((DEPTH, POOL_BUF, n_seq, POOL_WIDTH), _F32),
                 jax.ShapeDtypeStruct((DEPTH, n_seq, CONV_BUF, CONV_WIDTH), _F32)]
    w_in, w_up, w_down, w_gate = per_layer[0]
    buffers = [w_in.shape, w_mix.shape[1:], w_up.shape, w_down.shape, w_gate.shape]
    return pl.pallas_call(
        functools.partial(_sample_kernel, tile=tile, n_tiles=n_tiles),
        grid=(DEPTH * n_tiles,),
        in_specs=in_specs,
        out_specs=out_specs,
        out_shape=out_shape,
        scratch_shapes=[pltpu.VMEM(shape, _BF16) for shape in buffers] + [
            pltpu.VMEM((n_tiles, tile, D_MODEL), _F32),
            pltpu.VMEM(((POOL_HALO + SUBLANES) * seqs, POOL_WIDTH), _F32),
            pltpu.VMEM(((CONV_BUF + SUBLANES) * seqs, CONV_WIDTH), _F32),
            pltpu.SemaphoreType.DMA((len(buffers),))],
        compiler_params=pltpu.CompilerParams(
            dimension_semantics=("arbitrary",),
            vmem_limit_bytes=VMEM_LIMIT_BYTES),
        name="sample_trunk",
    )(x, p, state_pool, state_conv, *shared, w_mix, *big)


def kernel(x_prompt, x_sample, state_pool, state_conv, p_prompt, p_sample, norm_mix, w_in, pool_w, pool_scale, conv_w, conv_b, w_out, norm_mlp, w_up, w_down, norm_ple, w_ple_gate, w_ple_proj, norm_f):
    assert DEPTH == 2 and PAST_LEN >= POOL_BUF
    shared = (norm_mix, jnp.swapaxes(conv_w, 0, 1), conv_b, norm_mlp, norm_ple,
              norm_f.reshape(1, D_MODEL), w_ple_proj)
    big_f32 = (w_in, w_up, w_down, w_ple_gate)

    w_mix = _fold_pool(pool_w, pool_scale, w_out)

    n_big = len(big_f32)
    h, pool_p, conv_p, *big_bf16 = _prompt_layer(0, x_prompt, p_prompt, shared + (w_mix,), (),
                                                 big_f32, big_f32)
    per_layer = [big_bf16[n_big:], big_bf16[:n_big]]
    y_prompt, pool_p, conv_p = _prompt_layer(1, h, p_prompt, _Weights(*shared, w_mix, *per_layer[1]),
                                             (pool_p, conv_p), ())

    y_sample, pool_s, conv_s = _sample_trunk(x_sample, p_sample, jnp.swapaxes(state_pool, 1, 2),
                                             state_conv, shared, w_mix, per_layer)
    return (y_prompt, y_sample, jnp.swapaxes(pool_p, 1, 2), conv_p, jnp.swapaxes(pool_s, 1, 2), conv_s)
```

```python
import functools
from typing import NamedTuple

import jax
import jax.numpy as jnp
from jax import lax
from jax.experimental import pallas as pl
from jax.experimental.pallas import tpu as pltpu

D_MODEL = 1024
DEPTH = 2
PAST_LEN = 16384
POOL_WIDTH = 512
CONV_WIDTH = 512
POOL_WINDOWS = (2, 4, 8, 16)
POOL_GC = 128
POOL_BUF = 15
CONV_BUF = 2
D_FF = 4 * D_MODEL
PLE_DIM = 256
IN_COLS = POOL_WIDTH + 3 * CONV_WIDTH
EPS = 1e-6

SUBLANES = 8
BF16_ROWS = 16
POOL_HALO = POOL_BUF + 1
TOKEN_TILE = 512
PROMPT_FF_CHUNK = 1024
SAMPLE_FF_CHUNK = 512
V7X_VMEM_BYTES = 64 * 1024 * 1024
VMEM_LIMIT_BYTES = V7X_VMEM_BYTES * 7 // 8

_BF16 = jnp.bfloat16
_F32 = jnp.float32


class _Weights(NamedTuple):
    norm_mix: jax.Array
    conv_w: jax.Array
    conv_b: jax.Array
    norm_mlp: jax.Array
    norm_ple: jax.Array
    norm_f: jax.Array
    w_proj: jax.Array
    w_mix: jax.Array
    w_in: jax.Array
    w_up: jax.Array
    w_down: jax.Array
    w_gate: jax.Array


_BIG_FIELDS = _Weights._fields[-4:]
_MIX_FIELD = _Weights._fields.index("w_mix")
CAST_STAGE = (512, 1024)


def _rmsnorm(x, g):
    ms = jnp.mean(x * x, axis=-1, keepdims=True)
    return (x * lax.rsqrt(ms + EPS)) * g


def _dot(a, b):
    return jnp.dot(a, b, preferred_element_type=_F32)


def _norm_split(x, g):
    scale = lax.rsqrt(jnp.mean(x * x, axis=-1, keepdims=True) + EPS)
    return (x * g).astype(_BF16), scale


def _row(vec, layer):
    if isinstance(layer, int):
        return vec[layer:layer + 1, :]
    row = vec[0:1, :]
    for k in range(1, DEPTH):
        row = jnp.where(layer == k, vec[k:k + 1, :], row)
    return row


def _in_proj(h, w, layer):
    hg, scale = _norm_split(h, _row(w.norm_mix[...], layer))
    proj = _dot(hg, w.w_in[...])
    u = proj[:, :POOL_WIDTH] * scale
    bg = proj[:, POOL_WIDTH:POOL_WIDTH + CONV_WIDTH] * scale
    cg = proj[:, POOL_WIDTH + CONV_WIDTH:POOL_WIDTH + 2 * CONV_WIDTH] * scale
    v = proj[:, POOL_WIDTH + 2 * CONV_WIDTH:] * scale
    return u, bg, cg * v


def _fold_kernel(pool_w_ref, pool_scale_ref, w_out_ref, w_mix_ref):
    def split(a):
        head = a.astype(_BF16)
        return head, (a - head.astype(_F32)).astype(_BF16)

    pooling_half = pl.program_id(1) == 0
    for layer in range(DEPTH):
        @pl.when(pooling_half & (pl.program_id(0) == layer))
        def _():
            for g in range(len(POOL_WINDOWS)):
                sl = slice(g * POOL_GC, (g + 1) * POOL_GC)
                a_head, a_rem = split(pool_w_ref[g] * pool_scale_ref[layer:layer + 1, sl])
                b_head, b_rem = split(w_out_ref[sl, :])
                w_mix_ref[sl, :] = _dot(jnp.concatenate([a_head, a_head, a_rem], axis=1),
                                        jnp.concatenate([b_head, b_rem, b_head], axis=0)).astype(_BF16)

    @pl.when(jnp.logical_not(pooling_half))
    def _():
        w_mix_ref[...] = w_out_ref[...].astype(_BF16)


def _fold_pool(pool_w, pool_scale, w_out):
    assert w_out.shape[1] == 2 * POOL_WIDTH
    half = (None, POOL_WIDTH, w_out.shape[2])
    return pl.pallas_call(
        _fold_kernel,
        grid=(DEPTH, 2),
        in_specs=[pl.BlockSpec((None,) + pool_w.shape[1:], lambda l, k: (l, 0, 0, 0)),
                  pl.BlockSpec(pool_scale.shape, lambda l, k: (0, 0)),
                  pl.BlockSpec(half, lambda l, k: (l, k, 0))],
        out_specs=pl.BlockSpec(half, lambda l, k: (l, k, 0)),
        out_shape=jax.ShapeDtypeStruct(w_out.shape, _BF16),
        compiler_params=pltpu.CompilerParams(dimension_semantics=("arbitrary", "arbitrary")),
        name="fold_pool_maps",
    )(pool_w, pool_scale, w_out)


def _conv_out(w, layer, back2, back1, cur):
    tap = lambda k: _row(w.conv_w[k], layer)
    return _row(w.conv_b[...], layer) + (tap(0) * back2 + tap(1) * back1 + tap(2) * cur)


def _out_proj(h, ds, y_conv, w):
    return h + _dot(jnp.concatenate(ds + [y_conv.astype(_BF16)], axis=-1), w.w_mix[...])


def _mlp(h, w, layer, chunk):
    hg, scale = _norm_split(h, _row(w.norm_mlp[...], layer))
    acc = h
    for c in range(D_FF // chunk):
        cs = slice(c * chunk, (c + 1) * chunk)
        f = jnp.maximum(_dot(hg, w.w_up[:, cs]) * scale, 0.0)
        acc = acc + _dot((f * f).astype(_BF16), w.w_down[cs, :])
    return acc


def _gated_embed(h, p, w, layer):
    hg, scale = _norm_split(h, _row(w.norm_ple[...], layer))
    gate = jax.nn.sigmoid(_dot(hg, w.w_gate[...]) * scale)
    return h + gate * _dot(p.astype(_BF16), w.w_proj[layer].astype(_BF16))


def _prompt_kernel(*refs, layer, tile, n_tiles, n_cast, own_cast):
    n_big = len(_BIG_FIELDS)
    n_w = len(_Weights._fields) - (n_big if own_cast else 0)
    x_ref, p_ref = refs[:2]
    pos = 2 + n_w
    prev = refs[pos:pos + 2 * layer]
    pos += 2 * layer
    cast_in = refs[pos:pos + n_cast]
    pos += n_cast
    if own_cast:
        big_hbm = refs[pos:pos + n_big]
        pos += n_big
    out_ref, npool_ref, nconv_ref = refs[pos:pos + 3]
    cast_out = refs[pos + 3:pos + 3 + n_cast]
    pos += 3 + n_cast
    if own_cast:
        big_out = refs[pos:pos + n_big]
        pool_ext, conv_ext = refs[pos + n_big:pos + n_big + 2]
        big_vmem = refs[pos + n_big + 2:pos + 2 * n_big + 2]
        stage, fetch_sem, publish_sem = refs[pos + 2 * n_big + 2:]
        w = _Weights(*refs[2:2 + n_w], *big_vmem)
    else:
        pool_ext, conv_ext = refs[pos:]
        w = _Weights(*refs[2:2 + n_w])
    i = pl.program_id(1)
    halo = POOL_HALO
    chalo = SUBLANES

    if own_cast:
        first = (pl.program_id(0) == 0) & (i == 0)
        last = (pl.program_id(0) == pl.num_programs(0) - 1) & (i == n_tiles - 1)
        rows, cols = stage.shape[1:]
        chunks = [(m, r0, c0) for m in range(n_big)
                  for r0 in range(0, big_vmem[m].shape[0], rows)
                  for c0 in range(0, big_vmem[m].shape[1], cols)]

        def fetch(k):
            m, r0, c0 = chunks[k]
            return pltpu.make_async_copy(big_hbm[m].at[layer, pl.ds(r0, rows), pl.ds(c0, cols)],
                                         stage.at[k % 2], fetch_sem.at[k % 2])

        def publish(m):
            return pltpu.make_async_copy(big_vmem[m], big_out[m], publish_sem.at[m])

        @pl.when(first)
        def _():
            fetch(0).start()
            for k, (m, r0, c0) in enumerate(chunks):
                if k + 1 < len(chunks):
                    fetch(k + 1).start()
                fetch(k).wait()
                big_vmem[m][r0:r0 + rows, c0:c0 + cols] = stage[k % 2].astype(_BF16)
            for m in range(n_big):
                publish(m).start()

        @pl.when(last)
        def _():
            for m in range(n_big):
                publish(m).wait()

    @pl.when(i == 0)
    def _():
        pool_ext[0:halo, :] = jnp.zeros((halo, POOL_WIDTH), _F32)
        conv_ext[0:chalo, :] = jnp.zeros((chalo, CONV_WIDTH), _F32)

    h = x_ref[...]
    u, bg, cv = _in_proj(h, w, layer)
    pool_ext[halo:halo + tile, :] = u
    conv_ext[chalo:chalo + tile, :] = cv

    for src, dst in zip(cast_in, cast_out):
        dst[...] = src[...].astype(_BF16)

    position = i * tile + lax.broadcasted_iota(jnp.int32, (tile, 1), 0)
    ds = []
    for g, win in enumerate(POOL_WINDOWS):
        e = pool_ext[:, g * POOL_GC:(g + 1) * POOL_GC]
        s = e
        span = 1
        while span < win:
            s = s + pltpu.roll(s, span, axis=0)
            span *= 2
        cnt = jnp.minimum(win, position + 1).astype(_F32)
        ds.append((s[halo:, :] / cnt - e[halo:, :]).astype(_BF16))
    cvx = conv_ext[...]
    z = _conv_out(w, layer, pltpu.roll(cvx, 2, axis=0)[chalo:, :],
                  pltpu.roll(cvx, 1, axis=0)[chalo:, :], cvx[chalo:, :])

    h = _mlp(_out_proj(h, ds, bg * z, w), w, layer, PROMPT_FF_CHUNK)
    h = _gated_embed(h, p_ref[...], w, layer)
    out_ref[...] = _rmsnorm(h, w.norm_f[...]) if layer == DEPTH - 1 else h

    @pl.when(i == n_tiles - 1)
    def _():
        seq = pl.ds(pl.program_id(0), 1)
        new_pool = pool_ext[tile + halo - POOL_BUF:tile + halo, :][:, None, :]
        new_conv = conv_ext[tile + chalo - CONV_BUF:tile + chalo, :]
        if layer == 0:
            npool_ref[:, seq, :] = new_pool
            nconv_ref[...] = new_conv
        else:
            for k in range(layer):
                npool_ref[k] = prev[2 * k][...]
                nconv_ref[k] = prev[2 * k + 1][...]
            npool_ref[layer, :, seq, :] = new_pool
            nconv_ref[layer] = new_conv

    @pl.when(i < n_tiles - 1)
    def _():
        pool_ext[0:halo, :] = pool_ext[tile:tile + halo, :]
        conv_ext[0:chalo, :] = conv_ext[tile:tile + chalo, :]


def _sample_kernel(*refs, tile, n_tiles):
    n_big = len(_BIG_FIELDS)
    x_ref, p_ref, pool_hist, conv_hist = refs[:4]
    shared = refs[4:4 + _MIX_FIELD]
    w_mix_hbm = refs[4 + _MIX_FIELD]
    pos = 5 + _MIX_FIELD
    per_layer = [refs[pos + k * n_big:pos + (k + 1) * n_big] for k in range(DEPTH)]
    pos += DEPTH * n_big
    out_ref, npool_ref, nconv_ref = refs[pos:pos + 3]
    buffers = refs[pos + 3:pos + 4 + n_big]
    h_mid, pool_ext, conv_ext, sem = refs[pos + 4 + n_big:]
    w_in_buf, w_mix_buf, w_up_buf, w_down_buf, w_gate_buf = buffers
    w = _Weights(*shared, w_mix=w_mix_buf, w_in=w_in_buf, w_up=w_up_buf, w_down=w_down_buf,
                 w_gate=w_gate_buf)
    early, late = (0, 1), (2, 3, 4)

    def load(k, m):
        w_in_k, w_up_k, w_down_k, w_gate_k = per_layer[k]
        source = (w_in_k, w_mix_hbm.at[k], w_up_k, w_down_k, w_gate_k)[m]
        return pltpu.make_async_copy(source, buffers[m], sem.at[m])

    step = pl.program_id(0)
    layer = step // (2 * n_tiles)
    mlp_half = (step // n_tiles) % 2 == 1
    tile_index = step % n_tiles
    seqs = tile // SUBLANES
    halo = POOL_HALO * seqs
    chalo = CONV_BUF * seqs
    positions = range(SUBLANES)
    by_position = lambda ref: jnp.concatenate([ref[:, t, :] for t in positions], axis=0)

    for k in range(DEPTH):
        @pl.when(step == 2 * k * n_tiles)
        def _():
            if k == 0:
                for m in early + late:
                    load(0, m).start()
            else:
                for m in late:
                    load(k, m).start()
            for m in early:
                load(k, m).wait()

        @pl.when(step == (2 * k + 1) * n_tiles)
        def _():
            for m in late:
                load(k, m).wait()
            if k + 1 < DEPTH:
                for m in early:
                    load(k + 1, m).start()

    @pl.when(jnp.logical_not(mlp_half))
    def _():
        @pl.when(layer == 0)
        def _():
            h_mid[tile_index] = by_position(x_ref)

        h = h_mid[tile_index]
        u, bg, cv = _in_proj(h, w, layer)

        pool_ext[0:seqs, :] = pool_hist[0]
        pool_ext[seqs:halo, :] = pool_hist[...].reshape(POOL_BUF * seqs, POOL_WIDTH)
        pool_ext[halo:halo + tile, :] = u
        for k in range(CONV_BUF):
            conv_ext[k * seqs:(k + 1) * seqs, :] = conv_hist[:, k, :]
        conv_ext[chalo:chalo + tile, :] = cv

        ds = []
        for g, win in enumerate(POOL_WINDOWS):
            sl = slice(g * POOL_GC, (g + 1) * POOL_GC)
            s = pool_ext[halo - (win - 1) * seqs:halo + tile, sl]
            span = 1
            while span < win:
                s = s[span * seqs:, :] + s[:-span * seqs, :]
                span *= 2
            ds.append((s * (1.0 / win) - u[:, sl]).astype(_BF16))
        z = _conv_out(w, layer, conv_ext[0:tile, :], conv_ext[seqs:seqs + tile, :], cv)

        h_mid[tile_index] = _out_proj(h, ds, bg * z, w)
        npool_ref[...] = pool_ext[halo + tile - POOL_BUF * seqs:halo + tile, :].reshape(
            POOL_BUF, seqs, POOL_WIDTH)
        for k in range(CONV_BUF):
            nconv_ref[:, k, :] = conv_ext[tile + k * seqs:tile + (k + 1) * seqs, :]

    @pl.when(mlp_half)
    def _():
        h = _mlp(h_mid[tile_index], w, layer, SAMPLE_FF_CHUNK)
        h = _gated_embed(h, by_position(p_ref), w, layer)

        @pl.when(layer < DEPTH - 1)
        def _():
            h_mid[tile_index] = h

        @pl.when(layer == DEPTH - 1)
        def _():
            y = _rmsnorm(h, w.norm_f[...])
            for t in positions:
                out_ref[:, t, :] = y[t * seqs:(t + 1) * seqs, :]


def _weight_specs(weights, layer):
    once = pl.Buffered(1)
    specs = [pl.BlockSpec(a.shape, lambda *_, nd=a.ndim: (0,) * nd, pipeline_mode=once)
             for a in weights]
    specs[_MIX_FIELD] = pl.BlockSpec((None,) + weights[_MIX_FIELD].shape[1:],
                                     lambda *_: (layer, 0, 0), pipeline_mode=once)
    return specs


def _prompt_layer(layer, x, p, weights, prev_state, cast_next, cast_own=()):
    batch, seq, _ = x.shape
    tile = TOKEN_TILE
    n_tiles = seq // tile
    steps = batch * n_tiles
    assert seq == n_tiles * tile and tile >= POOL_HALO and len(prev_state) == 2 * layer
    for a in cast_next:
        assert a.shape[1] % (steps * BF16_ROWS) == 0
    for a in cast_own:
        assert a.shape[1] % CAST_STAGE[0] == 0 and a.shape[2] % CAST_STAGE[1] == 0
    assert len(weights) + len(cast_own) == len(_Weights._fields)
    hbm = pl.BlockSpec(memory_space=pl.ANY)
    whole = lambda shape: pl.BlockSpec(shape, lambda b, i, nd=len(shape): (0,) * nd)
    per_batch = pl.BlockSpec((None, CONV_BUF, CONV_WIDTH), lambda b, i: (b, 0, 0))
    pool_shape = (POOL_BUF, batch, POOL_WIDTH)
    in_specs = [
        pl.BlockSpec((None, tile, D_MODEL), lambda b, i: (b, i, 0)),
        pl.BlockSpec((None, None, tile, PLE_DIM), lambda b, i: (layer, b, i, 0)),
    ] + _weight_specs(weights, layer) + [whole(pool_shape), per_batch] * layer + [
        pl.BlockSpec((None, a.shape[1] // steps, a.shape[2]),
                     lambda b, i: (layer + 1, b * n_tiles + i, 0)) for a in cast_next] + [
        hbm] * len(cast_own)
    out_specs = [pl.BlockSpec((None, tile, D_MODEL), lambda b, i: (b, i, 0))]
    out_shape = [jax.ShapeDtypeStruct((batch, seq, D_MODEL), _F32)]
    if layer == 0:
        out_specs += [whole(pool_shape), per_batch]
        out_shape += [jax.ShapeDtypeStruct(pool_shape, _F32),
                      jax.ShapeDtypeStruct((batch, CONV_BUF, CONV_WIDTH), _F32)]
    else:
        out_specs += [whole((layer + 1,) + pool_shape),
                      pl.BlockSpec((layer + 1, None, CONV_BUF, CONV_WIDTH), lambda b, i: (0, b, 0, 0))]
        out_shape += [jax.ShapeDtypeStruct((layer + 1,) + pool_shape, _F32),
                      jax.ShapeDtypeStruct((layer + 1, batch, CONV_BUF, CONV_WIDTH), _F32)]
    out_specs += [pl.BlockSpec((a.shape[1] // steps, a.shape[2]), lambda b, i: (b * n_tiles + i, 0))
                  for a in cast_next]
    out_shape += [jax.ShapeDtypeStruct(a.shape[1:], _BF16) for a in cast_next]
    scratch_shapes = [pltpu.VMEM((POOL_HALO + tile, POOL_WIDTH), _F32),
                      pltpu.VMEM((SUBLANES + tile, CONV_WIDTH), _F32)]
    if cast_own:
        out_specs += [hbm] * len(cast_own)
        out_shape += [jax.ShapeDtypeStruct(a.shape[1:], _BF16) for a in cast_own]
        scratch_shapes += [pltpu.VMEM(a.shape[1:], _BF16) for a in cast_own] + [
            pltpu.VMEM((2,) + CAST_STAGE, _F32),
            pltpu.SemaphoreType.DMA((2,)), pltpu.SemaphoreType.DMA((len(cast_own),))]
    return pl.pallas_call(
        functools.partial(_prompt_kernel, layer=layer, tile=tile, n_tiles=n_tiles,
                          n_cast=len(cast_next), own_cast=bool(cast_own)),
        grid=(batch, n_tiles),
        in_specs=in_specs,
        out_specs=out_specs,
        out_shape=out_shape,
        scratch_shapes=scratch_shapes,
        compiler_params=pltpu.CompilerParams(
            dimension_semantics=("arbitrary", "arbitrary"),
            vmem_limit_bytes=VMEM_LIMIT_BYTES),
        name=f"prompt_layer{layer}",
    )(x, p, *weights, *prev_state, *cast_next, *cast_own)


def _sample_trunk(x, p, state_pool, state_conv, shared, w_mix, per_layer):
    n_seq = x.shape[0]
    seqs = TOKEN_TILE // SUBLANES
    tile = seqs * SUBLANES
    n_tiles = n_seq // seqs
    assert n_seq == n_tiles * seqs and x.shape[1] == SUBLANES and len(shared) == _MIX_FIELD
    assert len(per_layer) == DEPTH and all(len(big) == len(_BIG_FIELDS) for big in per_layer)
    last = n_tiles - 1

    def at(mixing, mlp):
        def index(s):
            layer, tile_index = s // (2 * n_tiles), s % n_tiles
            in_mlp_half = (s // n_tiles) % 2 == 1
            a, b = mixing(layer, tile_index), mlp(layer, tile_index)
            return tuple(jnp.where(in_mlp_half, j, i) for i, j in zip(a, b))
        return index

    hbm = pl.BlockSpec(memory_space=pl.ANY)
    big = [a for layer_big in per_layer for a in layer_big]
    state_rows = at(lambda l, i: (l, 0, i, 0), lambda l, i: (l, 0, last, 0))
    conv_rows = at(lambda l, i: (l, i, 0, 0), lambda l, i: (l, last, 0, 0))
    in_specs = [
        pl.BlockSpec((seqs, SUBLANES, D_MODEL), lambda s: (jnp.minimum(s, last), 0, 0)),
        pl.BlockSpec((None, seqs, SUBLANES, PLE_DIM),
                     at(lambda l, i: (l, 0, 0, 0), lambda l, i: (l, i, 0, 0))),
        pl.BlockSpec((None, POOL_BUF, seqs, POOL_WIDTH), state_rows,
                     pipeline_mode=pl.Buffered(1)),
        pl.BlockSpec((None, seqs, CONV_BUF, CONV_WIDTH), conv_rows, pipeline_mode=pl.Buffered(1)),
    ] + [pl.BlockSpec(a.shape, lambda s, nd=a.ndim: (0,) * nd, pipeline_mode=pl.Buffered(1))
         for a in shared] + [hbm] * (1 + len(big))
    out_specs = [
        pl.BlockSpec((seqs, SUBLANES, D_MODEL),
                     at(lambda l, i: (0, 0, 0),
                        lambda l, i: (jnp.where(l == DEPTH - 1, i, 0), 0, 0))),
        pl.BlockSpec((None, POOL_BUF, seqs, POOL_WIDTH), state_rows),
        pl.BlockSpec((None, seqs, CONV_BUF, CONV_WIDTH), conv_rows),
    ]
    out_shape = [jax.ShapeDtypeStruct((n_seq, SUBLANES, D_MODEL), _F32),
                 jax.ShapeDtypeStruct((DEPTH, POOL_BUF, n_seq, POOL_WIDTH), _F32),
                 jax.ShapeDtypeStruct((DEPTH, n_seq, CONV_BUF, CONV_WIDTH), _F32)]
    w_in, w_up, w_down, w_gate = per_layer[0]
    buffers = [w_in.shape, w_mix.shape[1:], w_up.shape, w_down.shape, w_gate.shape]
    return pl.pallas_call(
        functools.partial(_sample_kernel, tile=tile, n_tiles=n_tiles),
        grid=(2 * DEPTH * n_tiles,),
        in_specs=in_specs,
        out_specs=out_specs,
        out_shape=out_shape,
        scratch_shapes=[pltpu.VMEM(shape, _BF16) for shape in buffers] + [
            pltpu.VMEM((n_tiles, tile, D_MODEL), _F32),
            pltpu.VMEM(((POOL_HALO + SUBLANES) * seqs, POOL_WIDTH), _F32),
            pltpu.VMEM(((CONV_BUF + SUBLANES) * seqs, CONV_WIDTH), _F32),
            pltpu.SemaphoreType.DMA((len(buffers),))],
        compiler_params=pltpu.CompilerParams(
            dimension_semantics=("arbitrary",),
            vmem_limit_bytes=VMEM_LIMIT_BYTES),
        name="sample_trunk",
    )(x, p, state_pool, state_conv, *shared, w_mix, *big)


def kernel(x_prompt, x_sample, state_pool, state_conv, p_prompt, p_sample, norm_mix, w_in, pool_w, pool_scale, conv_w, conv_b, w_out, norm_mlp, w_up, w_down, norm_ple, w_ple_gate, w_ple_proj, norm_f):
    assert DEPTH == 2 and PAST_LEN >= POOL_BUF
    shared = (norm_mix, jnp.swapaxes(conv_w, 0, 1), conv_b, norm_mlp, norm_ple,
              norm_f.reshape(1, D_MODEL), w_ple_proj)
    big_f32 = (w_in, w_up, w_down, w_ple_gate)

    w_mix = _fold_pool(pool_w, pool_scale, w_out)

    n_big = len(big_f32)
    h, pool_p, conv_p, *big_bf16 = _prompt_layer(0, x_prompt, p_prompt, shared + (w_mix,), (),
                                                 big_f32, big_f32)
    per_layer = [big_bf16[n_big:], big_bf16[:n_big]]
    y_prompt, pool_p, conv_p = _prompt_layer(1, h, p_prompt, _Weights(*shared, w_mix, *per_layer[1]),
                                             (pool_p, conv_p), ())

    y_sample, pool_s, conv_s = _sample_trunk(x_sample, p_sample, jnp.swapaxes(state_pool, 1, 2),
                                             state_conv, shared, w_mix, per_layer)
    return (y_prompt, y_sample, jnp.swapaxes(pool_p, 1, 2), conv_p, jnp.swapaxes(pool_s, 1, 2), conv_s)
```

```python
import functools
from typing import NamedTuple

import jax
import jax.numpy as jnp
from jax import lax
from jax.experimental import pallas as pl
from jax.experimental.pallas import tpu as pltpu

D_MODEL = 1024
DEPTH = 2
PAST_LEN = 16384
POOL_WIDTH = 512
CONV_WIDTH = 512
POOL_WINDOWS = (2, 4, 8, 16)
POOL_GC = 128
POOL_BUF = 15
CONV_BUF = 2
D_FF = 4 * D_MODEL
PLE_DIM = 256
IN_COLS = POOL_WIDTH + 3 * CONV_WIDTH
EPS = 1e-6

SUBLANES = 8
BF16_ROWS = 16
POOL_HALO = POOL_BUF + 1
TOKEN_TILE = 512
PROMPT_FF_CHUNK = 1024
SAMPLE_FF_CHUNK = 512
V7X_VMEM_BYTES = 64 * 1024 * 1024
VMEM_LIMIT_BYTES = V7X_VMEM_BYTES * 7 // 8

_BF16 = jnp.bfloat16
_F32 = jnp.float32


class _Weights(NamedTuple):
    norm_mix: jax.Array
    conv_w: jax.Array
    conv_b: jax.Array
    norm_mlp: jax.Array
    norm_ple: jax.Array
    norm_f: jax.Array
    w_proj: jax.Array
    w_mix: jax.Array
    w_in: jax.Array
    w_up: jax.Array
    w_down: jax.Array
    w_gate: jax.Array


_BIG_FIELDS = _Weights._fields[-4:]
_MIX_FIELD = _Weights._fields.index("w_mix")
CAST_STAGE = (512, 1024)


def _rmsnorm(x, g):
    ms = jnp.mean(x * x, axis=-1, keepdims=True)
    return (x * lax.rsqrt(ms + EPS)) * g


def _dot(a, b):
    return jnp.dot(a, b, preferred_element_type=_F32)


def _norm_split(x, g):
    scale = lax.rsqrt(jnp.mean(x * x, axis=-1, keepdims=True) + EPS)
    return (x * g).astype(_BF16), scale


def _row(vec, layer):
    if isinstance(layer, int):
        return vec[layer:layer + 1, :]
    row = vec[0:1, :]
    for k in range(1, DEPTH):
        row = jnp.where(layer == k, vec[k:k + 1, :], row)
    return row


def _in_proj(h, w, layer):
    hg, scale = _norm_split(h, _row(w.norm_mix[...], layer))
    proj = _dot(hg, w.w_in[...])
    u = proj[:, :POOL_WIDTH] * scale
    bg = proj[:, POOL_WIDTH:POOL_WIDTH + CONV_WIDTH] * scale
    cg = proj[:, POOL_WIDTH + CONV_WIDTH:POOL_WIDTH + 2 * CONV_WIDTH] * scale
    v = proj[:, POOL_WIDTH + 2 * CONV_WIDTH:] * scale
    return u, bg, cg * v


def _fold_kernel(pool_w_ref, pool_scale_ref, w_out_ref, w_mix_ref):
    def split(a):
        head = a.astype(_BF16)
        return head, (a - head.astype(_F32)).astype(_BF16)

    pooling_half = pl.program_id(1) == 0
    for layer in range(DEPTH):
        @pl.when(pooling_half & (pl.program_id(0) == layer))
        def _():
            for g in range(len(POOL_WINDOWS)):
                sl = slice(g * POOL_GC, (g + 1) * POOL_GC)
                a_head, a_rem = split(pool_w_ref[g] * pool_scale_ref[layer:layer + 1, sl])
                b_head, b_rem = split(w_out_ref[sl, :])
                w_mix_ref[sl, :] = _dot(jnp.concatenate([a_head, a_head, a_rem], axis=1),
                                        jnp.concatenate([b_head, b_rem, b_head], axis=0)).astype(_BF16)

    @pl.when(jnp.logical_not(pooling_half))
    def _():
        w_mix_ref[...] = w_out_ref[...].astype(_BF16)


def _fold_pool(pool_w, pool_scale, w_out):
    assert w_out.shape[1] == 2 * POOL_WIDTH
    half = (None, POOL_WIDTH, w_out.shape[2])
    return pl.pallas_call(
        _fold_kernel,
        grid=(DEPTH, 2),
        in_specs=[pl.BlockSpec((None,) + pool_w.shape[1:], lambda l, k: (l, 0, 0, 0)),
                  pl.BlockSpec(pool_scale.shape, lambda l, k: (0, 0)),
                  pl.BlockSpec(half, lambda l, k: (l, k, 0))],
        out_specs=pl.BlockSpec(half, lambda l, k: (l, k, 0)),
        out_shape=jax.ShapeDtypeStruct(w_out.shape, _BF16),
        compiler_params=pltpu.CompilerParams(dimension_semantics=("arbitrary", "arbitrary")),
        name="fold_pool_maps",
    )(pool_w, pool_scale, w_out)


def _conv_out(w, layer, back2, back1, cur):
    tap = lambda k: _row(w.conv_w[k], layer)
    return _row(w.conv_b[...], layer) + (tap(0) * back2 + tap(1) * back1 + tap(2) * cur)


def _out_proj(h, ds, y_conv, w):
    return h + _dot(jnp.concatenate(ds + [y_conv.astype(_BF16)], axis=-1), w.w_mix[...])


def _mlp(h, w, layer, chunk):
    hg, scale = _norm_split(h, _row(w.norm_mlp[...], layer))
    acc = h
    for c in range(D_FF // chunk):
        cs = slice(c * chunk, (c + 1) * chunk)
        f = jnp.maximum(_dot(hg, w.w_up[:, cs]) * scale, 0.0)
        acc = acc + _dot((f * f).astype(_BF16), w.w_down[cs, :])
    return acc


def _gated_embed(h, p, w, layer):
    hg, scale = _norm_split(h, _row(w.norm_ple[...], layer))
    gate = jax.nn.sigmoid(_dot(hg, w.w_gate[...]) * scale)
    return h + gate * _dot(p.astype(_BF16), w.w_proj[layer].astype(_BF16))


def _prompt_kernel(*refs, layer, tile, n_tiles, n_cast, own_cast):
    n_big = len(_BIG_FIELDS)
    n_w = len(_Weights._fields) - (n_big if own_cast else 0)
    x_ref, p_ref = refs[:2]
    pos = 2 + n_w
    prev = refs[pos:pos + 2 * layer]
    pos += 2 * layer
    cast_in = refs[pos:pos + n_cast]
    pos += n_cast
    if own_cast:
        big_hbm = refs[pos:pos + n_big]
        pos += n_big
    out_ref, npool_ref, nconv_ref = refs[pos:pos + 3]
    cast_out = refs[pos + 3:pos + 3 + n_cast]
    pos += 3 + n_cast
    if own_cast:
        big_out = refs[pos:pos + n_big]
        pool_ext, conv_ext = refs[pos + n_big:pos + n_big + 2]
        big_vmem = refs[pos + n_big + 2:pos + 2 * n_big + 2]
        stage, fetch_sem, publish_sem = refs[pos + 2 * n_big + 2:]
        w = _Weights(*refs[2:2 + n_w], *big_vmem)
    else:
        pool_ext, conv_ext = refs[pos:]
        w = _Weights(*refs[2:2 + n_w])
    i = pl.program_id(1)
    halo = POOL_HALO
    chalo = SUBLANES

    if own_cast:
        first = (pl.program_id(0) == 0) & (i == 0)
        last = (pl.program_id(0) == pl.num_programs(0) - 1) & (i == n_tiles - 1)
        rows, cols = stage.shape[1:]
        chunks = [(m, r0, c0) for m in range(n_big)
                  for r0 in range(0, big_vmem[m].shape[0], rows)
                  for c0 in range(0, big_vmem[m].shape[1], cols)]

        def fetch(k):
            m, r0, c0 = chunks[k]
            return pltpu.make_async_copy(big_hbm[m].at[layer, pl.ds(r0, rows), pl.ds(c0, cols)],
                                         stage.at[k % 2], fetch_sem.at[k % 2])

        def publish(m):
            return pltpu.make_async_copy(big_vmem[m], big_out[m], publish_sem.at[m])

        @pl.when(first)
        def _():
            fetch(0).start()
            for k, (m, r0, c0) in enumerate(chunks):
                if k + 1 < len(chunks):
                    fetch(k + 1).start()
                fetch(k).wait()
                big_vmem[m][r0:r0 + rows, c0:c0 + cols] = stage[k % 2].astype(_BF16)
            for m in range(n_big):
                publish(m).start()

        @pl.when(last)
        def _():
            for m in range(n_big):
                publish(m).wait()

    @pl.when(i == 0)
    def _():
        pool_ext[0:halo, :] = jnp.zeros((halo, POOL_WIDTH), _F32)
        conv_ext[0:chalo, :] = jnp.zeros((chalo, CONV_WIDTH), _F32)

    h = x_ref[...]
    u, bg, cv = _in_proj(h, w, layer)
    pool_ext[halo:halo + tile, :] = u
    conv_ext[chalo:chalo + tile, :] = cv

    for src, dst in zip(cast_in, cast_out):
        dst[...] = src[...].astype(_BF16)

    position = i * tile + lax.broadcasted_iota(jnp.int32, (tile, 1), 0)
    ds = []
    for g, win in enumerate(POOL_WINDOWS):
        e = pool_ext[:, g * POOL_GC:(g + 1) * POOL_GC]
        s = e
        span = 1
        while span < win:
            s = s + pltpu.roll(s, span, axis=0)
            span *= 2
        cnt = jnp.minimum(win, position + 1).astype(_F32)
        ds.append((s[halo:, :] / cnt - e[halo:, :]).astype(_BF16))
    cvx = conv_ext[...]
    z = _conv_out(w, layer, pltpu.roll(cvx, 2, axis=0)[chalo:, :],
                  pltpu.roll(cvx, 1, axis=0)[chalo:, :], cvx[chalo:, :])

    h = _mlp(_out_proj(h, ds, bg * z, w), w, layer, PROMPT_FF_CHUNK)
    h = _gated_embed(h, p_ref[...], w, layer)
    out_ref[...] = _rmsnorm(h, w.norm_f[...]) if layer == DEPTH - 1 else h

    @pl.when(i == n_tiles - 1)
    def _():
        seq = pl.ds(pl.program_id(0), 1)
        new_pool = pool_ext[tile + halo - POOL_BUF:tile + halo, :][:, None, :]
        new_conv = conv_ext[tile + chalo - CONV_BUF:tile + chalo, :]
        if layer == 0:
            npool_ref[:, seq, :] = new_pool
            nconv_ref[...] = new_conv
        else:
            for k in range(layer):
                npool_ref[k] = prev[2 * k][...]
                nconv_ref[k] = prev[2 * k + 1][...]
            npool_ref[layer, :, seq, :] = new_pool
            nconv_ref[layer] = new_conv

    @pl.when(i < n_tiles - 1)
    def _():
        pool_ext[0:halo, :] = pool_ext[tile:tile + halo, :]
        conv_ext[0:chalo, :] = conv_ext[tile:tile + chalo, :]


def _sample_kernel(*refs, tile, n_tiles):
    n_big = len(_BIG_FIELDS)
    x_ref, p_ref, pool_hist, conv_hist = refs[:4]
    shared = refs[4:4 + _MIX_FIELD]
    w_mix_hbm = refs[4 + _MIX_FIELD]
    pos = 5 + _MIX_FIELD
    per_layer = [refs[pos + k * n_big:pos + (k + 1) * n_big] for k in range(DEPTH)]
    pos += DEPTH * n_big
    out_ref, npool_ref, nconv_ref = refs[pos:pos + 3]
    buffers = refs[pos + 3:pos + 4 + n_big]
    h_mid, pool_ext, conv_ext, sem = refs[pos + 4 + n_big:]
    w_in_buf, w_mix_buf, w_up_buf, w_down_buf, w_gate_buf = buffers
    w = _Weights(*shared, w_mix=w_mix_buf, w_in=w_in_buf, w_up=w_up_buf, w_down=w_down_buf,
                 w_gate=w_gate_buf)
    early, late = (0, 1), (2, 3, 4)

    def load(k, m):
        w_in_k, w_up_k, w_down_k, w_gate_k = per_layer[k]
        source = (w_in_k, w_mix_hbm.at[k], w_up_k, w_down_k, w_gate_k)[m]
        return pltpu.make_async_copy(source, buffers[m], sem.at[m])

    step = pl.program_id(0)
    layer = step // (2 * n_tiles)
    mlp_half = (step // n_tiles) % 2 == 1
    tile_index = step % n_tiles
    seqs = tile // SUBLANES
    halo = POOL_HALO * seqs
    chalo = CONV_BUF * seqs
    positions = range(SUBLANES)
    by_position = lambda ref: jnp.concatenate([ref[:, t, :] for t in positions], axis=0)

    for k in range(DEPTH):
        @pl.when(step == 2 * k * n_tiles)
        def _():
            if k == 0:
                for m in early + late:
                    load(0, m).start()
            else:
                for m in late:
                    load(k, m).start()
            for m in early:
                load(k, m).wait()

        @pl.when(step == (2 * k + 1) * n_tiles)
        def _():
            for m in late:
                load(k, m).wait()
            if k + 1 < DEPTH:
                for m in early:
                    load(k + 1, m).start()

    @pl.when(jnp.logical_not(mlp_half))
    def _():
        @pl.when(layer == 0)
        def _():
            h_mid[tile_index] = by_position(x_ref)

        h = h_mid[tile_index]
        u, bg, cv = _in_proj(h, w, layer)

        pool_ext[0:seqs, :] = pool_hist[0]
        pool_ext[seqs:halo, :] = pool_hist[...].reshape(POOL_BUF * seqs, POOL_WIDTH)
        pool_ext[halo:halo + tile, :] = u
        for k in range(CONV_BUF):
            conv_ext[k * seqs:(k + 1) * seqs, :] = conv_hist[:, k, :]
        conv_ext[chalo:chalo + tile, :] = cv

        ds = []
        for g, win in enumerate(POOL_WINDOWS):
            sl = slice(g * POOL_GC, (g + 1) * POOL_GC)
            s = pool_ext[halo - (win - 1) * seqs:halo + tile, sl]
            span = 1
            while span < win:
                s = s[span * seqs:, :] + s[:-span * seqs, :]
                span *= 2
            ds.append((s * (1.0 / win) - u[:, sl]).astype(_BF16))
        z = _conv_out(w, layer, conv_ext[0:tile, :], conv_ext[seqs:seqs + tile, :], cv)

        h_mid[tile_index] = _out_proj(h, ds, bg * z, w)
        npool_ref[...] = pool_ext[halo + tile - POOL_BUF * seqs:halo + tile, :].reshape(
            POOL_BUF, seqs, POOL_WIDTH)
        for k in range(CONV_BUF):
            nconv_ref[:, k, :] = conv_ext[tile + k * seqs:tile + (k + 1) * seqs, :]

    @pl.when(mlp_half)
    def _():
        h = _mlp(h_mid[tile_index], w, layer, SAMPLE_FF_CHUNK)
        h = _gated_embed(h, by_position(p_ref), w, layer)

        @pl.when(layer < DEPTH - 1)
        def _():
            h_mid[tile_index] = h

        @pl.when(layer == DEPTH - 1)
        def _():
            y = _rmsnorm(h, w.norm_f[...])
            for t in positions:
                out_ref[:, t, :] = y[t * seqs:(t + 1) * seqs, :]


def _weight_specs(weights, layer):
    once = pl.Buffered(1)
    specs = [pl.BlockSpec(a.shape, lambda *_, nd=a.ndim: (0,) * nd, pipeline_mode=once)
             for a in weights]
    specs[_MIX_FIELD] = pl.BlockSpec((None,) + weights[_MIX_FIELD].shape[1:],
                                     lambda *_: (layer, 0, 0), pipeline_mode=once)
    return specs


def _prompt_layer(layer, x, p, weights, prev_state, cast_next, cast_own=()):
    batch, seq, _ = x.shape
    tile = TOKEN_TILE
    n_tiles = seq // tile
    steps = batch * n_tiles
    assert seq == n_tiles * tile and tile >= POOL_HALO and len(prev_state) == 2 * layer
    for a in cast_next:
        assert a.shape[1] % (steps * BF16_ROWS) == 0
    for a in cast_own:
        assert a.shape[1] % CAST_STAGE[0] == 0 and a.shape[2] % CAST_STAGE[1] == 0
    assert len(weights) + len(cast_own) == len(_Weights._fields)
    hbm = pl.BlockSpec(memory_space=pl.ANY)
    whole = lambda shape: pl.BlockSpec(shape, lambda b, i, nd=len(shape): (0,) * nd)
    per_batch = pl.BlockSpec((None, CONV_BUF, CONV_WIDTH), lambda b, i: (b, 0, 0))
    pool_shape = (POOL_BUF, batch, POOL_WIDTH)
    in_specs = [
        pl.BlockSpec((None, tile, D_MODEL), lambda b, i: (b, i, 0)),
        pl.BlockSpec((None, None, tile, PLE_DIM), lambda b, i: (layer, b, i, 0)),
    ] + _weight_specs(weights, layer) + [whole(pool_shape), per_batch] * layer + [
        pl.BlockSpec((None, a.shape[1] // steps, a.shape[2]),
                     lambda b, i: (layer + 1, b * n_tiles + i, 0)) for a in cast_next] + [
        hbm] * len(cast_own)
    out_specs = [pl.BlockSpec((None, tile, D_MODEL), lambda b, i: (b, i, 0))]
    out_shape = [jax.ShapeDtypeStruct((batch, seq, D_MODEL), _F32)]
    if layer == 0:
        out_specs += [whole(pool_shape), per_batch]
        out_shape += [jax.ShapeDtypeStruct(pool_shape, _F32),
                      jax.ShapeDtypeStruct((batch, CONV_BUF, CONV_WIDTH), _F32)]
    else:
        out_specs += [whole((layer + 1,) + pool_shape),
                      pl.BlockSpec((layer + 1, None, CONV_BUF, CONV_WIDTH), lambda b, i: (0, b, 0, 0))]
        out_shape += [jax.ShapeDtypeStruct((layer + 1,) + pool_shape, _F32),
                      jax.ShapeDtypeStruct((layer + 1, batch, CONV_BUF, CONV_WIDTH), _F32)]
    out_specs += [pl.BlockSpec((a.shape[1] // steps, a.shape[2]), lambda b, i: (b * n_tiles + i, 0))
                  for a in cast_next]
    out_shape += [jax.ShapeDtypeStruct(a.shape[1:], _BF16) for a in cast_next]
    scratch_shapes = [pltpu.VMEM((POOL_HALO + tile, POOL_WIDTH), _F32),
                      pltpu.VMEM((SUBLANES + tile, CONV_WIDTH), _F32)]
    if cast_own:
        out_specs += [hbm] * len(cast_own)
        out_shape += [jax.ShapeDtypeStruct(a.shape[1:], _BF16) for a in cast_own]
        scratch_shapes += [pltpu.VMEM(a.shape[1:], _BF16) for a in cast_own] + [
            pltpu.VMEM((2,) + CAST_STAGE, _F32),
            pltpu.SemaphoreType.DMA((2,)), pltpu.SemaphoreType.DMA((len(cast_own),))]
    return pl.pallas_call(
        functools.partial(_prompt_kernel, layer=layer, tile=tile, n_tiles=n_tiles,
                          n_cast=len(cast_next), own_cast=bool(cast_own)),
        grid=(batch, n_tiles),
        in_specs=in_specs,
        out_specs=out_specs,
        out_shape=out_shape,
        scratch_shapes=scratch_shapes,
        compiler_params=pltpu.CompilerParams(
            dimension_semantics=("arbitrary", "arbitrary"),
            vmem_limit_bytes=VMEM_LIMIT_BYTES),
        name=f"prompt_layer{layer}",
    )(x, p, *weights, *prev_state, *cast_next, *cast_own)


def _sample_trunk(x, p, state_pool, state_conv, shared, w_mix, per_layer):
    n_seq = x.shape[0]
    seqs = TOKEN_TILE // SUBLANES
    tile = seqs * SUBLANES
    n_tiles = n_seq // seqs
    assert n_seq == n_tiles * seqs and x.shape[1] == SUBLANES and len(shared) == _MIX_FIELD
    assert len(per_layer) == DEPTH and all(len(big) == len(_BIG_FIELDS) for big in per_layer)
    last = n_tiles - 1

    def at(mixing, mlp):
        def index(s):
            layer, tile_index = s // (2 * n_tiles), s % n_tiles
            in_mlp_half = (s // n_tiles) % 2 == 1
            a, b = mixing(layer, tile_index), mlp(layer, tile_index)
            return tuple(jnp.where(in_mlp_half, j, i) for i, j in zip(a, b))
        return index

    hbm = pl.BlockSpec(memory_space=pl.ANY)
    big = [a for layer_big in per_layer for a in layer_big]
    state_rows = at(lambda l, i: (l, 0, i, 0), lambda l, i: (l, 0, last, 0))
    conv_rows = at(lambda l, i: (l, i, 0, 0), lambda l, i: (l, last, 0, 0))
    in_specs = [
        pl.BlockSpec((seqs, SUBLANES, D_MODEL), lambda s: (jnp.minimum(s, last), 0, 0),
                     pipeline_mode=pl.Buffered(1)),
        pl.BlockSpec((None, seqs, SUBLANES, PLE_DIM),
                     at(lambda l, i: (l, 0, 0, 0), lambda l, i: (l, i, 0, 0))),
        pl.BlockSpec((None, POOL_BUF, seqs, POOL_WIDTH), state_rows),
        pl.BlockSpec((None, seqs, CONV_BUF, CONV_WIDTH), conv_rows),
    ] + [pl.BlockSpec(a.shape, lambda s, nd=a.ndim: (0,) * nd, pipeline_mode=pl.Buffered(1))
         for a in shared] + [hbm] * (1 + len(big))
    out_specs = [
        pl.BlockSpec((seqs, SUBLANES, D_MODEL),
                     at(lambda l, i: (0, 0, 0),
                        lambda l, i: (jnp.where(l == DEPTH - 1, i, 0), 0, 0))),
        pl.BlockSpec((None, POOL_BUF, seqs, POOL_WIDTH), state_rows),
        pl.BlockSpec((None, seqs, CONV_BUF, CONV_WIDTH), conv_rows),
    ]
    out_shape = [jax.ShapeDtypeStruct((n_seq, SUBLANES, D_MODEL), _F32),
                 jax.ShapeDtypeStruct((DEPTH, POOL_BUF, n_seq, POOL_WIDTH), _F32),
                 jax.ShapeDtypeStruct((DEPTH, n_seq, CONV_BUF, CONV_WIDTH), _F32)]
    w_in, w_up, w_down, w_gate = per_layer[0]
    buffers = [w_in.shape, w_mix.shape[1:], w_up.shape, w_down.shape, w_gate.shape]
    return pl.pallas_call(
        functools.partial(_sample_kernel, tile=tile, n_tiles=n_tiles),
        grid=(2 * DEPTH * n_tiles,),
        in_specs=in_specs,
        out_specs=out_specs,
        out_shape=out_shape,
        scratch_shapes=[pltpu.VMEM(shape, _BF16) for shape in buffers] + [
            pltpu.VMEM((n_tiles, tile, D_MODEL), _F32),
            pltpu.VMEM(((POOL_HALO + SUBLANES) * seqs, POOL_WIDTH), _F32),
            pltpu.VMEM(((CONV_BUF + SUBLANES) * seqs, CONV_WIDTH), _F32),
            pltpu.SemaphoreType.DMA((len(buffers),))],
        compiler_params=pltpu.CompilerParams(
            dimension_semantics=("arbitrary",),
            vmem_limit_bytes=VMEM_LIMIT_BYTES),
        name="sample_trunk",
    )(x, p, state_pool, state_conv, *shared, w_mix, *big)


def kernel(x_prompt, x_sample, state_pool, state_conv, p_prompt, p_sample, norm_mix, w_in, pool_w, pool_scale, conv_w, conv_b, w_out, norm_mlp, w_up, w_down, norm_ple, w_ple_gate, w_ple_proj, norm_f):
    assert DEPTH == 2 and PAST_LEN >= POOL_BUF
    shared = (norm_mix, jnp.swapaxes(conv_w, 0, 1), conv_b, norm_mlp, norm_ple,
              norm_f.reshape(1, D_MODEL), w_ple_proj)
    big_f32 = (w_in, w_up, w_down, w_ple_gate)

    w_mix = _fold_pool(pool_w, pool_scale, w_out)

    n_big = len(big_f32)
    h, pool_p, conv_p, *big_bf16 = _prompt_layer(0, x_prompt, p_prompt, shared + (w_mix,), (),
                                                 big_f32, big_f32)
    per_layer = [big_bf16[n_big:], big_bf16[:n_big]]
    y_prompt, pool_p, conv_p = _prompt_layer(1, h, p_prompt, _Weights(*shared, w_mix, *per_layer[1]),
                                             (pool_p, conv_p), ())

    y_sample, pool_s, conv_s = _sample_trunk(x_sample, p_sample, jnp.swapaxes(state_pool, 1, 2),
                                             state_conv, shared, w_mix, per_layer)
    return (y_prompt, y_sample, jnp.swapaxes(pool_p, 1, 2), conv_p, jnp.swapaxes(pool_s, 1, 2), conv_s)
```

```python
import functools
from typing import NamedTuple

import jax
import jax.numpy as jnp
from jax import lax
from jax.experimental import pallas as pl
from jax.experimental.pallas import tpu as pltpu

D_MODEL = 1024
DEPTH = 2
PAST_LEN = 16384
POOL_WIDTH = 512
CONV_WIDTH = 512
POOL_WINDOWS = (2, 4, 8, 16)
POOL_GC = 128
POOL_BUF = 15
CONV_BUF = 2
D_FF = 4 * D_MODEL
PLE_DIM = 256
IN_COLS = POOL_WIDTH + 3 * CONV_WIDTH
EPS = 1e-6

SUBLANES = 8
BF16_ROWS = 16
POOL_HALO = POOL_BUF + 1
TOKEN_TILE = 512
PROMPT_FF_CHUNK = 1024
SAMPLE_FF_CHUNK = 512
V7X_VMEM_BYTES = 64 * 1024 * 1024
VMEM_LIMIT_BYTES = V7X_VMEM_BYTES * 29 // 32

_BF16 = jnp.bfloat16
_F32 = jnp.float32


class _Weights(NamedTuple):
    norm_mix: jax.Array
    conv_w: jax.Array
    conv_b: jax.Array
    norm_mlp: jax.Array
    norm_ple: jax.Array
    norm_f: jax.Array
    w_proj: jax.Array
    w_mix: jax.Array
    w_in: jax.Array
    w_up: jax.Array
    w_down: jax.Array
    w_gate: jax.Array


_BIG_FIELDS = _Weights._fields[-4:]
_MIX_FIELD = _Weights._fields.index("w_mix")
CAST_STAGE = (512, 1024)


def _rmsnorm(x, g):
    ms = jnp.mean(x * x, axis=-1, keepdims=True)
    return (x * lax.rsqrt(ms + EPS)) * g


def _dot(a, b):
    return jnp.dot(a, b, preferred_element_type=_F32)


def _norm_split(x, g):
    scale = lax.rsqrt(jnp.mean(x * x, axis=-1, keepdims=True) + EPS)
    return (x * g).astype(_BF16), scale


def _row(vec, layer):
    if isinstance(layer, int):
        return vec[layer:layer + 1, :]
    row = vec[0:1, :]
    for k in range(1, DEPTH):
        row = jnp.where(layer == k, vec[k:k + 1, :], row)
    return row


def _in_proj(h, w, layer):
    hg, scale = _norm_split(h, _row(w.norm_mix[...], layer))
    proj = _dot(hg, w.w_in[...])
    u = proj[:, :POOL_WIDTH] * scale
    bg = proj[:, POOL_WIDTH:POOL_WIDTH + CONV_WIDTH] * scale
    cg = proj[:, POOL_WIDTH + CONV_WIDTH:POOL_WIDTH + 2 * CONV_WIDTH] * scale
    v = proj[:, POOL_WIDTH + 2 * CONV_WIDTH:] * scale
    return u, bg, cg * v


def _fold_kernel(pool_w_ref, pool_scale_ref, w_out_ref, w_mix_ref):
    def split(a):
        head = a.astype(_BF16)
        return head, (a - head.astype(_F32)).astype(_BF16)

    pooling_half = pl.program_id(1) == 0
    for layer in range(DEPTH):
        @pl.when(pooling_half & (pl.program_id(0) == layer))
        def _():
            for g in range(len(POOL_WINDOWS)):
                sl = slice(g * POOL_GC, (g + 1) * POOL_GC)
                a_head, a_rem = split(pool_w_ref[g] * pool_scale_ref[layer:layer + 1, sl])
                b_head, b_rem = split(w_out_ref[sl, :])
                w_mix_ref[sl, :] = _dot(jnp.concatenate([a_head, a_head, a_rem], axis=1),
                                        jnp.concatenate([b_head, b_rem, b_head], axis=0)).astype(_BF16)

    @pl.when(jnp.logical_not(pooling_half))
    def _():
        w_mix_ref[...] = w_out_ref[...].astype(_BF16)


def _fold_pool(pool_w, pool_scale, w_out):
    assert w_out.shape[1] == 2 * POOL_WIDTH
    half = (None, POOL_WIDTH, w_out.shape[2])
    return pl.pallas_call(
        _fold_kernel,
        grid=(DEPTH, 2),
        in_specs=[pl.BlockSpec((None,) + pool_w.shape[1:], lambda l, k: (l, 0, 0, 0)),
                  pl.BlockSpec(pool_scale.shape, lambda l, k: (0, 0)),
                  pl.BlockSpec(half, lambda l, k: (l, k, 0))],
        out_specs=pl.BlockSpec(half, lambda l, k: (l, k, 0)),
        out_shape=jax.ShapeDtypeStruct(w_out.shape, _BF16),
        compiler_params=pltpu.CompilerParams(dimension_semantics=("arbitrary", "arbitrary")),
        name="fold_pool_maps",
    )(pool_w, pool_scale, w_out)


def _conv_out(w, layer, back2, back1, cur):
    tap = lambda k: _row(w.conv_w[k], layer)
    return _row(w.conv_b[...], layer) + (tap(0) * back2 + tap(1) * back1 + tap(2) * cur)


def _out_proj(h, ds, y_conv, w):
    return h + _dot(jnp.concatenate(ds + [y_conv.astype(_BF16)], axis=-1), w.w_mix[...])


def _mlp(h, w, layer, chunk):
    hg, scale = _norm_split(h, _row(w.norm_mlp[...], layer))
    acc = h
    for c in range(D_FF // chunk):
        cs = slice(c * chunk, (c + 1) * chunk)
        f = jnp.maximum(_dot(hg, w.w_up[:, cs]) * scale, 0.0)
        acc = acc + _dot((f * f).astype(_BF16), w.w_down[cs, :])
    return acc


def _gated_embed(h, p, w, layer):
    hg, scale = _norm_split(h, _row(w.norm_ple[...], layer))
    gate = jax.nn.sigmoid(_dot(hg, w.w_gate[...]) * scale)
    return h + gate * _dot(p.astype(_BF16), w.w_proj[layer].astype(_BF16))


def _prompt_kernel(*refs, layer, tile, n_tiles, n_cast, own_cast):
    n_big = len(_BIG_FIELDS)
    n_w = len(_Weights._fields) - (n_big if own_cast else 0)
    x_ref, p_ref = refs[:2]
    pos = 2 + n_w
    prev = refs[pos:pos + 2 * layer]
    pos += 2 * layer
    cast_in = refs[pos:pos + n_cast]
    pos += n_cast
    if own_cast:
        big_hbm = refs[pos:pos + n_big]
        pos += n_big
    out_ref, npool_ref, nconv_ref = refs[pos:pos + 3]
    cast_out = refs[pos + 3:pos + 3 + n_cast]
    pos += 3 + n_cast
    if own_cast:
        big_out = refs[pos:pos + n_big]
        pool_ext, conv_ext = refs[pos + n_big:pos + n_big + 2]
        big_vmem = refs[pos + n_big + 2:pos + 2 * n_big + 2]
        stage, fetch_sem, publish_sem = refs[pos + 2 * n_big + 2:]
        w = _Weights(*refs[2:2 + n_w], *big_vmem)
    else:
        pool_ext, conv_ext = refs[pos:]
        w = _Weights(*refs[2:2 + n_w])
    i = pl.program_id(1)
    halo = POOL_HALO
    chalo = SUBLANES

    if own_cast:
        first = (pl.program_id(0) == 0) & (i == 0)
        last = (pl.program_id(0) == pl.num_programs(0) - 1) & (i == n_tiles - 1)
        rows, cols = stage.shape[1:]
        chunks = [(m, r0, c0) for m in range(n_big)
                  for r0 in range(0, big_vmem[m].shape[0], rows)
                  for c0 in range(0, big_vmem[m].shape[1], cols)]

        def fetch(k):
            m, r0, c0 = chunks[k]
            return pltpu.make_async_copy(big_hbm[m].at[layer, pl.ds(r0, rows), pl.ds(c0, cols)],
                                         stage.at[k % 2], fetch_sem.at[k % 2])

        def publish(m):
            return pltpu.make_async_copy(big_vmem[m], big_out[m], publish_sem.at[m])

        @pl.when(first)
        def _():
            fetch(0).start()
            for k, (m, r0, c0) in enumerate(chunks):
                if k + 1 < len(chunks):
                    fetch(k + 1).start()
                fetch(k).wait()
                big_vmem[m][r0:r0 + rows, c0:c0 + cols] = stage[k % 2].astype(_BF16)
            for m in range(n_big):
                publish(m).start()

        @pl.when(last)
        def _():
            for m in range(n_big):
                publish(m).wait()

    @pl.when(i == 0)
    def _():
        pool_ext[0:halo, :] = jnp.zeros((halo, POOL_WIDTH), _F32)
        conv_ext[0:chalo, :] = jnp.zeros((chalo, CONV_WIDTH), _F32)

    h = x_ref[...]
    u, bg, cv = _in_proj(h, w, layer)
    pool_ext[halo:halo + tile, :] = u
    conv_ext[chalo:chalo + tile, :] = cv

    for src, dst in zip(cast_in, cast_out):
        dst[...] = src[...].astype(_BF16)

    position = i * tile + lax.broadcasted_iota(jnp.int32, (tile, 1), 0)
    ds = []
    for g, win in enumerate(POOL_WINDOWS):
        e = pool_ext[:, g * POOL_GC:(g + 1) * POOL_GC]
        s = e
        span = 1
        while span < win:
            s = s + pltpu.roll(s, span, axis=0)
            span *= 2
        cnt = jnp.minimum(win, position + 1).astype(_F32)
        ds.append((s[halo:, :] / cnt - e[halo:, :]).astype(_BF16))
    cvx = conv_ext[...]
    z = _conv_out(w, layer, pltpu.roll(cvx, 2, axis=0)[chalo:, :],
                  pltpu.roll(cvx, 1, axis=0)[chalo:, :], cvx[chalo:, :])

    h = _mlp(_out_proj(h, ds, bg * z, w), w, layer, PROMPT_FF_CHUNK)
    h = _gated_embed(h, p_ref[...], w, layer)
    out_ref[...] = _rmsnorm(h, w.norm_f[...]) if layer == DEPTH - 1 else h

    @pl.when(i == n_tiles - 1)
    def _():
        seq = pl.ds(pl.program_id(0), 1)
        new_pool = pool_ext[tile + halo - POOL_BUF:tile + halo, :][:, None, :]
        new_conv = conv_ext[tile + chalo - CONV_BUF:tile + chalo, :]
        if layer == 0:
            npool_ref[:, seq, :] = new_pool
            nconv_ref[...] = new_conv
        else:
            for k in range(layer):
                npool_ref[k] = prev[2 * k][...]
                nconv_ref[k] = prev[2 * k + 1][...]
            npool_ref[layer, :, seq, :] = new_pool
            nconv_ref[layer] = new_conv

    @pl.when(i < n_tiles - 1)
    def _():
        pool_ext[0:halo, :] = pool_ext[tile:tile + halo, :]
        conv_ext[0:chalo, :] = conv_ext[tile:tile + chalo, :]


def _sample_kernel(*refs, tile, n_tiles):
    n_big = len(_BIG_FIELDS)
    x_ref, p_ref, pool_hist, conv_hist = refs[:4]
    shared = refs[4:4 + _MIX_FIELD]
    w_mix_hbm = refs[4 + _MIX_FIELD]
    pos = 5 + _MIX_FIELD
    per_layer = [refs[pos + k * n_big:pos + (k + 1) * n_big] for k in range(DEPTH)]
    pos += DEPTH * n_big
    out_ref, npool_ref, nconv_ref = refs[pos:pos + 3]
    buffers = refs[pos + 3:pos + 4 + n_big]
    h_mid, pool_ext, conv_ext, sem = refs[pos + 4 + n_big:]
    w_in_buf, w_mix_buf, w_up_buf, w_down_buf, w_gate_buf = buffers
    w = _Weights(*shared, w_mix=w_mix_buf, w_in=w_in_buf, w_up=w_up_buf, w_down=w_down_buf,
                 w_gate=w_gate_buf)
    early, late = (0, 1), (2, 3, 4)

    def load(k, m):
        w_in_k, w_up_k, w_down_k, w_gate_k = per_layer[k]
        source = (w_in_k, w_mix_hbm.at[k], w_up_k, w_down_k, w_gate_k)[m]
        return pltpu.make_async_copy(source, buffers[m], sem.at[m])

    step = pl.program_id(0)
    layer = step // (2 * n_tiles)
    mlp_half = (step // n_tiles) % 2 == 1
    tile_index = step % n_tiles
    seqs = tile // SUBLANES
    halo = POOL_HALO * seqs
    chalo = CONV_BUF * seqs
    positions = range(SUBLANES)
    by_position = lambda ref: jnp.concatenate([ref[:, t, :] for t in positions], axis=0)

    for k in range(DEPTH):
        @pl.when(step == 2 * k * n_tiles)
        def _():
            if k == 0:
                for m in early + late:
                    load(0, m).start()
            else:
                for m in late:
                    load(k, m).start()
            for m in early:
                load(k, m).wait()

        @pl.when(step == (2 * k + 1) * n_tiles)
        def _():
            for m in late:
                load(k, m).wait()
            if k + 1 < DEPTH:
                for m in early:
                    load(k + 1, m).start()

    @pl.when(jnp.logical_not(mlp_half))
    def _():
        @pl.when(layer == 0)
        def _():
            h_mid[tile_index] = by_position(x_ref)

        h = h_mid[tile_index]
        u, bg, cv = _in_proj(h, w, layer)

        pool_ext[0:seqs, :] = pool_hist[0]
        pool_ext[seqs:halo, :] = pool_hist[...].reshape(POOL_BUF * seqs, POOL_WIDTH)
        pool_ext[halo:halo + tile, :] = u
        for k in range(CONV_BUF):
            conv_ext[k * seqs:(k + 1) * seqs, :] = conv_hist[:, k, :]
        conv_ext[chalo:chalo + tile, :] = cv

        ds = []
        for g, win in enumerate(POOL_WINDOWS):
            sl = slice(g * POOL_GC, (g + 1) * POOL_GC)
            s = pool_ext[halo - (win - 1) * seqs:halo + tile, sl]
            span = 1
            while span < win:
                s = s[span * seqs:, :] + s[:-span * seqs, :]
                span *= 2
            ds.append((s * (1.0 / win) - u[:, sl]).astype(_BF16))
        z = _conv_out(w, layer, conv_ext[0:tile, :], conv_ext[seqs:seqs + tile, :], cv)

        h_mid[tile_index] = _out_proj(h, ds, bg * z, w)
        npool_ref[...] = pool_ext[halo + tile - POOL_BUF * seqs:halo + tile, :].reshape(
            POOL_BUF, seqs, POOL_WIDTH)
        for k in range(CONV_BUF):
            nconv_ref[:, k, :] = conv_ext[tile + k * seqs:tile + (k + 1) * seqs, :]

    @pl.when(mlp_half)
    def _():
        h = _mlp(h_mid[tile_index], w, layer, SAMPLE_FF_CHUNK)
        h = _gated_embed(h, by_position(p_ref), w, layer)

        @pl.when(layer < DEPTH - 1)
        def _():
            h_mid[tile_index] = h

        @pl.when(layer == DEPTH - 1)
        def _():
            y = _rmsnorm(h, w.norm_f[...])
            for t in positions:
                out_ref[:, t, :] = y[t * seqs:(t + 1) * seqs, :]


def _weight_specs(weights, layer):
    once = pl.Buffered(1)
    specs = [pl.BlockSpec(a.shape, lambda *_, nd=a.ndim: (0,) * nd, pipeline_mode=once)
             for a in weights]
    specs[_MIX_FIELD] = pl.BlockSpec((None,) + weights[_MIX_FIELD].shape[1:],
                                     lambda *_: (layer, 0, 0), pipeline_mode=once)
    return specs


def _prompt_layer(layer, x, p, weights, prev_state, cast_next, cast_own=()):
    batch, seq, _ = x.shape
    tile = TOKEN_TILE
    n_tiles = seq // tile
    steps = batch * n_tiles
    assert seq == n_tiles * tile and tile >= POOL_HALO and len(prev_state) == 2 * layer
    for a in cast_next:
        assert a.shape[1] % (steps * BF16_ROWS) == 0
    for a in cast_own:
        assert a.shape[1] % CAST_STAGE[0] == 0 and a.shape[2] % CAST_STAGE[1] == 0
    assert len(weights) + len(cast_own) == len(_Weights._fields)
    hbm = pl.BlockSpec(memory_space=pl.ANY)
    whole = lambda shape: pl.BlockSpec(shape, lambda b, i, nd=len(shape): (0,) * nd)
    per_batch = pl.BlockSpec((None, CONV_BUF, CONV_WIDTH), lambda b, i: (b, 0, 0))
    pool_shape = (POOL_BUF, batch, POOL_WIDTH)
    in_specs = [
        pl.BlockSpec((None, tile, D_MODEL), lambda b, i: (b, i, 0)),
        pl.BlockSpec((None, None, tile, PLE_DIM), lambda b, i: (layer, b, i, 0)),
    ] + _weight_specs(weights, layer) + [whole(pool_shape), per_batch] * layer + [
        pl.BlockSpec((None, a.shape[1] // steps, a.shape[2]),
                     lambda b, i: (layer + 1, b * n_tiles + i, 0)) for a in cast_next] + [
        hbm] * len(cast_own)
    out_specs = [pl.BlockSpec((None, tile, D_MODEL), lambda b, i: (b, i, 0))]
    out_shape = [jax.ShapeDtypeStruct((batch, seq, D_MODEL), _F32)]
    if layer == 0:
        out_specs += [whole(pool_shape), per_batch]
        out_shape += [jax.ShapeDtypeStruct(pool_shape, _F32),
                      jax.ShapeDtypeStruct((batch, CONV_BUF, CONV_WIDTH), _F32)]
    else:
        out_specs += [whole((layer + 1,) + pool_shape),
                      pl.BlockSpec((layer + 1, None, CONV_BUF, CONV_WIDTH), lambda b, i: (0, b, 0, 0))]
        out_shape += [jax.ShapeDtypeStruct((layer + 1,) + pool_shape, _F32),
                      jax.ShapeDtypeStruct((layer + 1, batch, CONV_BUF, CONV_WIDTH), _F32)]
    out_specs += [pl.BlockSpec((a.shape[1] // steps, a.shape[2]), lambda b, i: (b * n_tiles + i, 0))
                  for a in cast_next]
    out_shape += [jax.ShapeDtypeStruct(a.shape[1:], _BF16) for a in cast_next]
    scratch_shapes = [pltpu.VMEM((POOL_HALO + tile, POOL_WIDTH), _F32),
                      pltpu.VMEM((SUBLANES + tile, CONV_WIDTH), _F32)]
    if cast_own:
        out_specs += [hbm] * len(cast_own)
        out_shape += [jax.ShapeDtypeStruct(a.shape[1:], _BF16) for a in cast_own]
        scratch_shapes += [pltpu.VMEM(a.shape[1:], _BF16) for a in cast_own] + [
            pltpu.VMEM((2,) + CAST_STAGE, _F32),
            pltpu.SemaphoreType.DMA((2,)), pltpu.SemaphoreType.DMA((len(cast_own),))]
    return pl.pallas_call(
        functools.partial(_prompt_kernel, layer=layer, tile=tile, n_tiles=n_tiles,
                          n_cast=len(cast_next), own_cast=bool(cast_own)),
        grid=(batch, n_tiles),
        in_specs=in_specs,
        out_specs=out_specs,
        out_shape=out_shape,
        scratch_shapes=scratch_shapes,
        compiler_params=pltpu.CompilerParams(
            dimension_semantics=("arbitrary", "arbitrary"),
            vmem_limit_bytes=VMEM_LIMIT_BYTES),
        name=f"prompt_layer{layer}",
    )(x, p, *weights, *prev_state, *cast_next, *cast_own)


def _sample_trunk(x, p, state_pool, state_conv, shared, w_mix, per_layer):
    n_seq = x.shape[0]
    seqs = TOKEN_TILE // SUBLANES
    tile = seqs * SUBLANES
    n_tiles = n_seq // seqs
    assert n_seq == n_tiles * seqs and x.shape[1] == SUBLANES and len(shared) == _MIX_FIELD
    assert len(per_layer) == DEPTH and all(len(big) == len(_BIG_FIELDS) for big in per_layer)
    last = n_tiles - 1

    def at(mixing, mlp):
        def index(s):
            layer, tile_index = s // (2 * n_tiles), s % n_tiles
            in_mlp_half = (s // n_tiles) % 2 == 1
            a, b = mixing(layer, tile_index), mlp(layer, tile_index)
            return tuple(jnp.where(in_mlp_half, j, i) for i, j in zip(a, b))
        return index

    hbm = pl.BlockSpec(memory_space=pl.ANY)
    big = [a for layer_big in per_layer for a in layer_big]
    state_rows = at(lambda l, i: (l, 0, i, 0), lambda l, i: (l, 0, last, 0))
    conv_rows = at(lambda l, i: (l, i, 0, 0), lambda l, i: (l, last, 0, 0))
    in_specs = [
        pl.BlockSpec((seqs, SUBLANES, D_MODEL), lambda s: (jnp.minimum(s, last), 0, 0)),
        pl.BlockSpec((None, seqs, SUBLANES, PLE_DIM),
                     at(lambda l, i: (l, 0, 0, 0), lambda l, i: (l, i, 0, 0))),
        pl.BlockSpec((None, POOL_BUF, seqs, POOL_WIDTH), state_rows),
        pl.BlockSpec((None, seqs, CONV_BUF, CONV_WIDTH), conv_rows),
    ] + [pl.BlockSpec(a.shape, lambda s, nd=a.ndim: (0,) * nd, pipeline_mode=pl.Buffered(1))
         for a in shared] + [hbm] * (1 + len(big))
    out_specs = [
        pl.BlockSpec((seqs, SUBLANES, D_MODEL),
                     at(lambda l, i: (0, 0, 0),
                        lambda l, i: (jnp.where(l == DEPTH - 1, i, 0), 0, 0))),
        pl.BlockSpec((None, POOL_BUF, seqs, POOL_WIDTH), state_rows),
        pl.BlockSpec((None, seqs, CONV_BUF, CONV_WIDTH), conv_rows),
    ]
    out_shape = [jax.ShapeDtypeStruct((n_seq, SUBLANES, D_MODEL), _F32),
                 jax.ShapeDtypeStruct((DEPTH, POOL_BUF, n_seq, POOL_WIDTH), _F32),
                 jax.ShapeDtypeStruct((DEPTH, n_seq, CONV_BUF, CONV_WIDTH), _F32)]
    w_in, w_up, w_down, w_gate = per_layer[0]
    buffers = [w_in.shape, w_mix.shape[1:], w_up.shape, w_down.shape, w_gate.shape]
    return pl.pallas_call(
        functools.partial(_sample_kernel, tile=tile, n_tiles=n_tiles),
        grid=(2 * DEPTH * n_tiles,),
        in_specs=in_specs,
        out_specs=out_specs,
        out_shape=out_shape,
        scratch_shapes=[pltpu.VMEM(shape, _BF16) for shape in buffers] + [
            pltpu.VMEM((n_tiles, tile, D_MODEL), _F32),
            pltpu.VMEM(((POOL_HALO + SUBLANES) * seqs, POOL_WIDTH), _F32),
            pltpu.VMEM(((CONV_BUF + SUBLANES) * seqs, CONV_WIDTH), _F32),
            pltpu.SemaphoreType.DMA((len(buffers),))],
        compiler_params=pltpu.CompilerParams(
            dimension_semantics=("arbitrary",),
            vmem_limit_bytes=VMEM_LIMIT_BYTES),
        name="sample_trunk",
    )(x, p, state_pool, state_conv, *shared, w_mix, *big)


def kernel(x_prompt, x_sample, state_pool, state_conv, p_prompt, p_sample, norm_mix, w_in, pool_w, pool_scale, conv_w, conv_b, w_out, norm_mlp, w_up, w_down, norm_ple, w_ple_gate, w_ple_proj, norm_f):
    assert DEPTH == 2 and PAST_LEN >= POOL_BUF
    shared = (norm_mix, jnp.swapaxes(conv_w, 0, 1), conv_b, norm_mlp, norm_ple,
              norm_f.reshape(1, D_MODEL), w_ple_proj)
    big_f32 = (w_in, w_up, w_down, w_ple_gate)

    w_mix = _fold_pool(pool_w, pool_scale, w_out)

    n_big = len(big_f32)
    h, pool_p, conv_p, *big_bf16 = _prompt_layer(0, x_prompt, p_prompt, shared + (w_mix,), (),
                                                 big_f32, big_f32)
    per_layer = [big_bf16[n_big:], big_bf16[:n_big]]
    y_prompt, pool_p, conv_p = _prompt_layer(1, h, p_prompt, _Weights(*shared, w_mix, *per_layer[1]),
                                             (pool_p, conv_p), ())

    y_sample, pool_s, conv_s = _sample_trunk(x_sample, p_sample, jnp.swapaxes(state_pool, 1, 2),
                                             state_conv, shared, w_mix, per_layer)
    return (y_prompt, y_sample, jnp.swapaxes(pool_p, 1, 2), conv_p, jnp.swapaxes(pool_s, 1, 2), conv_s)
```

```python
import functools
from typing import NamedTuple

import jax
import jax.numpy as jnp
from jax import lax
from jax.experimental import pallas as pl
from jax.experimental.pallas import tpu as pltpu

D_MODEL = 1024
DEPTH = 2
PAST_LEN = 16384
POOL_WIDTH = 512
CONV_WIDTH = 512
POOL_WINDOWS = (2, 4, 8, 16)
POOL_GC = 128
POOL_BUF = 15
CONV_BUF = 2
D_FF = 4 * D_MODEL
PLE_DIM = 256
IN_COLS = POOL_WIDTH + 3 * CONV_WIDTH
EPS = 1e-6

SUBLANES = 8
BF16_ROWS = 16
POOL_HALO = POOL_BUF + 1
TOKEN_TILE = 512
PROMPT_FF_CHUNK = 1024
SAMPLE_FF_CHUNK = 512
V7X_VMEM_BYTES = 64 * 1024 * 1024
VMEM_LIMIT_BYTES = V7X_VMEM_BYTES * 29 // 32

_BF16 = jnp.bfloat16
_F32 = jnp.float32


class _Weights(NamedTuple):
    norm_mix: jax.Array
    conv_w: jax.Array
    conv_b: jax.Array
    norm_mlp: jax.Array
    norm_ple: jax.Array
    norm_f: jax.Array
    w_proj: jax.Array
    w_mix: jax.Array
    w_in: jax.Array
    w_up: jax.Array
    w_down: jax.Array
    w_gate: jax.Array


_BIG_FIELDS = _Weights._fields[-4:]
_MIX_FIELD = _Weights._fields.index("w_mix")
CAST_STAGE = (512, 1024)


def _rmsnorm(x, g):
    ms = jnp.mean(x * x, axis=-1, keepdims=True)
    return (x * lax.rsqrt(ms + EPS)) * g


def _dot(a, b):
    return jnp.dot(a, b, preferred_element_type=_F32)


def _norm_split(x, g):
    scale = lax.rsqrt(jnp.mean(x * x, axis=-1, keepdims=True) + EPS)
    return (x * g).astype(_BF16), scale


def _row(vec, layer):
    if isinstance(layer, int):
        return vec[layer:layer + 1, :]
    row = vec[0:1, :]
    for k in range(1, DEPTH):
        row = jnp.where(layer == k, vec[k:k + 1, :], row)
    return row


def _in_proj(h, w, layer):
    hg, scale = _norm_split(h, _row(w.norm_mix[...], layer))
    proj = _dot(hg, w.w_in[...])
    u = proj[:, :POOL_WIDTH] * scale
    bg = proj[:, POOL_WIDTH:POOL_WIDTH + CONV_WIDTH] * scale
    cg = proj[:, POOL_WIDTH + CONV_WIDTH:POOL_WIDTH + 2 * CONV_WIDTH] * scale
    v = proj[:, POOL_WIDTH + 2 * CONV_WIDTH:] * scale
    return u, bg, cg * v


def _fold_kernel(pool_w_ref, pool_scale_ref, w_out_ref, w_mix_ref):
    def split(a):
        head = a.astype(_BF16)
        return head, (a - head.astype(_F32)).astype(_BF16)

    pooling_half = pl.program_id(1) == 0
    for layer in range(DEPTH):
        @pl.when(pooling_half & (pl.program_id(0) == layer))
        def _():
            for g in range(len(POOL_WINDOWS)):
                sl = slice(g * POOL_GC, (g + 1) * POOL_GC)
                a_head, a_rem = split(pool_w_ref[g] * pool_scale_ref[layer:layer + 1, sl])
                b_head, b_rem = split(w_out_ref[sl, :])
                w_mix_ref[sl, :] = _dot(jnp.concatenate([a_head, a_head, a_rem], axis=1),
                                        jnp.concatenate([b_head, b_rem, b_head], axis=0)).astype(_BF16)

    @pl.when(jnp.logical_not(pooling_half))
    def _():
        w_mix_ref[...] = w_out_ref[...].astype(_BF16)


def _fold_pool(pool_w, pool_scale, w_out):
    assert w_out.shape[1] == 2 * POOL_WIDTH
    half = (None, POOL_WIDTH, w_out.shape[2])
    return pl.pallas_call(
        _fold_kernel,
        grid=(DEPTH, 2),
        in_specs=[pl.BlockSpec((None,) + pool_w.shape[1:], lambda l, k: (l, 0, 0, 0)),
                  pl.BlockSpec(pool_scale.shape, lambda l, k: (0, 0)),
                  pl.BlockSpec(half, lambda l, k: (l, k, 0))],
        out_specs=pl.BlockSpec(half, lambda l, k: (l, k, 0)),
        out_shape=jax.ShapeDtypeStruct(w_out.shape, _BF16),
        compiler_params=pltpu.CompilerParams(dimension_semantics=("arbitrary", "arbitrary")),
        name="fold_pool_maps",
    )(pool_w, pool_scale, w_out)


def _conv_out(w, layer, back2, back1, cur):
    tap = lambda k: _row(w.conv_w[k], layer)
    return _row(w.conv_b[...], layer) + (tap(0) * back2 + tap(1) * back1 + tap(2) * cur)


def _out_proj(h, ds, y_conv, w):
    return h + _dot(jnp.concatenate(ds + [y_conv.astype(_BF16)], axis=-1), w.w_mix[...])


def _mlp(h, w, layer, chunk):
    hg, scale = _norm_split(h, _row(w.norm_mlp[...], layer))
    acc = h
    for c in range(D_FF // chunk):
        cs = slice(c * chunk, (c + 1) * chunk)
        f = jnp.maximum(_dot(hg, w.w_up[:, cs]) * scale, 0.0)
        acc = acc + _dot((f * f).astype(_BF16), w.w_down[cs, :])
    return acc


def _gated_embed(h, p, w, layer):
    hg, scale = _norm_split(h, _row(w.norm_ple[...], layer))
    gate = jax.nn.sigmoid(_dot(hg, w.w_gate[...]) * scale)
    return h + gate * _dot(p.astype(_BF16), w.w_proj[layer].astype(_BF16))


def _prompt_kernel(*refs, layer, tile, n_tiles, n_cast, own_cast):
    n_big = len(_BIG_FIELDS)
    n_w = len(_Weights._fields) - (n_big if own_cast else 0)
    x_ref, p_ref = refs[:2]
    pos = 2 + n_w
    prev = refs[pos:pos + 2 * layer]
    pos += 2 * layer
    cast_in = refs[pos:pos + n_cast]
    pos += n_cast
    if own_cast:
        big_hbm = refs[pos:pos + n_big]
        pos += n_big
    out_ref, npool_ref, nconv_ref = refs[pos:pos + 3]
    cast_out = refs[pos + 3:pos + 3 + n_cast]
    pos += 3 + n_cast
    if own_cast:
        big_out = refs[pos:pos + n_big]
        pool_ext, conv_ext = refs[pos + n_big:pos + n_big + 2]
        big_vmem = refs[pos + n_big + 2:pos + 2 * n_big + 2]
        stage, fetch_sem, publish_sem = refs[pos + 2 * n_big + 2:]
        w = _Weights(*refs[2:2 + n_w], *big_vmem)
    else:
        pool_ext, conv_ext = refs[pos:]
        w = _Weights(*refs[2:2 + n_w])
    i = pl.program_id(1)
    halo = POOL_HALO
    chalo = SUBLANES

    if own_cast:
        first = (pl.program_id(0) == 0) & (i == 0)
        last = (pl.program_id(0) == pl.num_programs(0) - 1) & (i == n_tiles - 1)
        rows, cols = stage.shape[1:]
        chunks = [(m, r0, c0) for m in range(n_big)
                  for r0 in range(0, big_vmem[m].shape[0], rows)
                  for c0 in range(0, big_vmem[m].shape[1], cols)]

        def fetch(k):
            m, r0, c0 = chunks[k]
            return pltpu.make_async_copy(big_hbm[m].at[layer, pl.ds(r0, rows), pl.ds(c0, cols)],
                                         stage.at[k % 2], fetch_sem.at[k % 2])

        def publish(m):
            return pltpu.make_async_copy(big_vmem[m], big_out[m], publish_sem.at[m])

        @pl.when(first)
        def _():
            fetch(0).start()
            for k, (m, r0, c0) in enumerate(chunks):
                if k + 1 < len(chunks):
                    fetch(k + 1).start()
                fetch(k).wait()
                big_vmem[m][r0:r0 + rows, c0:c0 + cols] = stage[k % 2].astype(_BF16)
            for m in range(n_big):
                publish(m).start()

        @pl.when(last)
        def _():
            for m in range(n_big):
                publish(m).wait()

    @pl.when(i == 0)
    def _():
        pool_ext[0:halo, :] = jnp.zeros((halo, POOL_WIDTH), _F32)
        conv_ext[0:chalo, :] = jnp.zeros((chalo, CONV_WIDTH), _F32)

    h = x_ref[...]
    u, bg, cv = _in_proj(h, w, layer)
    pool_ext[halo:halo + tile, :] = u
    conv_ext[chalo:chalo + tile, :] = cv

    for src, dst in zip(cast_in, cast_out):
        dst[...] = src[...].astype(_BF16)

    ds, z = _prompt_mixers(pool_ext, conv_ext, i, tile, w, layer)
    h = _mlp(_out_proj(h, ds, bg * z, w), w, layer, PROMPT_FF_CHUNK)
    h = _gated_embed(h, p_ref[...], w, layer)
    out_ref[...] = _rmsnorm(h, w.norm_f[...]) if layer == DEPTH - 1 else h
    _prompt_new_state(pool_ext, conv_ext, npool_ref, nconv_ref, prev, pl.program_id(0), i,
                      tile, n_tiles, layer)


def _prompt_mixers(pool_ext, conv_ext, i, tile, w, layer):
    halo, chalo = POOL_HALO, SUBLANES
    position = i * tile + lax.broadcasted_iota(jnp.int32, (tile, 1), 0)
    ds = []
    for g, win in enumerate(POOL_WINDOWS):
        e = pool_ext[:, g * POOL_GC:(g + 1) * POOL_GC]
        s = e
        span = 1
        while span < win:
            s = s + pltpu.roll(s, span, axis=0)
            span *= 2
        cnt = jnp.minimum(win, position + 1).astype(_F32)
        ds.append((s[halo:, :] / cnt - e[halo:, :]).astype(_BF16))
    cvx = conv_ext[...]
    z = _conv_out(w, layer, pltpu.roll(cvx, 2, axis=0)[chalo:, :],
                  pltpu.roll(cvx, 1, axis=0)[chalo:, :], cvx[chalo:, :])
    return ds, z


def _prompt_new_state(pool_ext, conv_ext, npool_ref, nconv_ref, prev, b, i, tile, n_tiles, layer):
    halo, chalo = POOL_HALO, SUBLANES

    @pl.when(i == n_tiles - 1)
    def _():
        seq = pl.ds(b, 1)
        new_pool = pool_ext[tile + halo - POOL_BUF:tile + halo, :][:, None, :]
        new_conv = conv_ext[tile + chalo - CONV_BUF:tile + chalo, :]
        if layer == 0:
            npool_ref[:, seq, :] = new_pool
            nconv_ref[...] = new_conv
        else:
            for k in range(layer):
                npool_ref[k] = prev[2 * k][...]
                nconv_ref[k] = prev[2 * k + 1][...]
            npool_ref[layer, :, seq, :] = new_pool
            nconv_ref[layer] = new_conv

    @pl.when(i < n_tiles - 1)
    def _():
        pool_ext[0:halo, :] = pool_ext[tile:tile + halo, :]
        conv_ext[0:chalo, :] = conv_ext[tile:tile + chalo, :]


def _prompt_skewed_kernel(*refs, layer, tile, n_tiles):
    n_big = len(_BIG_FIELDS)
    x_ref, p_ref = refs[:2]
    shared = refs[2:2 + _MIX_FIELD]
    w_mix_hbm = refs[2 + _MIX_FIELD]
    pos = 3 + _MIX_FIELD
    w_in_hbm, w_up_hbm, w_down_hbm, w_gate_hbm = refs[pos:pos + n_big]
    prev = refs[pos + n_big:pos + n_big + 2 * layer]
    pos += n_big + 2 * layer
    out_ref, npool_ref, nconv_ref = refs[pos:pos + 3]
    pool_ext, conv_ext, h_mid = refs[pos + 3:pos + 6]
    buffers = refs[pos + 6:pos + 7 + n_big]
    sem = refs[pos + 7 + n_big]
    w_in_buf, w_mix_buf, w_up_buf, w_down_buf, w_gate_buf = buffers
    w = _Weights(*shared, w_mix=w_mix_buf, w_in=w_in_buf, w_up=w_up_buf, w_down=w_down_buf,
                 w_gate=w_gate_buf)
    sources = (w_in_hbm, w_mix_hbm.at[layer], w_up_hbm, w_down_hbm, w_gate_hbm)
    early, late = (0, 1), (2, 3, 4)
    load = lambda m: pltpu.make_async_copy(sources[m], buffers[m], sem.at[m])

    step = pl.program_id(0)
    n_steps = pl.num_programs(0) - 1
    halo, chalo = POOL_HALO, SUBLANES

    def mixing_half():
        tile_index = jnp.minimum(step, n_steps - 1)
        b, i = tile_index // n_tiles, tile_index % n_tiles

        @pl.when(i == 0)
        def _():
            pool_ext[0:halo, :] = jnp.zeros((halo, POOL_WIDTH), _F32)
            conv_ext[0:chalo, :] = jnp.zeros((chalo, CONV_WIDTH), _F32)

        h = x_ref[...]
        u, bg, cv = _in_proj(h, w, layer)
        pool_ext[halo:halo + tile, :] = u
        conv_ext[chalo:chalo + tile, :] = cv
        ds, z = _prompt_mixers(pool_ext, conv_ext, i, tile, w, layer)
        h_mid[...] = _out_proj(h, ds, bg * z, w)
        _prompt_new_state(pool_ext, conv_ext, npool_ref, nconv_ref, prev, b, i, tile, n_tiles, layer)

    def mlp_half():
        h = _mlp(h_mid[...], w, layer, PROMPT_FF_CHUNK)
        h = _gated_embed(h, p_ref[...], w, layer)
        out_ref[...] = _rmsnorm(h, w.norm_f[...]) if layer == DEPTH - 1 else h

    @pl.when(step == 0)
    def _():
        for m in early + late:
            load(m).start()
        for m in early:
            load(m).wait()
        mixing_half()

    @pl.when(step == 1)
    def _():
        for m in late:
            load(m).wait()

    @pl.when((step > 0) & (step < n_steps))
    def _():
        mlp_half()
        mixing_half()

    @pl.when(step == n_steps)
    def _():
        mlp_half()


def _sample_kernel(*refs, tile, n_tiles):
    n_big = len(_BIG_FIELDS)
    x_ref, p_ref, pool_hist, conv_hist = refs[:4]
    shared = refs[4:4 + _MIX_FIELD]
    w_mix_hbm = refs[4 + _MIX_FIELD]
    pos = 5 + _MIX_FIELD
    per_layer = [refs[pos + k * n_big:pos + (k + 1) * n_big] for k in range(DEPTH)]
    pos += DEPTH * n_big
    out_ref, npool_ref, nconv_ref = refs[pos:pos + 3]
    buffers = refs[pos + 3:pos + 4 + n_big]
    h_mid, pool_ext, conv_ext, sem = refs[pos + 4 + n_big:]
    w_in_buf, w_mix_buf, w_up_buf, w_down_buf, w_gate_buf = buffers
    w = _Weights(*shared, w_mix=w_mix_buf, w_in=w_in_buf, w_up=w_up_buf, w_down=w_down_buf,
                 w_gate=w_gate_buf)
    early, late = (0, 1), (2, 3, 4)

    def load(k, m):
        w_in_k, w_up_k, w_down_k, w_gate_k = per_layer[k]
        source = (w_in_k, w_mix_hbm.at[k], w_up_k, w_down_k, w_gate_k)[m]
        return pltpu.make_async_copy(source, buffers[m], sem.at[m])

    step = pl.program_id(0)
    layer = step // (2 * n_tiles)
    mlp_half = (step // n_tiles) % 2 == 1
    tile_index = step % n_tiles
    seqs = tile // SUBLANES
    halo = POOL_HALO * seqs
    chalo = CONV_BUF * seqs
    positions = range(SUBLANES)
    by_position = lambda ref: jnp.concatenate([ref[:, t, :] for t in positions], axis=0)

    for k in range(DEPTH):
        @pl.when(step == 2 * k * n_tiles)
        def _():
            if k == 0:
                for m in early + late:
                    load(0, m).start()
            else:
                for m in late:
                    load(k, m).start()
            for m in early:
                load(k, m).wait()

        @pl.when(step == (2 * k + 1) * n_tiles)
        def _():
            for m in late:
                load(k, m).wait()
            if k + 1 < DEPTH:
                for m in early:
                    load(k + 1, m).start()

    @pl.when(jnp.logical_not(mlp_half))
    def _():
        @pl.when(layer == 0)
        def _():
            h_mid[tile_index] = by_position(x_ref)

        h = h_mid[tile_index]
        u, bg, cv = _in_proj(h, w, layer)

        pool_ext[0:seqs, :] = pool_hist[0]
        pool_ext[seqs:halo, :] = pool_hist[...].reshape(POOL_BUF * seqs, POOL_WIDTH)
        pool_ext[halo:halo + tile, :] = u
        for k in range(CONV_BUF):
            conv_ext[k * seqs:(k + 1) * seqs, :] = conv_hist[:, k, :]
        conv_ext[chalo:chalo + tile, :] = cv

        ds = []
        for g, win in enumerate(POOL_WINDOWS):
            sl = slice(g * POOL_GC, (g + 1) * POOL_GC)
            s = pool_ext[halo - (win - 1) * seqs:halo + tile, sl]
            span = 1
            while span < win:
                s = s[span * seqs:, :] + s[:-span * seqs, :]
                span *= 2
            ds.append((s * (1.0 / win) - u[:, sl]).astype(_BF16))
        z = _conv_out(w, layer, conv_ext[0:tile, :], conv_ext[seqs:seqs + tile, :], cv)

        h_mid[tile_index] = _out_proj(h, ds, bg * z, w)
        npool_ref[...] = pool_ext[halo + tile - POOL_BUF * seqs:halo + tile, :].reshape(
            POOL_BUF, seqs, POOL_WIDTH)
        for k in range(CONV_BUF):
            nconv_ref[:, k, :] = conv_ext[tile + k * seqs:tile + (k + 1) * seqs, :]

    @pl.when(mlp_half)
    def _():
        h = _mlp(h_mid[tile_index], w, layer, SAMPLE_FF_CHUNK)
        h = _gated_embed(h, by_position(p_ref), w, layer)

        @pl.when(layer < DEPTH - 1)
        def _():
            h_mid[tile_index] = h

        @pl.when(layer == DEPTH - 1)
        def _():
            y = _rmsnorm(h, w.norm_f[...])
            for t in positions:
                out_ref[:, t, :] = y[t * seqs:(t + 1) * seqs, :]


def _weight_specs(weights, layer):
    once = pl.Buffered(1)
    specs = [pl.BlockSpec(a.shape, lambda *_, nd=a.ndim: (0,) * nd, pipeline_mode=once)
             for a in weights]
    specs[_MIX_FIELD] = pl.BlockSpec((None,) + weights[_MIX_FIELD].shape[1:],
                                     lambda *_: (layer, 0, 0), pipeline_mode=once)
    return specs


def _prompt_layer(layer, x, p, weights, prev_state, cast_next, cast_own=()):
    batch, seq, _ = x.shape
    tile = TOKEN_TILE
    n_tiles = seq // tile
    steps = batch * n_tiles
    assert seq == n_tiles * tile and tile >= POOL_HALO and len(prev_state) == 2 * layer
    for a in cast_next:
        assert a.shape[1] % (steps * BF16_ROWS) == 0
    for a in cast_own:
        assert a.shape[1] % CAST_STAGE[0] == 0 and a.shape[2] % CAST_STAGE[1] == 0
    assert len(weights) + len(cast_own) == len(_Weights._fields)
    hbm = pl.BlockSpec(memory_space=pl.ANY)
    whole = lambda shape: pl.BlockSpec(shape, lambda b, i, nd=len(shape): (0,) * nd)
    per_batch = pl.BlockSpec((None, CONV_BUF, CONV_WIDTH), lambda b, i: (b, 0, 0))
    pool_shape = (POOL_BUF, batch, POOL_WIDTH)
    in_specs = [
        pl.BlockSpec((None, tile, D_MODEL), lambda b, i: (b, i, 0)),
        pl.BlockSpec((None, None, tile, PLE_DIM), lambda b, i: (layer, b, i, 0)),
    ] + _weight_specs(weights, layer) + [whole(pool_shape), per_batch] * layer + [
        pl.BlockSpec((None, a.shape[1] // steps, a.shape[2]),
                     lambda b, i: (layer + 1, b * n_tiles + i, 0)) for a in cast_next] + [
        hbm] * len(cast_own)
    out_specs = [pl.BlockSpec((None, tile, D_MODEL), lambda b, i: (b, i, 0))]
    out_shape = [jax.ShapeDtypeStruct((batch, seq, D_MODEL), _F32)]
    if layer == 0:
        out_specs += [whole(pool_shape), per_batch]
        out_shape += [jax.ShapeDtypeStruct(pool_shape, _F32),
                      jax.ShapeDtypeStruct((batch, CONV_BUF, CONV_WIDTH), _F32)]
    else:
        out_specs += [whole((layer + 1,) + pool_shape),
                      pl.BlockSpec((layer + 1, None, CONV_BUF, CONV_WIDTH), lambda b, i: (0, b, 0, 0))]
        out_shape += [jax.ShapeDtypeStruct((layer + 1,) + pool_shape, _F32),
                      jax.ShapeDtypeStruct((layer + 1, batch, CONV_BUF, CONV_WIDTH), _F32)]
    out_specs += [pl.BlockSpec((a.shape[1] // steps, a.shape[2]), lambda b, i: (b * n_tiles + i, 0))
                  for a in cast_next]
    out_shape += [jax.ShapeDtypeStruct(a.shape[1:], _BF16) for a in cast_next]
    scratch_shapes = [pltpu.VMEM((POOL_HALO + tile, POOL_WIDTH), _F32),
                      pltpu.VMEM((SUBLANES + tile, CONV_WIDTH), _F32)]
    if cast_own:
        out_specs += [hbm] * len(cast_own)
        out_shape += [jax.ShapeDtypeStruct(a.shape[1:], _BF16) for a in cast_own]
        scratch_shapes += [pltpu.VMEM(a.shape[1:], _BF16) for a in cast_own] + [
            pltpu.VMEM((2,) + CAST_STAGE, _F32),
            pltpu.SemaphoreType.DMA((2,)), pltpu.SemaphoreType.DMA((len(cast_own),))]
    return pl.pallas_call(
        functools.partial(_prompt_kernel, layer=layer, tile=tile, n_tiles=n_tiles,
                          n_cast=len(cast_next), own_cast=bool(cast_own)),
        grid=(batch, n_tiles),
        in_specs=in_specs,
        out_specs=out_specs,
        out_shape=out_shape,
        scratch_shapes=scratch_shapes,
        compiler_params=pltpu.CompilerParams(
            dimension_semantics=("arbitrary", "arbitrary"),
            vmem_limit_bytes=VMEM_LIMIT_BYTES),
        name=f"prompt_layer{layer}",
    )(x, p, *weights, *prev_state, *cast_next, *cast_own)


def _prompt_last_layer(layer, x, p, shared, w_mix, big, prev_state):
    batch, seq, _ = x.shape
    tile = TOKEN_TILE
    n_tiles = seq // tile
    steps = batch * n_tiles
    assert seq == n_tiles * tile and tile >= POOL_HALO and len(prev_state) == 2 * layer
    assert len(shared) == _MIX_FIELD and len(big) == len(_BIG_FIELDS)
    mixing = lambda fn: lambda s: fn(*divmod(jnp.minimum(s, steps - 1), n_tiles))
    mlp = lambda fn: lambda s: fn(*divmod(jnp.maximum(s - 1, 0), n_tiles))
    whole = lambda shape: pl.BlockSpec(shape, lambda s, nd=len(shape): (0,) * nd)
    per_batch = pl.BlockSpec((None, CONV_BUF, CONV_WIDTH), mixing(lambda b, i: (b, 0, 0)))
    pool_shape = (POOL_BUF, batch, POOL_WIDTH)
    hbm = pl.BlockSpec(memory_space=pl.ANY)
    in_specs = [
        pl.BlockSpec((None, tile, D_MODEL), mixing(lambda b, i: (b, i, 0))),
        pl.BlockSpec((None, None, tile, PLE_DIM), mlp(lambda b, i: (layer, b, i, 0))),
    ] + [pl.BlockSpec(a.shape, lambda s, nd=a.ndim: (0,) * nd, pipeline_mode=pl.Buffered(1))
         for a in shared] + [hbm] * (1 + len(big)) + [whole(pool_shape), per_batch] * layer
    out_specs = [
        pl.BlockSpec((None, tile, D_MODEL), mlp(lambda b, i: (b, i, 0))),
        whole((layer + 1,) + pool_shape),
        pl.BlockSpec((layer + 1, None, CONV_BUF, CONV_WIDTH), mixing(lambda b, i: (0, b, 0, 0))),
    ]
    out_shape = [jax.ShapeDtypeStruct((batch, seq, D_MODEL), _F32),
                 jax.ShapeDtypeStruct((layer + 1,) + pool_shape, _F32),
                 jax.ShapeDtypeStruct((layer + 1, batch, CONV_BUF, CONV_WIDTH), _F32)]
    w_in, w_up, w_down, w_gate = big
    buffers = [w_in.shape, w_mix.shape[1:], w_up.shape, w_down.shape, w_gate.shape]
    return pl.pallas_call(
        functools.partial(_prompt_skewed_kernel, layer=layer, tile=tile, n_tiles=n_tiles),
        grid=(steps + 1,),
        in_specs=in_specs,
        out_specs=out_specs,
        out_shape=out_shape,
        scratch_shapes=[pltpu.VMEM((POOL_HALO + tile, POOL_WIDTH), _F32),
                        pltpu.VMEM((SUBLANES + tile, CONV_WIDTH), _F32),
                        pltpu.VMEM((tile, D_MODEL), _F32)] + [
            pltpu.VMEM(shape, _BF16) for shape in buffers] + [
            pltpu.SemaphoreType.DMA((len(buffers),))],
        compiler_params=pltpu.CompilerParams(
            dimension_semantics=("arbitrary",),
            vmem_limit_bytes=VMEM_LIMIT_BYTES),
        name=f"prompt_layer{layer}",
    )(x, p, *shared, w_mix, *big, *prev_state)


def _sample_trunk(x, p, state_pool, state_conv, shared, w_mix, per_layer):
    n_seq = x.shape[0]
    seqs = TOKEN_TILE // SUBLANES
    tile = seqs * SUBLANES
    n_tiles = n_seq // seqs
    assert n_seq == n_tiles * seqs and x.shape[1] == SUBLANES and len(shared) == _MIX_FIELD
    assert len(per_layer) == DEPTH and all(len(big) == len(_BIG_FIELDS) for big in per_layer)
    last = n_tiles - 1

    def at(mixing, mlp):
        def index(s):
            layer, tile_index = s // (2 * n_tiles), s % n_tiles
            in_mlp_half = (s // n_tiles) % 2 == 1
            a, b = mixing(layer, tile_index), mlp(layer, tile_index)
            return tuple(jnp.where(in_mlp_half, j, i) for i, j in zip(a, b))
        return index

    hbm = pl.BlockSpec(memory_space=pl.ANY)
    big = [a for layer_big in per_layer for a in layer_big]
    state_rows = at(lambda l, i: (l, 0, i, 0), lambda l, i: (l, 0, last, 0))
    conv_rows = at(lambda l, i: (l, i, 0, 0), lambda l, i: (l, last, 0, 0))
    in_specs = [
        pl.BlockSpec((seqs, SUBLANES, D_MODEL), lambda s: (jnp.minimum(s, last), 0, 0)),
        pl.BlockSpec((None, seqs, SUBLANES, PLE_DIM),
                     at(lambda l, i: (l, 0, 0, 0), lambda l, i: (l, i, 0, 0))),
        pl.BlockSpec((None, POOL_BUF, seqs, POOL_WIDTH), state_rows),
        pl.BlockSpec((None, seqs, CONV_BUF, CONV_WIDTH), conv_rows),
    ] + [pl.BlockSpec(a.shape, lambda s, nd=a.ndim: (0,) * nd, pipeline_mode=pl.Buffered(1))
         for a in shared] + [hbm] * (1 + len(big))
    out_specs = [
        pl.BlockSpec((seqs, SUBLANES, D_MODEL),
                     at(lambda l, i: (0, 0, 0),
                        lambda l, i: (jnp.where(l == DEPTH - 1, i, 0), 0, 0))),
        pl.BlockSpec((None, POOL_BUF, seqs, POOL_WIDTH), state_rows),
        pl.BlockSpec((None, seqs, CONV_BUF, CONV_WIDTH), conv_rows),
    ]
    out_shape = [jax.ShapeDtypeStruct((n_seq, SUBLANES, D_MODEL), _F32),
                 jax.ShapeDtypeStruct((DEPTH, POOL_BUF, n_seq, POOL_WIDTH), _F32),
                 jax.ShapeDtypeStruct((DEPTH, n_seq, CONV_BUF, CONV_WIDTH), _F32)]
    w_in, w_up, w_down, w_gate = per_layer[0]
    buffers = [w_in.shape, w_mix.shape[1:], w_up.shape, w_down.shape, w_gate.shape]
    return pl.pallas_call(
        functools.partial(_sample_kernel, tile=tile, n_tiles=n_tiles),
        grid=(2 * DEPTH * n_tiles,),
        in_specs=in_specs,
        out_specs=out_specs,
        out_shape=out_shape,
        scratch_shapes=[pltpu.VMEM(shape, _BF16) for shape in buffers] + [
            pltpu.VMEM((n_tiles, tile, D_MODEL), _F32),
            pltpu.VMEM(((POOL_HALO + SUBLANES) * seqs, POOL_WIDTH), _F32),
            pltpu.VMEM(((CONV_BUF + SUBLANES) * seqs, CONV_WIDTH), _F32),
            pltpu.SemaphoreType.DMA((len(buffers),))],
        compiler_params=pltpu.CompilerParams(
            dimension_semantics=("arbitrary",),
            vmem_limit_bytes=VMEM_LIMIT_BYTES),
        name="sample_trunk",
    )(x, p, state_pool, state_conv, *shared, w_mix, *big)


def kernel(x_prompt, x_sample, state_pool, state_conv, p_prompt, p_sample, norm_mix, w_in, pool_w, pool_scale, conv_w, conv_b, w_out, norm_mlp, w_up, w_down, norm_ple, w_ple_gate, w_ple_proj, norm_f):
    assert DEPTH == 2 and PAST_LEN >= POOL_BUF
    shared = (norm_mix, jnp.swapaxes(conv_w, 0, 1), conv_b, norm_mlp, norm_ple,
              norm_f.reshape(1, D_MODEL), w_ple_proj)
    big_f32 = (w_in, w_up, w_down, w_ple_gate)

    w_mix = _fold_pool(pool_w, pool_scale, w_out)

    n_big = len(big_f32)
    h, pool_p, conv_p, *big_bf16 = _prompt_layer(0, x_prompt, p_prompt, shared + (w_mix,), (),
                                                 big_f32, big_f32)
    per_layer = [big_bf16[n_big:], big_bf16[:n_big]]
    y_prompt, pool_p, conv_p = _prompt_last_layer(1, h, p_prompt, shared, w_mix, per_layer[1],
                                                  (pool_p, conv_p))

    y_sample, pool_s, conv_s = _sample_trunk(x_sample, p_sample, jnp.swapaxes(state_pool, 1, 2),
                                             state_conv, shared, w_mix, per_layer)
    return (y_prompt, y_sample, jnp.swapaxes(pool_p, 1, 2), conv_p, jnp.swapaxes(pool_s, 1, 2), conv_s)
```

```python
import functools
from typing import NamedTuple

import jax
import jax.numpy as jnp
from jax import lax
from jax.experimental import pallas as pl
from jax.experimental.pallas import tpu as pltpu

D_MODEL = 1024
DEPTH = 2
PAST_LEN = 16384
POOL_WIDTH = 512
CONV_WIDTH = 512
POOL_WINDOWS = (2, 4, 8, 16)
POOL_GC = 128
POOL_BUF = 15
CONV_BUF = 2
D_FF = 4 * D_MODEL
PLE_DIM = 256
IN_COLS = POOL_WIDTH + 3 * CONV_WIDTH
EPS = 1e-6

SUBLANES = 8
BF16_ROWS = 16
POOL_HALO = POOL_BUF + 1
TOKEN_TILE = 512
PROMPT_FF_CHUNK = 1024
SAMPLE_FF_CHUNK = 1024
V7X_VMEM_BYTES = 64 * 1024 * 1024
VMEM_LIMIT_BYTES = V7X_VMEM_BYTES * 29 // 32

_BF16 = jnp.bfloat16
_F32 = jnp.float32


class _Weights(NamedTuple):
    norm_mix: jax.Array
    conv_w: jax.Array
    conv_b: jax.Array
    norm_mlp: jax.Array
    norm_ple: jax.Array
    norm_f: jax.Array
    w_proj: jax.Array
    w_mix: jax.Array
    w_in: jax.Array
    w_up: jax.Array
    w_down: jax.Array
    w_gate: jax.Array


_BIG_FIELDS = _Weights._fields[-4:]
_MIX_FIELD = _Weights._fields.index("w_mix")
CAST_STAGE = (512, 1024)


def _rmsnorm(x, g):
    ms = jnp.mean(x * x, axis=-1, keepdims=True)
    return (x * lax.rsqrt(ms + EPS)) * g


def _dot(a, b):
    return jnp.dot(a, b, preferred_element_type=_F32)


def _norm_split(x, g):
    scale = lax.rsqrt(jnp.mean(x * x, axis=-1, keepdims=True) + EPS)
    return (x * g).astype(_BF16), scale


def _row(vec, layer):
    if isinstance(layer, int):
        return vec[layer:layer + 1, :]
    row = vec[0:1, :]
    for k in range(1, DEPTH):
        row = jnp.where(layer == k, vec[k:k + 1, :], row)
    return row


def _in_proj(h, w, layer):
    hg, scale = _norm_split(h, _row(w.norm_mix[...], layer))
    proj = _dot(hg, w.w_in[...])
    u = proj[:, :POOL_WIDTH] * scale
    bg = proj[:, POOL_WIDTH:POOL_WIDTH + CONV_WIDTH] * scale
    cg = proj[:, POOL_WIDTH + CONV_WIDTH:POOL_WIDTH + 2 * CONV_WIDTH] * scale
    v = proj[:, POOL_WIDTH + 2 * CONV_WIDTH:] * scale
    return u, bg, cg * v


def _fold_kernel(pool_w_ref, pool_scale_ref, w_out_ref, w_mix_ref):
    def split(a):
        head = a.astype(_BF16)
        return head, (a - head.astype(_F32)).astype(_BF16)

    pooling_half = pl.program_id(1) == 0
    for layer in range(DEPTH):
        @pl.when(pooling_half & (pl.program_id(0) == layer))
        def _():
            for g in range(len(POOL_WINDOWS)):
                sl = slice(g * POOL_GC, (g + 1) * POOL_GC)
                a_head, a_rem = split(pool_w_ref[g] * pool_scale_ref[layer:layer + 1, sl])
                b_head, b_rem = split(w_out_ref[sl, :])
                w_mix_ref[sl, :] = _dot(jnp.concatenate([a_head, a_head, a_rem], axis=1),
                                        jnp.concatenate([b_head, b_rem, b_head], axis=0)).astype(_BF16)

    @pl.when(jnp.logical_not(pooling_half))
    def _():
        w_mix_ref[...] = w_out_ref[...].astype(_BF16)


def _fold_pool(pool_w, pool_scale, w_out):
    assert w_out.shape[1] == 2 * POOL_WIDTH
    half = (None, POOL_WIDTH, w_out.shape[2])
    return pl.pallas_call(
        _fold_kernel,
        grid=(DEPTH, 2),
        in_specs=[pl.BlockSpec((None,) + pool_w.shape[1:], lambda l, k: (l, 0, 0, 0)),
                  pl.BlockSpec(pool_scale.shape, lambda l, k: (0, 0)),
                  pl.BlockSpec(half, lambda l, k: (l, k, 0))],
        out_specs=pl.BlockSpec(half, lambda l, k: (l, k, 0)),
        out_shape=jax.ShapeDtypeStruct(w_out.shape, _BF16),
        compiler_params=pltpu.CompilerParams(dimension_semantics=("arbitrary", "arbitrary")),
        name="fold_pool_maps",
    )(pool_w, pool_scale, w_out)


def _conv_out(w, layer, back2, back1, cur):
    tap = lambda k: _row(w.conv_w[k], layer)
    return _row(w.conv_b[...], layer) + (tap(0) * back2 + tap(1) * back1 + tap(2) * cur)


def _out_proj(h, ds, y_conv, w):
    return h + _dot(jnp.concatenate(ds + [y_conv.astype(_BF16)], axis=-1), w.w_mix[...])


def _mlp(h, w, layer, chunk):
    hg, scale = _norm_split(h, _row(w.norm_mlp[...], layer))
    acc = h
    for c in range(D_FF // chunk):
        cs = slice(c * chunk, (c + 1) * chunk)
        f = jnp.maximum(_dot(hg, w.w_up[:, cs]) * scale, 0.0)
        acc = acc + _dot((f * f).astype(_BF16), w.w_down[cs, :])
    return acc


def _gated_embed(h, p, w, layer):
    hg, scale = _norm_split(h, _row(w.norm_ple[...], layer))
    gate = jax.nn.sigmoid(_dot(hg, w.w_gate[...]) * scale)
    return h + gate * _dot(p.astype(_BF16), w.w_proj[layer].astype(_BF16))


def _prompt_kernel(*refs, layer, tile, n_tiles, n_cast, own_cast):
    n_big = len(_BIG_FIELDS)
    n_w = len(_Weights._fields) - (n_big if own_cast else 0)
    x_ref, p_ref = refs[:2]
    pos = 2 + n_w
    prev = refs[pos:pos + 2 * layer]
    pos += 2 * layer
    cast_in = refs[pos:pos + n_cast]
    pos += n_cast
    if own_cast:
        big_hbm = refs[pos:pos + n_big]
        pos += n_big
    out_ref, npool_ref, nconv_ref = refs[pos:pos + 3]
    cast_out = refs[pos + 3:pos + 3 + n_cast]
    pos += 3 + n_cast
    if own_cast:
        big_out = refs[pos:pos + n_big]
        pool_ext, conv_ext = refs[pos + n_big:pos + n_big + 2]
        big_vmem = refs[pos + n_big + 2:pos + 2 * n_big + 2]
        stage, fetch_sem, publish_sem = refs[pos + 2 * n_big + 2:]
        w = _Weights(*refs[2:2 + n_w], *big_vmem)
    else:
        pool_ext, conv_ext = refs[pos:]
        w = _Weights(*refs[2:2 + n_w])
    i = pl.program_id(1)
    halo = POOL_HALO
    chalo = SUBLANES

    if own_cast:
        first = (pl.program_id(0) == 0) & (i == 0)
        last = (pl.program_id(0) == pl.num_programs(0) - 1) & (i == n_tiles - 1)
        rows, cols = stage.shape[1:]
        chunks = [(m, r0, c0) for m in range(n_big)
                  for r0 in range(0, big_vmem[m].shape[0], rows)
                  for c0 in range(0, big_vmem[m].shape[1], cols)]

        def fetch(k):
            m, r0, c0 = chunks[k]
            return pltpu.make_async_copy(big_hbm[m].at[layer, pl.ds(r0, rows), pl.ds(c0, cols)],
                                         stage.at[k % 2], fetch_sem.at[k % 2])

        def publish(m):
            return pltpu.make_async_copy(big_vmem[m], big_out[m], publish_sem.at[m])

        @pl.when(first)
        def _():
            fetch(0).start()
            for k, (m, r0, c0) in enumerate(chunks):
                if k + 1 < len(chunks):
                    fetch(k + 1).start()
                fetch(k).wait()
                big_vmem[m][r0:r0 + rows, c0:c0 + cols] = stage[k % 2].astype(_BF16)
            for m in range(n_big):
                publish(m).start()

        @pl.when(last)
        def _():
            for m in range(n_big):
                publish(m).wait()

    @pl.when(i == 0)
    def _():
        pool_ext[0:halo, :] = jnp.zeros((halo, POOL_WIDTH), _F32)
        conv_ext[0:chalo, :] = jnp.zeros((chalo, CONV_WIDTH), _F32)

    h = x_ref[...]
    u, bg, cv = _in_proj(h, w, layer)
    pool_ext[halo:halo + tile, :] = u
    conv_ext[chalo:chalo + tile, :] = cv

    for src, dst in zip(cast_in, cast_out):
        dst[...] = src[...].astype(_BF16)

    position = i * tile + lax.broadcasted_iota(jnp.int32, (tile, 1), 0)
    ds = []
    for g, win in enumerate(POOL_WINDOWS):
        e = pool_ext[:, g * POOL_GC:(g + 1) * POOL_GC]
        s = e
        span = 1
        while span < win:
            s = s + pltpu.roll(s, span, axis=0)
            span *= 2
        cnt = jnp.minimum(win, position + 1).astype(_F32)
        ds.append((s[halo:, :] / cnt - e[halo:, :]).astype(_BF16))
    cvx = conv_ext[...]
    z = _conv_out(w, layer, pltpu.roll(cvx, 2, axis=0)[chalo:, :],
                  pltpu.roll(cvx, 1, axis=0)[chalo:, :], cvx[chalo:, :])

    h = _mlp(_out_proj(h, ds, bg * z, w), w, layer, PROMPT_FF_CHUNK)
    h = _gated_embed(h, p_ref[...], w, layer)
    out_ref[...] = _rmsnorm(h, w.norm_f[...]) if layer == DEPTH - 1 else h

    @pl.when(i == n_tiles - 1)
    def _():
        seq = pl.ds(pl.program_id(0), 1)
        new_pool = pool_ext[tile + halo - POOL_BUF:tile + halo, :][:, None, :]
        new_conv = conv_ext[tile + chalo - CONV_BUF:tile + chalo, :]
        if layer == 0:
            npool_ref[:, seq, :] = new_pool
            nconv_ref[...] = new_conv
        else:
            for k in range(layer):
                npool_ref[k] = prev[2 * k][...]
                nconv_ref[k] = prev[2 * k + 1][...]
            npool_ref[layer, :, seq, :] = new_pool
            nconv_ref[layer] = new_conv

    @pl.when(i < n_tiles - 1)
    def _():
        pool_ext[0:halo, :] = pool_ext[tile:tile + halo, :]
        conv_ext[0:chalo, :] = conv_ext[tile:tile + chalo, :]


def _sample_kernel(*refs, tile, n_tiles):
    n_big = len(_BIG_FIELDS)
    x_ref, p_ref, pool_hist, conv_hist = refs[:4]
    shared = refs[4:4 + _MIX_FIELD]
    w_mix_hbm = refs[4 + _MIX_FIELD]
    pos = 5 + _MIX_FIELD
    per_layer = [refs[pos + k * n_big:pos + (k + 1) * n_big] for k in range(DEPTH)]
    pos += DEPTH * n_big
    out_ref, npool_ref, nconv_ref = refs[pos:pos + 3]
    buffers = refs[pos + 3:pos + 4 + n_big]
    h_mid, pool_ext, conv_ext, sem = refs[pos + 4 + n_big:]
    w_in_buf, w_mix_buf, w_up_buf, w_down_buf, w_gate_buf = buffers
    w = _Weights(*shared, w_mix=w_mix_buf, w_in=w_in_buf, w_up=w_up_buf, w_down=w_down_buf,
                 w_gate=w_gate_buf)
    early, late = (0, 1), (2, 3, 4)

    def load(k, m):
        w_in_k, w_up_k, w_down_k, w_gate_k = per_layer[k]
        source = (w_in_k, w_mix_hbm.at[k], w_up_k, w_down_k, w_gate_k)[m]
        return pltpu.make_async_copy(source, buffers[m], sem.at[m])

    step = pl.program_id(0)
    layer = step // (2 * n_tiles)
    mlp_half = (step // n_tiles) % 2 == 1
    tile_index = step % n_tiles
    seqs = tile // SUBLANES
    halo = POOL_HALO * seqs
    chalo = CONV_BUF * seqs
    positions = range(SUBLANES)
    by_position = lambda ref: jnp.concatenate([ref[:, t, :] for t in positions], axis=0)

    for k in range(DEPTH):
        @pl.when(step == 2 * k * n_tiles)
        def _():
            if k == 0:
                for m in early + late:
                    load(0, m).start()
            else:
                for m in late:
                    load(k, m).start()
            for m in early:
                load(k, m).wait()

        @pl.when(step == (2 * k + 1) * n_tiles)
        def _():
            for m in late:
                load(k, m).wait()
            if k + 1 < DEPTH:
                for m in early:
                    load(k + 1, m).start()

    @pl.when(jnp.logical_not(mlp_half))
    def _():
        h = jnp.where(layer == 0, by_position(x_ref), h_mid[tile_index])
        u, bg, cv = _in_proj(h, w, layer)

        pool_ext[0:seqs, :] = pool_hist[0]
        pool_ext[seqs:halo, :] = pool_hist[...].reshape(POOL_BUF * seqs, POOL_WIDTH)
        pool_ext[halo:halo + tile, :] = u
        for k in range(CONV_BUF):
            conv_ext[k * seqs:(k + 1) * seqs, :] = conv_hist[:, k, :]
        conv_ext[chalo:chalo + tile, :] = cv

        ds = []
        for g, win in enumerate(POOL_WINDOWS):
            sl = slice(g * POOL_GC, (g + 1) * POOL_GC)
            s = pool_ext[halo - (win - 1) * seqs:halo + tile, sl]
            span = 1
            while span < win:
                s = s[span * seqs:, :] + s[:-span * seqs, :]
                span *= 2
            ds.append((s * (1.0 / win) - u[:, sl]).astype(_BF16))
        z = _conv_out(w, layer, conv_ext[0:tile, :], conv_ext[seqs:seqs + tile, :], cv)

        h_mid[tile_index] = _out_proj(h, ds, bg * z, w)
        npool_ref[...] = pool_ext[halo + tile - POOL_BUF * seqs:halo + tile, :].reshape(
            POOL_BUF, seqs, POOL_WIDTH)
        for k in range(CONV_BUF):
            nconv_ref[:, k, :] = conv_ext[tile + k * seqs:tile + (k + 1) * seqs, :]

    @pl.when(mlp_half)
    def _():
        h = _mlp(h_mid[tile_index], w, layer, SAMPLE_FF_CHUNK)
        h_mid[tile_index] = _gated_embed(h, by_position(p_ref), w, layer)

        @pl.when(layer == DEPTH - 1)
        def _():
            y = _rmsnorm(h_mid[tile_index], w.norm_f[...])
            for t in positions:
                out_ref[:, t, :] = y[t * seqs:(t + 1) * seqs, :]


def _weight_specs(weights, layer):
    once = pl.Buffered(1)
    specs = [pl.BlockSpec(a.shape, lambda *_, nd=a.ndim: (0,) * nd, pipeline_mode=once)
             for a in weights]
    specs[_MIX_FIELD] = pl.BlockSpec((None,) + weights[_MIX_FIELD].shape[1:],
                                     lambda *_: (layer, 0, 0), pipeline_mode=once)
    return specs


def _prompt_layer(layer, x, p, weights, prev_state, cast_next, cast_own=()):
    batch, seq, _ = x.shape
    tile = TOKEN_TILE
    n_tiles = seq // tile
    steps = batch * n_tiles
    assert seq == n_tiles * tile and tile >= POOL_HALO and len(prev_state) == 2 * layer
    for a in cast_next:
        assert a.shape[1] % (steps * BF16_ROWS) == 0
    for a in cast_own:
        assert a.shape[1] % CAST_STAGE[0] == 0 and a.shape[2] % CAST_STAGE[1] == 0
    assert len(weights) + len(cast_own) == len(_Weights._fields)
    hbm = pl.BlockSpec(memory_space=pl.ANY)
    whole = lambda shape: pl.BlockSpec(shape, lambda b, i, nd=len(shape): (0,) * nd)
    per_batch = pl.BlockSpec((None, CONV_BUF, CONV_WIDTH), lambda b, i: (b, 0, 0))
    pool_shape = (POOL_BUF, batch, POOL_WIDTH)
    in_specs = [
        pl.BlockSpec((None, tile, D_MODEL), lambda b, i: (b, i, 0)),
        pl.BlockSpec((None, None, tile, PLE_DIM), lambda b, i: (layer, b, i, 0)),
    ] + _weight_specs(weights, layer) + [whole(pool_shape), per_batch] * layer + [
        pl.BlockSpec((None, a.shape[1] // steps, a.shape[2]),
                     lambda b, i: (layer + 1, b * n_tiles + i, 0)) for a in cast_next] + [
        hbm] * len(cast_own)
    out_specs = [pl.BlockSpec((None, tile, D_MODEL), lambda b, i: (b, i, 0))]
    out_shape = [jax.ShapeDtypeStruct((batch, seq, D_MODEL), _F32)]
    if layer == 0:
        out_specs += [whole(pool_shape), per_batch]
        out_shape += [jax.ShapeDtypeStruct(pool_shape, _F32),
                      jax.ShapeDtypeStruct((batch, CONV_BUF, CONV_WIDTH), _F32)]
    else:
        out_specs += [whole((layer + 1,) + pool_shape),
                      pl.BlockSpec((layer + 1, None, CONV_BUF, CONV_WIDTH), lambda b, i: (0, b, 0, 0))]
        out_shape += [jax.ShapeDtypeStruct((layer + 1,) + pool_shape, _F32),
                      jax.ShapeDtypeStruct((layer + 1, batch, CONV_BUF, CONV_WIDTH), _F32)]
    out_specs += [pl.BlockSpec((a.shape[1] // steps, a.shape[2]), lambda b, i: (b * n_tiles + i, 0))
                  for a in cast_next]
    out_shape += [jax.ShapeDtypeStruct(a.shape[1:], _BF16) for a in cast_next]
    scratch_shapes = [pltpu.VMEM((POOL_HALO + tile, POOL_WIDTH), _F32),
                      pltpu.VMEM((SUBLANES + tile, CONV_WIDTH), _F32)]
    if cast_own:
        out_specs += [hbm] * len(cast_own)
        out_shape += [jax.ShapeDtypeStruct(a.shape[1:], _BF16) for a in cast_own]
        scratch_shapes += [pltpu.VMEM(a.shape[1:], _BF16) for a in cast_own] + [
            pltpu.VMEM((2,) + CAST_STAGE, _F32),
            pltpu.SemaphoreType.DMA((2,)), pltpu.SemaphoreType.DMA((len(cast_own),))]
    return pl.pallas_call(
        functools.partial(_prompt_kernel, layer=layer, tile=tile, n_tiles=n_tiles,
                          n_cast=len(cast_next), own_cast=bool(cast_own)),
        grid=(batch, n_tiles),
        in_specs=in_specs,
        out_specs=out_specs,
        out_shape=out_shape,
        scratch_shapes=scratch_shapes,
        compiler_params=pltpu.CompilerParams(
            dimension_semantics=("arbitrary", "arbitrary"),
            vmem_limit_bytes=VMEM_LIMIT_BYTES),
        name=f"prompt_layer{layer}",
    )(x, p, *weights, *prev_state, *cast_next, *cast_own)


def _sample_trunk(x, p, state_pool, state_conv, shared, w_mix, per_layer):
    n_seq = x.shape[0]
    seqs = TOKEN_TILE // SUBLANES
    tile = seqs * SUBLANES
    n_tiles = n_seq // seqs
    assert n_seq == n_tiles * seqs and x.shape[1] == SUBLANES and len(shared) == _MIX_FIELD
    assert len(per_layer) == DEPTH and all(len(big) == len(_BIG_FIELDS) for big in per_layer)
    last = n_tiles - 1

    def at(mixing, mlp):
        def index(s):
            layer, tile_index = s // (2 * n_tiles), s % n_tiles
            in_mlp_half = (s // n_tiles) % 2 == 1
            a, b = mixing(layer, tile_index), mlp(layer, tile_index)
            return tuple(jnp.where(in_mlp_half, j, i) for i, j in zip(a, b))
        return index

    hbm = pl.BlockSpec(memory_space=pl.ANY)
    big = [a for layer_big in per_layer for a in layer_big]
    state_rows = at(lambda l, i: (l, 0, i, 0), lambda l, i: (l, 0, last, 0))
    conv_rows = at(lambda l, i: (l, i, 0, 0), lambda l, i: (l, last, 0, 0))
    in_specs = [
        pl.BlockSpec((seqs, SUBLANES, D_MODEL), lambda s: (jnp.minimum(s, last), 0, 0)),
        pl.BlockSpec((None, seqs, SUBLANES, PLE_DIM),
                     at(lambda l, i: (l, 0, 0, 0), lambda l, i: (l, i, 0, 0))),
        pl.BlockSpec((None, POOL_BUF, seqs, POOL_WIDTH), state_rows),
        pl.BlockSpec((None, seqs, CONV_BUF, CONV_WIDTH), conv_rows),
    ] + [pl.BlockSpec(a.shape, lambda s, nd=a.ndim: (0,) * nd, pipeline_mode=pl.Buffered(1))
         for a in shared] + [hbm] * (1 + len(big))
    out_specs = [
        pl.BlockSpec((seqs, SUBLANES, D_MODEL),
                     at(lambda l, i: (0, 0, 0),
                        lambda l, i: (jnp.where(l == DEPTH - 1, i, 0), 0, 0))),
        pl.BlockSpec((None, POOL_BUF, seqs, POOL_WIDTH), state_rows),
        pl.BlockSpec((None, seqs, CONV_BUF, CONV_WIDTH), conv_rows),
    ]
    out_shape = [jax.ShapeDtypeStruct((n_seq, SUBLANES, D_MODEL), _F32),
                 jax.ShapeDtypeStruct((DEPTH, POOL_BUF, n_seq, POOL_WIDTH), _F32),
                 jax.ShapeDtypeStruct((DEPTH, n_seq, CONV_BUF, CONV_WIDTH), _F32)]
    w_in, w_up, w_down, w_gate = per_layer[0]
    buffers = [w_in.shape, w_mix.shape[1:], w_up.shape, w_down.shape, w_gate.shape]
    return pl.pallas_call(
        functools.partial(_sample_kernel, tile=tile, n_tiles=n_tiles),
        grid=(2 * DEPTH * n_tiles,),
        in_specs=in_specs,
        out_specs=out_specs,
        out_shape=out_shape,
        scratch_shapes=[pltpu.VMEM(shape, _BF16) for shape in buffers] + [
            pltpu.VMEM((n_tiles, tile, D_MODEL), _F32),
            pltpu.VMEM(((POOL_HALO + SUBLANES) * seqs, POOL_WIDTH), _F32),
            pltpu.VMEM(((CONV_BUF + SUBLANES) * seqs, CONV_WIDTH), _F32),
            pltpu.SemaphoreType.DMA((len(buffers),))],
        compiler_params=pltpu.CompilerParams(
            dimension_semantics=("arbitrary",),
            vmem_limit_bytes=VMEM_LIMIT_BYTES),
        name="sample_trunk",
    )(x, p, state_pool, state_conv, *shared, w_mix, *big)


def kernel(x_prompt, x_sample, state_pool, state_conv, p_prompt, p_sample, norm_mix, w_in, pool_w, pool_scale, conv_w, conv_b, w_out, norm_mlp, w_up, w_down, norm_ple, w_ple_gate, w_ple_proj, norm_f):
    assert DEPTH == 2 and PAST_LEN >= POOL_BUF
    shared = (norm_mix, jnp.swapaxes(conv_w, 0, 1), conv_b, norm_mlp, norm_ple,
              norm_f.reshape(1, D_MODEL), w_ple_proj)
    big_f32 = (w_in, w_up, w_down, w_ple_gate)

    w_mix = _fold_pool(pool_w, pool_scale, w_out)

    n_big = len(big_f32)
    h, pool_p, conv_p, *big_bf16 = _prompt_layer(0, x_prompt, p_prompt, shared + (w_mix,), (),
                                                 big_f32, big_f32)
    per_layer = [big_bf16[n_big:], big_bf16[:n_big]]
    y_prompt, pool_p, conv_p = _prompt_layer(1, h, p_prompt, _Weights(*shared, w_mix, *per_layer[1]),
                                             (pool_p, conv_p), ())

    y_sample, pool_s, conv_s = _sample_trunk(x_sample, p_sample, jnp.swapaxes(state_pool, 1, 2),
                                             state_conv, shared, w_mix, per_layer)
    return (y_prompt, y_sample, jnp.swapaxes(pool_p, 1, 2), conv_p, jnp.swapaxes(pool_s, 1, 2), conv_s)
```

```python
import functools
from typing import NamedTuple

import jax
import jax.numpy as jnp
from jax import lax
from jax.experimental import pallas as pl
from jax.experimental.pallas import tpu as pltpu

D_MODEL = 1024
DEPTH = 2
PAST_LEN = 16384
POOL_WIDTH = 512
CONV_WIDTH = 512
POOL_WINDOWS = (2, 4, 8, 16)
POOL_GC = 128
POOL_BUF = 15
CONV_BUF = 2
D_FF = 4 * D_MODEL
PLE_DIM = 256
IN_COLS = POOL_WIDTH + 3 * CONV_WIDTH
EPS = 1e-6

SUBLANES = 8
BF16_ROWS = 16
POOL_HALO = POOL_BUF + 1
TOKEN_TILE = 512
PROMPT_FF_CHUNK = 1024
SAMPLE_FF_CHUNK = 512
V7X_VMEM_BYTES = 64 * 1024 * 1024
VMEM_LIMIT_BYTES = V7X_VMEM_BYTES * 29 // 32

_BF16 = jnp.bfloat16
_F32 = jnp.float32


class _Weights(NamedTuple):
    norm_mix: jax.Array
    conv_w: jax.Array
    conv_b: jax.Array
    norm_mlp: jax.Array
    norm_ple: jax.Array
    norm_f: jax.Array
    w_proj: jax.Array
    w_mix: jax.Array
    w_in: jax.Array
    w_up: jax.Array
    w_down: jax.Array
    w_gate: jax.Array


_BIG_FIELDS = _Weights._fields[-4:]
_MIX_FIELD = _Weights._fields.index("w_mix")
CAST_STAGE = (512, 1024)
CAST_SLOTS = 4
CAST_LEAD = 2
assert PROMPT_FF_CHUNK % CAST_STAGE[0] == 0 and PROMPT_FF_CHUNK % CAST_STAGE[1] == 0


def _rmsnorm(x, g):
    ms = jnp.mean(x * x, axis=-1, keepdims=True)
    return (x * lax.rsqrt(ms + EPS)) * g


def _dot(a, b):
    return jnp.dot(a, b, preferred_element_type=_F32)


def _norm_split(x, g):
    scale = lax.rsqrt(jnp.mean(x * x, axis=-1, keepdims=True) + EPS)
    return (x * g).astype(_BF16), scale


def _row(vec, layer):
    if isinstance(layer, int):
        return vec[layer:layer + 1, :]
    row = vec[0:1, :]
    for k in range(1, DEPTH):
        row = jnp.where(layer == k, vec[k:k + 1, :], row)
    return row


def _in_proj(h, w, layer):
    hg, scale = _norm_split(h, _row(w.norm_mix[...], layer))
    proj = _dot(hg, w.w_in[...])
    u = proj[:, :POOL_WIDTH] * scale
    bg = proj[:, POOL_WIDTH:POOL_WIDTH + CONV_WIDTH] * scale
    cg = proj[:, POOL_WIDTH + CONV_WIDTH:POOL_WIDTH + 2 * CONV_WIDTH] * scale
    v = proj[:, POOL_WIDTH + 2 * CONV_WIDTH:] * scale
    return u, bg, cg * v


def _fold_kernel(pool_w_ref, pool_scale_ref, w_out_ref, w_mix_ref):
    def split(a):
        head = a.astype(_BF16)
        return head, (a - head.astype(_F32)).astype(_BF16)

    pooling_half = pl.program_id(1) == 0
    for layer in range(DEPTH):
        @pl.when(pooling_half & (pl.program_id(0) == layer))
        def _():
            for g in range(len(POOL_WINDOWS)):
                sl = slice(g * POOL_GC, (g + 1) * POOL_GC)
                a_head, a_rem = split(pool_w_ref[g] * pool_scale_ref[layer:layer + 1, sl])
                b_head, b_rem = split(w_out_ref[sl, :])
                w_mix_ref[sl, :] = _dot(jnp.concatenate([a_head, a_head, a_rem], axis=1),
                                        jnp.concatenate([b_head, b_rem, b_head], axis=0)).astype(_BF16)

    @pl.when(jnp.logical_not(pooling_half))
    def _():
        w_mix_ref[...] = w_out_ref[...].astype(_BF16)


def _fold_pool(pool_w, pool_scale, w_out):
    assert w_out.shape[1] == 2 * POOL_WIDTH
    half = (None, POOL_WIDTH, w_out.shape[2])
    return pl.pallas_call(
        _fold_kernel,
        grid=(DEPTH, 2),
        in_specs=[pl.BlockSpec((None,) + pool_w.shape[1:], lambda l, k: (l, 0, 0, 0)),
                  pl.BlockSpec(pool_scale.shape, lambda l, k: (0, 0)),
                  pl.BlockSpec(half, lambda l, k: (l, k, 0))],
        out_specs=pl.BlockSpec(half, lambda l, k: (l, k, 0)),
        out_shape=jax.ShapeDtypeStruct(w_out.shape, _BF16),
        compiler_params=pltpu.CompilerParams(dimension_semantics=("arbitrary", "arbitrary")),
        name="fold_pool_maps",
    )(pool_w, pool_scale, w_out)


def _conv_out(w, layer, back2, back1, cur):
    tap = lambda k: _row(w.conv_w[k], layer)
    return _row(w.conv_b[...], layer) + (tap(0) * back2 + tap(1) * back1 + tap(2) * cur)


def _out_proj(h, ds, y_conv, w):
    return h + _dot(jnp.concatenate(ds + [y_conv.astype(_BF16)], axis=-1), w.w_mix[...])


def _mlp(h, w, layer, chunk, need=lambda key: None):
    hg, scale = _norm_split(h, _row(w.norm_mlp[...], layer))
    acc = h
    for c in range(D_FF // chunk):
        need(c)
        cs = slice(c * chunk, (c + 1) * chunk)
        f = jnp.maximum(_dot(hg, w.w_up[:, cs]) * scale, 0.0)
        need("spare")
        acc = acc + _dot((f * f).astype(_BF16), w.w_down[cs, :])
    return acc


def _gated_embed(h, p, w, layer):
    hg, scale = _norm_split(h, _row(w.norm_ple[...], layer))
    gate = jax.nn.sigmoid(_dot(hg, w.w_gate[...]) * scale)
    return h + gate * _dot(p.astype(_BF16), w.w_proj[layer].astype(_BF16))


def _prompt_kernel(*refs, layer, tile, n_tiles, n_cast, own_cast):
    n_big = len(_BIG_FIELDS)
    n_w = len(_Weights._fields) - (n_big if own_cast else 0)
    x_ref, p_ref = refs[:2]
    pos = 2 + n_w
    prev = refs[pos:pos + 2 * layer]
    pos += 2 * layer
    cast_in = refs[pos:pos + n_cast]
    pos += n_cast
    if own_cast:
        big_hbm = refs[pos:pos + n_big]
        pos += n_big
    out_ref, npool_ref, nconv_ref = refs[pos:pos + 3]
    cast_out = refs[pos + 3:pos + 3 + n_cast]
    pos += 3 + n_cast
    if own_cast:
        big_out = refs[pos:pos + n_big]
        pool_ext, conv_ext = refs[pos + n_big:pos + n_big + 2]
        big_vmem = refs[pos + n_big + 2:pos + 2 * n_big + 2]
        stage, fetch_sem, publish_sem = refs[pos + 2 * n_big + 2:]
        w = _Weights(*refs[2:2 + n_w], *big_vmem)
    else:
        pool_ext, conv_ext = refs[pos:]
        w = _Weights(*refs[2:2 + n_w])
    i = pl.program_id(1)
    halo = POOL_HALO
    chalo = SUBLANES

    if own_cast:
        first = (pl.program_id(0) == 0) & (i == 0)
        last = (pl.program_id(0) == pl.num_programs(0) - 1) & (i == n_tiles - 1)
        slots, rows, cols = stage.shape
        part = lambda m, r, c: [(m, r0, c0) for r0 in range(r[0], r[1], rows)
                                for c0 in range(c[0], c[1], cols)]
        whole = lambda m: part(m, (0, big_vmem[m].shape[0]), (0, big_vmem[m].shape[1]))
        chunks, ready = whole(0), {}
        ready["in"] = len(chunks)
        for c in range(D_FF // PROMPT_FF_CHUNK):
            span = (c * PROMPT_FF_CHUNK, (c + 1) * PROMPT_FF_CHUNK)
            chunks += part(1, (0, D_MODEL), span) + part(2, span, (0, D_MODEL))
            ready[c] = len(chunks)
        chunks += whole(3)
        ready["gate"] = len(chunks)

        def fetch(k):
            m, r0, c0 = chunks[k]
            return pltpu.make_async_copy(big_hbm[m].at[layer, pl.ds(r0, rows), pl.ds(c0, cols)],
                                         stage.at[k % slots], fetch_sem.at[k % slots])

        def publish(m):
            return pltpu.make_async_copy(big_vmem[m], big_out[m], publish_sem.at[m])

        progress = {"cast": 0, "started": 0}

        def stream_until(key):
            target = max(ready.get(key, 0), min(len(chunks), progress["cast"] + CAST_LEAD))
            while progress["cast"] < target:
                k = progress["cast"]
                while progress["started"] < min(k + slots, len(chunks)):
                    fetch(progress["started"]).start()
                    progress["started"] += 1
                fetch(k).wait()
                m, r0, c0 = chunks[k]
                big_vmem[m][r0:r0 + rows, c0:c0 + cols] = stage[k % slots].astype(_BF16)
                progress["cast"] += 1

    def body(need):
        @pl.when(i == 0)
        def _():
            pool_ext[0:halo, :] = jnp.zeros((halo, POOL_WIDTH), _F32)
            conv_ext[0:chalo, :] = jnp.zeros((chalo, CONV_WIDTH), _F32)

        need("in")
        h = x_ref[...]
        u, bg, cv = _in_proj(h, w, layer)
        pool_ext[halo:halo + tile, :] = u
        conv_ext[chalo:chalo + tile, :] = cv
        need("spare")

        for src, dst in zip(cast_in, cast_out):
            dst[...] = src[...].astype(_BF16)

        position = i * tile + lax.broadcasted_iota(jnp.int32, (tile, 1), 0)
        ds = []
        for g, win in enumerate(POOL_WINDOWS):
            e = pool_ext[:, g * POOL_GC:(g + 1) * POOL_GC]
            s = e
            span = 1
            while span < win:
                s = s + pltpu.roll(s, span, axis=0)
                span *= 2
            cnt = jnp.minimum(win, position + 1).astype(_F32)
            ds.append((s[halo:, :] / cnt - e[halo:, :]).astype(_BF16))
        cvx = conv_ext[...]
        z = _conv_out(w, layer, pltpu.roll(cvx, 2, axis=0)[chalo:, :],
                      pltpu.roll(cvx, 1, axis=0)[chalo:, :], cvx[chalo:, :])

        h = _out_proj(h, ds, bg * z, w)
        need("spare")
        h = _mlp(h, w, layer, PROMPT_FF_CHUNK, need)
        need("gate")
        h = _gated_embed(h, p_ref[...], w, layer)
        out_ref[...] = _rmsnorm(h, w.norm_f[...]) if layer == DEPTH - 1 else h

        @pl.when(i == n_tiles - 1)
        def _():
            seq = pl.ds(pl.program_id(0), 1)
            new_pool = pool_ext[tile + halo - POOL_BUF:tile + halo, :][:, None, :]
            new_conv = conv_ext[tile + chalo - CONV_BUF:tile + chalo, :]
            if layer == 0:
                npool_ref[:, seq, :] = new_pool
                nconv_ref[...] = new_conv
            else:
                for k in range(layer):
                    npool_ref[k] = prev[2 * k][...]
                    nconv_ref[k] = prev[2 * k + 1][...]
                npool_ref[layer, :, seq, :] = new_pool
                nconv_ref[layer] = new_conv

        @pl.when(i < n_tiles - 1)
        def _():
            pool_ext[0:halo, :] = pool_ext[tile:tile + halo, :]
            conv_ext[0:chalo, :] = conv_ext[tile:tile + chalo, :]

    resident = lambda key: None
    if own_cast:
        @pl.when(first)
        def _():
            body(stream_until)
            for m in range(n_big):
                publish(m).start()

        @pl.when(jnp.logical_not(first))
        def _():
            body(resident)

        @pl.when(last)
        def _():
            for m in range(n_big):
                publish(m).wait()
    else:
        body(resident)


def _sample_kernel(*refs, tile, n_tiles):
    n_big = len(_BIG_FIELDS)
    x_ref, p_ref, pool_hist, conv_hist = refs[:4]
    shared = refs[4:4 + _MIX_FIELD]
    w_mix_hbm = refs[4 + _MIX_FIELD]
    pos = 5 + _MIX_FIELD
    per_layer = [refs[pos + k * n_big:pos + (k + 1) * n_big] for k in range(DEPTH)]
    pos += DEPTH * n_big
    out_ref, npool_ref, nconv_ref = refs[pos:pos + 3]
    buffers = refs[pos + 3:pos + 4 + n_big]
    h_mid, pool_ext, conv_ext, sem = refs[pos + 4 + n_big:]
    w_in_buf, w_mix_buf, w_up_buf, w_down_buf, w_gate_buf = buffers
    w = _Weights(*shared, w_mix=w_mix_buf, w_in=w_in_buf, w_up=w_up_buf, w_down=w_down_buf,
                 w_gate=w_gate_buf)
    early, late = (0, 1), (2, 3, 4)

    def load(k, m):
        w_in_k, w_up_k, w_down_k, w_gate_k = per_layer[k]
        source = (w_in_k, w_mix_hbm.at[k], w_up_k, w_down_k, w_gate_k)[m]
        return pltpu.make_async_copy(source, buffers[m], sem.at[m])

    step = pl.program_id(0)
    layer = step // (2 * n_tiles)
    mlp_half = (step // n_tiles) % 2 == 1
    tile_index = step % n_tiles
    seqs = tile // SUBLANES
    halo = POOL_HALO * seqs
    chalo = CONV_BUF * seqs
    positions = range(SUBLANES)
    by_position = lambda ref: jnp.concatenate([ref[:, t, :] for t in positions], axis=0)

    for k in range(DEPTH):
        @pl.when(step == 2 * k * n_tiles)
        def _():
            if k == 0:
                for m in early + late:
                    load(0, m).start()
            else:
                for m in late:
                    load(k, m).start()
            for m in early:
                load(k, m).wait()

        @pl.when(step == (2 * k + 1) * n_tiles)
        def _():
            for m in late:
                load(k, m).wait()
            if k + 1 < DEPTH:
                for m in early:
                    load(k + 1, m).start()

    @pl.when(jnp.logical_not(mlp_half))
    def _():
        @pl.when(layer == 0)
        def _():
            h_mid[tile_index] = by_position(x_ref)

        h = h_mid[tile_index]
        u, bg, cv = _in_proj(h, w, layer)

        pool_ext[0:seqs, :] = pool_hist[0]
        pool_ext[seqs:halo, :] = pool_hist[...].reshape(POOL_BUF * seqs, POOL_WIDTH)
        pool_ext[halo:halo + tile, :] = u
        for k in range(CONV_BUF):
            conv_ext[k * seqs:(k + 1) * seqs, :] = conv_hist[:, k, :]
        conv_ext[chalo:chalo + tile, :] = cv

        ds = []
        for g, win in enumerate(POOL_WINDOWS):
            sl = slice(g * POOL_GC, (g + 1) * POOL_GC)
            s = pool_ext[halo - (win - 1) * seqs:halo + tile, sl]
            span = 1
            while span < win:
                s = s[span * seqs:, :] + s[:-span * seqs, :]
                span *= 2
            ds.append((s * (1.0 / win) - u[:, sl]).astype(_BF16))
        z = _conv_out(w, layer, conv_ext[0:tile, :], conv_ext[seqs:seqs + tile, :], cv)

        h_mid[tile_index] = _out_proj(h, ds, bg * z, w)
        npool_ref[...] = pool_ext[halo + tile - POOL_BUF * seqs:halo + tile, :].reshape(
            POOL_BUF, seqs, POOL_WIDTH)
        for k in range(CONV_BUF):
            nconv_ref[:, k, :] = conv_ext[tile + k * seqs:tile + (k + 1) * seqs, :]

    @pl.when(mlp_half)
    def _():
        h = _mlp(h_mid[tile_index], w, layer, SAMPLE_FF_CHUNK)
        h = _gated_embed(h, by_position(p_ref), w, layer)

        @pl.when(layer < DEPTH - 1)
        def _():
            h_mid[tile_index] = h

        @pl.when(layer == DEPTH - 1)
        def _():
            y = _rmsnorm(h, w.norm_f[...])
            for t in positions:
                out_ref[:, t, :] = y[t * seqs:(t + 1) * seqs, :]


def _weight_specs(weights, layer):
    once = pl.Buffered(1)
    specs = [pl.BlockSpec(a.shape, lambda *_, nd=a.ndim: (0,) * nd, pipeline_mode=once)
             for a in weights]
    specs[_MIX_FIELD] = pl.BlockSpec((None,) + weights[_MIX_FIELD].shape[1:],
                                     lambda *_: (layer, 0, 0), pipeline_mode=once)
    return specs


def _prompt_layer(layer, x, p, weights, prev_state, cast_next, cast_own=()):
    batch, seq, _ = x.shape
    tile = TOKEN_TILE
    n_tiles = seq // tile
    steps = batch * n_tiles
    assert seq == n_tiles * tile and tile >= POOL_HALO and len(prev_state) == 2 * layer
    for a in cast_next:
        assert a.shape[1] % (steps * BF16_ROWS) == 0
    for a in cast_own:
        assert a.shape[1] % CAST_STAGE[0] == 0 and a.shape[2] % CAST_STAGE[1] == 0
    assert len(weights) + len(cast_own) == len(_Weights._fields)
    hbm = pl.BlockSpec(memory_space=pl.ANY)
    whole = lambda shape: pl.BlockSpec(shape, lambda b, i, nd=len(shape): (0,) * nd)
    per_batch = pl.BlockSpec((None, CONV_BUF, CONV_WIDTH), lambda b, i: (b, 0, 0))
    pool_shape = (POOL_BUF, batch, POOL_WIDTH)
    in_specs = [
        pl.BlockSpec((None, tile, D_MODEL), lambda b, i: (b, i, 0)),
        pl.BlockSpec((None, None, tile, PLE_DIM), lambda b, i: (layer, b, i, 0)),
    ] + _weight_specs(weights, layer) + [whole(pool_shape), per_batch] * layer + [
        pl.BlockSpec((None, a.shape[1] // steps, a.shape[2]),
                     lambda b, i: (layer + 1, b * n_tiles + i, 0)) for a in cast_next] + [
        hbm] * len(cast_own)
    out_specs = [pl.BlockSpec((None, tile, D_MODEL), lambda b, i: (b, i, 0))]
    out_shape = [jax.ShapeDtypeStruct((batch, seq, D_MODEL), _F32)]
    if layer == 0:
        out_specs += [whole(pool_shape), per_batch]
        out_shape += [jax.ShapeDtypeStruct(pool_shape, _F32),
                      jax.ShapeDtypeStruct((batch, CONV_BUF, CONV_WIDTH), _F32)]
    else:
        out_specs += [whole((layer + 1,) + pool_shape),
                      pl.BlockSpec((layer + 1, None, CONV_BUF, CONV_WIDTH), lambda b, i: (0, b, 0, 0))]
        out_shape += [jax.ShapeDtypeStruct((layer + 1,) + pool_shape, _F32),
                      jax.ShapeDtypeStruct((layer + 1, batch, CONV_BUF, CONV_WIDTH), _F32)]
    out_specs += [pl.BlockSpec((a.shape[1] // steps, a.shape[2]), lambda b, i: (b * n_tiles + i, 0))
                  for a in cast_next]
    out_shape += [jax.ShapeDtypeStruct(a.shape[1:], _BF16) for a in cast_next]
    scratch_shapes = [pltpu.VMEM((POOL_HALO + tile, POOL_WIDTH), _F32),
                      pltpu.VMEM((SUBLANES + tile, CONV_WIDTH), _F32)]
    if cast_own:
        out_specs += [hbm] * len(cast_own)
        out_shape += [jax.ShapeDtypeStruct(a.shape[1:], _BF16) for a in cast_own]
        scratch_shapes += [pltpu.VMEM(a.shape[1:], _BF16) for a in cast_own] + [
            pltpu.VMEM((CAST_SLOTS,) + CAST_STAGE, _F32),
            pltpu.SemaphoreType.DMA((CAST_SLOTS,)), pltpu.SemaphoreType.DMA((len(cast_own),))]
    return pl.pallas_call(
        functools.partial(_prompt_kernel, layer=layer, tile=tile, n_tiles=n_tiles,
                          n_cast=len(cast_next), own_cast=bool(cast_own)),
        grid=(batch, n_tiles),
        in_specs=in_specs,
        out_specs=out_specs,
        out_shape=out_shape,
        scratch_shapes=scratch_shapes,
        compiler_params=pltpu.CompilerParams(
            dimension_semantics=("arbitrary", "arbitrary"),
            vmem_limit_bytes=VMEM_LIMIT_BYTES),
        name=f"prompt_layer{layer}",
    )(x, p, *weights, *prev_state, *cast_next, *cast_own)


def _sample_trunk(x, p, state_pool, state_conv, shared, w_mix, per_layer):
    n_seq = x.shape[0]
    seqs = TOKEN_TILE // SUBLANES
    tile = seqs * SUBLANES
    n_tiles = n_seq // seqs
    assert n_seq == n_tiles * seqs and x.shape[1] == SUBLANES and len(shared) == _MIX_FIELD
    assert len(per_layer) == DEPTH and all(len(big) == len(_BIG_FIELDS) for big in per_layer)
    last = n_tiles - 1

    def at(mixing, mlp):
        def index(s):
            layer, tile_index = s // (2 * n_tiles), s % n_tiles
            in_mlp_half = (s // n_tiles) % 2 == 1
            a, b = mixing(layer, tile_index), mlp(layer, tile_index)
            return tuple(jnp.where(in_mlp_half, j, i) for i, j in zip(a, b))
        return index

    hbm = pl.BlockSpec(memory_space=pl.ANY)
    big = [a for layer_big in per_layer for a in layer_big]
    state_rows = at(lambda l, i: (l, 0, i, 0), lambda l, i: (l, 0, last, 0))
    conv_rows = at(lambda l, i: (l, i, 0, 0), lambda l, i: (l, last, 0, 0))
    in_specs = [
        pl.BlockSpec((seqs, SUBLANES, D_MODEL), lambda s: (jnp.minimum(s, last), 0, 0)),
        pl.BlockSpec((None, seqs, SUBLANES, PLE_DIM),
                     at(lambda l, i: (l, 0, 0, 0), lambda l, i: (l, i, 0, 0))),
        pl.BlockSpec((None, POOL_BUF, seqs, POOL_WIDTH), state_rows),
        pl.BlockSpec((None, seqs, CONV_BUF, CONV_WIDTH), conv_rows),
    ] + [pl.BlockSpec(a.shape, lambda s, nd=a.ndim: (0,) * nd, pipeline_mode=pl.Buffered(1))
         for a in shared] + [hbm] * (1 + len(big))
    out_specs = [
        pl.BlockSpec((seqs, SUBLANES, D_MODEL),
                     at(lambda l, i: (0, 0, 0),
                        lambda l, i: (jnp.where(l == DEPTH - 1, i, 0), 0, 0))),
        pl.BlockSpec((None, POOL_BUF, seqs, POOL_WIDTH), state_rows),
        pl.BlockSpec((None, seqs, CONV_BUF, CONV_WIDTH), conv_rows),
    ]
    out_shape = [jax.ShapeDtypeStruct((n_seq, SUBLANES, D_MODEL), _F32),
                 jax.ShapeDtypeStruct((DEPTH, POOL_BUF, n_seq, POOL_WIDTH), _F32),
                 jax.ShapeDtypeStruct((DEPTH, n_seq, CONV_BUF, CONV_WIDTH), _F32)]
    w_in, w_up, w_down, w_gate = per_layer[0]
    buffers = [w_in.shape, w_mix.shape[1:], w_up.shape, w_down.shape, w_gate.shape]
    return pl.pallas_call(
        functools.partial(_sample_kernel, tile=tile, n_tiles=n_tiles),
        grid=(2 * DEPTH * n_tiles,),
        in_specs=in_specs,
        out_specs=out_specs,
        out_shape=out_shape,
        scratch_shapes=[pltpu.VMEM(shape, _BF16) for shape in buffers] + [
            pltpu.VMEM((n_tiles, tile, D_MODEL), _F32),
            pltpu.VMEM(((POOL_HALO + SUBLANES) * seqs, POOL_WIDTH), _F32),
            pltpu.VMEM(((CONV_BUF + SUBLANES) * seqs, CONV_WIDTH), _F32),
            pltpu.SemaphoreType.DMA((len(buffers),))],
        compiler_params=pltpu.CompilerParams(
            dimension_semantics=("arbitrary",),
            vmem_limit_bytes=VMEM_LIMIT_BYTES),
        name="sample_trunk",
    )(x, p, state_pool, state_conv, *shared, w_mix, *big)


def kernel(x_prompt, x_sample, state_pool, state_conv, p_prompt, p_sample, norm_mix, w_in, pool_w, pool_scale, conv_w, conv_b, w_out, norm_mlp, w_up, w_down, norm_ple, w_ple_gate, w_ple_proj, norm_f):
    assert DEPTH == 2 and PAST_LEN >= POOL_BUF
    shared = (norm_mix, jnp.swapaxes(conv_w, 0, 1), conv_b, norm_mlp, norm_ple,
              norm_f.reshape(1, D_MODEL), w_ple_proj)
    big_f32 = (w_in, w_up, w_down, w_ple_gate)

    w_mix = _fold_pool(pool_w, pool_scale, w_out)

    n_big = len(big_f32)
    h, pool_p, conv_p, *big_bf16 = _prompt_layer(0, x_prompt, p_prompt, shared + (w_mix,), (),
                                                 big_f32, big_f32)
    per_layer = [big_bf16[n_big:], big_bf16[:n_big]]
    y_prompt, pool_p, conv_p = _prompt_layer(1, h, p_prompt, _Weights(*shared, w_mix, *per_layer[1]),
                                             (pool_p, conv_p), ())

    y_sample, pool_s, conv_s = _sample_trunk(x_sample, p_sample, jnp.swapaxes(state_pool, 1, 2),
                                             state_conv, shared, w_mix, per_layer)
    return (y_prompt, y_sample, jnp.swapaxes(pool_p, 1, 2), conv_p, jnp.swapaxes(pool_s, 1, 2), conv_s)
```

```python
import functools
from typing import NamedTuple

import jax
import jax.numpy as jnp
from jax import lax
from jax.experimental import pallas as pl
from jax.experimental.pallas import tpu as pltpu

D_MODEL = 1024
DEPTH = 2
PAST_LEN = 16384
POOL_WIDTH = 512
CONV_WIDTH = 512
POOL_WINDOWS = (2, 4, 8, 16)
POOL_GC = 128
POOL_BUF = 15
CONV_BUF = 2
D_FF = 4 * D_MODEL
PLE_DIM = 256
IN_COLS = POOL_WIDTH + 3 * CONV_WIDTH
EPS = 1e-6

SUBLANES = 8
BF16_ROWS = 16
POOL_HALO = POOL_BUF + 1
TOKEN_TILE = 512
PROMPT_FF_CHUNK = 1024
SAMPLE_FF_CHUNK = 512
V7X_VMEM_BYTES = 64 * 1024 * 1024
VMEM_LIMIT_BYTES = V7X_VMEM_BYTES * 29 // 32

_BF16 = jnp.bfloat16
_F32 = jnp.float32


class _Weights(NamedTuple):
    norm_mix: jax.Array
    conv_w: jax.Array
    conv_b: jax.Array
    norm_mlp: jax.Array
    norm_ple: jax.Array
    norm_f: jax.Array
    w_proj: jax.Array
    w_mix: jax.Array
    w_in: jax.Array
    w_up: jax.Array
    w_down: jax.Array
    w_gate: jax.Array


_BIG_FIELDS = _Weights._fields[-4:]
_MIX_FIELD = _Weights._fields.index("w_mix")
CAST_STAGE = (512, 1024)
CAST_SLOTS = 4
CAST_LEAD = 2
LOAD_PART = (1024, 1024)
assert PROMPT_FF_CHUNK % LOAD_PART[0] == 0 and PROMPT_FF_CHUNK % LOAD_PART[1] == 0
assert PROMPT_FF_CHUNK % CAST_STAGE[0] == 0 and PROMPT_FF_CHUNK % CAST_STAGE[1] == 0


def _rmsnorm(x, g):
    ms = jnp.mean(x * x, axis=-1, keepdims=True)
    return (x * lax.rsqrt(ms + EPS)) * g


def _dot(a, b):
    return jnp.dot(a, b, preferred_element_type=_F32)


def _norm_split(x, g):
    scale = lax.rsqrt(jnp.mean(x * x, axis=-1, keepdims=True) + EPS)
    return (x * g).astype(_BF16), scale


def _row(vec, layer):
    if isinstance(layer, int):
        return vec[layer:layer + 1, :]
    row = vec[0:1, :]
    for k in range(1, DEPTH):
        row = jnp.where(layer == k, vec[k:k + 1, :], row)
    return row


def _in_proj(h, w, layer):
    hg, scale = _norm_split(h, _row(w.norm_mix[...], layer))
    proj = _dot(hg, w.w_in[...])
    u = proj[:, :POOL_WIDTH] * scale
    bg = proj[:, POOL_WIDTH:POOL_WIDTH + CONV_WIDTH] * scale
    cg = proj[:, POOL_WIDTH + CONV_WIDTH:POOL_WIDTH + 2 * CONV_WIDTH] * scale
    v = proj[:, POOL_WIDTH + 2 * CONV_WIDTH:] * scale
    return u, bg, cg * v


def _fold_kernel(pool_w_ref, pool_scale_ref, w_out_ref, w_mix_ref):
    def split(a):
        head = a.astype(_BF16)
        return head, (a - head.astype(_F32)).astype(_BF16)

    pooling_half = pl.program_id(1) == 0
    for layer in range(DEPTH):
        @pl.when(pooling_half & (pl.program_id(0) == layer))
        def _():
            for g in range(len(POOL_WINDOWS)):
                sl = slice(g * POOL_GC, (g + 1) * POOL_GC)
                a_head, a_rem = split(pool_w_ref[g] * pool_scale_ref[layer:layer + 1, sl])
                b_head, b_rem = split(w_out_ref[sl, :])
                w_mix_ref[sl, :] = _dot(jnp.concatenate([a_head, a_head, a_rem], axis=1),
                                        jnp.concatenate([b_head, b_rem, b_head], axis=0)).astype(_BF16)

    @pl.when(jnp.logical_not(pooling_half))
    def _():
        w_mix_ref[...] = w_out_ref[...].astype(_BF16)


def _fold_pool(pool_w, pool_scale, w_out):
    assert w_out.shape[1] == 2 * POOL_WIDTH
    half = (None, POOL_WIDTH, w_out.shape[2])
    return pl.pallas_call(
        _fold_kernel,
        grid=(DEPTH, 2),
        in_specs=[pl.BlockSpec((None,) + pool_w.shape[1:], lambda l, k: (l, 0, 0, 0)),
                  pl.BlockSpec(pool_scale.shape, lambda l, k: (0, 0)),
                  pl.BlockSpec(half, lambda l, k: (l, k, 0))],
        out_specs=pl.BlockSpec(half, lambda l, k: (l, k, 0)),
        out_shape=jax.ShapeDtypeStruct(w_out.shape, _BF16),
        compiler_params=pltpu.CompilerParams(dimension_semantics=("arbitrary", "arbitrary")),
        name="fold_pool_maps",
    )(pool_w, pool_scale, w_out)


def _conv_out(w, layer, back2, back1, cur):
    tap = lambda k: _row(w.conv_w[k], layer)
    return _row(w.conv_b[...], layer) + (tap(0) * back2 + tap(1) * back1 + tap(2) * cur)


def _out_proj(h, ds, y_conv, w):
    return h + _dot(jnp.concatenate(ds + [y_conv.astype(_BF16)], axis=-1), w.w_mix[...])


def _mlp(h, w, layer, chunk, need=lambda key: None):
    hg, scale = _norm_split(h, _row(w.norm_mlp[...], layer))
    acc = h
    for c in range(D_FF // chunk):
        need(c)
        cs = slice(c * chunk, (c + 1) * chunk)
        f = jnp.maximum(_dot(hg, w.w_up[:, cs]) * scale, 0.0)
        need("spare")
        acc = acc + _dot((f * f).astype(_BF16), w.w_down[cs, :])
    return acc


def _gated_embed(h, p, w, layer):
    hg, scale = _norm_split(h, _row(w.norm_ple[...], layer))
    gate = jax.nn.sigmoid(_dot(hg, w.w_gate[...]) * scale)
    return h + gate * _dot(p.astype(_BF16), w.w_proj[layer].astype(_BF16))


def _prompt_kernel(*refs, layer, tile, n_tiles, n_cast, own_cast, own_load):
    n_big = len(_BIG_FIELDS)
    streams = own_cast or own_load
    n_w = len(_Weights._fields) - (n_big if streams else 0)
    x_ref, p_ref = refs[:2]
    pos = 2 + n_w
    prev = refs[pos:pos + 2 * layer]
    pos += 2 * layer
    cast_in = refs[pos:pos + n_cast]
    pos += n_cast
    if streams:
        big_hbm = refs[pos:pos + n_big]
        pos += n_big
    out_ref, npool_ref, nconv_ref = refs[pos:pos + 3]
    cast_out = refs[pos + 3:pos + 3 + n_cast]
    pos += 3 + n_cast
    if own_cast:
        big_out = refs[pos:pos + n_big]
        pos += n_big
    pool_ext, conv_ext = refs[pos:pos + 2]
    pos += 2
    if streams:
        big_vmem = refs[pos:pos + n_big]
        pos += n_big
        w = _Weights(*refs[2:2 + n_w], *big_vmem)
    else:
        w = _Weights(*refs[2:2 + n_w])
    if own_cast:
        stage, fetch_sem, publish_sem = refs[pos:]
    elif own_load:
        fetch_sem, = refs[pos:]
    i = pl.program_id(1)
    halo = POOL_HALO
    chalo = SUBLANES

    if streams:
        first = (pl.program_id(0) == 0) & (i == 0)
        last = (pl.program_id(0) == pl.num_programs(0) - 1) & (i == n_tiles - 1)
        rows, cols = stage.shape[1:] if own_cast else LOAD_PART
        part = lambda m, r, c: [(m, r0, c0) for r0 in range(r[0], r[1], rows)
                                for c0 in range(c[0], c[1], cols)]
        whole = lambda m: part(m, (0, big_vmem[m].shape[0]), (0, big_vmem[m].shape[1]))
        chunks, ready = whole(0), {}
        ready["in"] = len(chunks)
        for c in range(D_FF // PROMPT_FF_CHUNK):
            span = (c * PROMPT_FF_CHUNK, (c + 1) * PROMPT_FF_CHUNK)
            chunks += part(1, (0, D_MODEL), span) + part(2, span, (0, D_MODEL))
            ready[c] = len(chunks)
        chunks += whole(3)
        ready["gate"] = len(chunks)
        progress = {"cast": 0, "started": 0}

    if own_load:
        def fetch(k):
            m, r0, c0 = chunks[k]
            block = (pl.ds(r0, rows), pl.ds(c0, cols))
            return pltpu.make_async_copy(big_hbm[m].at[block], big_vmem[m].at[block], fetch_sem.at[k])

        def stream_until(key):
            while progress["started"] < (ready[0] if key == "in" else len(chunks)):
                fetch(progress["started"]).start()
                progress["started"] += 1
            while progress["cast"] < ready.get(key, 0):
                fetch(progress["cast"]).wait()
                progress["cast"] += 1

    if own_cast:
        slots = stage.shape[0]

        def fetch(k):
            m, r0, c0 = chunks[k]
            return pltpu.make_async_copy(big_hbm[m].at[layer, pl.ds(r0, rows), pl.ds(c0, cols)],
                                         stage.at[k % slots], fetch_sem.at[k % slots])

        def publish(m):
            return pltpu.make_async_copy(big_vmem[m], big_out[m], publish_sem.at[m])

        def stream_until(key):
            target = max(ready.get(key, 0), min(len(chunks), progress["cast"] + CAST_LEAD))
            while progress["cast"] < target:
                k = progress["cast"]
                while progress["started"] < min(k + slots, len(chunks)):
                    fetch(progress["started"]).start()
                    progress["started"] += 1
                fetch(k).wait()
                m, r0, c0 = chunks[k]
                big_vmem[m][r0:r0 + rows, c0:c0 + cols] = stage[k % slots].astype(_BF16)
                progress["cast"] += 1

    def body(need):
        @pl.when(i == 0)
        def _():
            pool_ext[0:halo, :] = jnp.zeros((halo, POOL_WIDTH), _F32)
            conv_ext[0:chalo, :] = jnp.zeros((chalo, CONV_WIDTH), _F32)

        need("in")
        h = x_ref[...]
        u, bg, cv = _in_proj(h, w, layer)
        pool_ext[halo:halo + tile, :] = u
        conv_ext[chalo:chalo + tile, :] = cv
        need("spare")

        for src, dst in zip(cast_in, cast_out):
            dst[...] = src[...].astype(_BF16)

        position = i * tile + lax.broadcasted_iota(jnp.int32, (tile, 1), 0)
        ds = []
        for g, win in enumerate(POOL_WINDOWS):
            e = pool_ext[:, g * POOL_GC:(g + 1) * POOL_GC]
            s = e
            span = 1
            while span < win:
                s = s + pltpu.roll(s, span, axis=0)
                span *= 2
            cnt = jnp.minimum(win, position + 1).astype(_F32)
            ds.append((s[halo:, :] / cnt - e[halo:, :]).astype(_BF16))
        cvx = conv_ext[...]
        z = _conv_out(w, layer, pltpu.roll(cvx, 2, axis=0)[chalo:, :],
                      pltpu.roll(cvx, 1, axis=0)[chalo:, :], cvx[chalo:, :])

        h = _out_proj(h, ds, bg * z, w)
        need("spare")
        h = _mlp(h, w, layer, PROMPT_FF_CHUNK, need)
        need("gate")
        h = _gated_embed(h, p_ref[...], w, layer)
        out_ref[...] = _rmsnorm(h, w.norm_f[...]) if layer == DEPTH - 1 else h

        @pl.when(i == n_tiles - 1)
        def _():
            seq = pl.ds(pl.program_id(0), 1)
            new_pool = pool_ext[tile + halo - POOL_BUF:tile + halo, :][:, None, :]
            new_conv = conv_ext[tile + chalo - CONV_BUF:tile + chalo, :]
            if layer == 0:
                npool_ref[:, seq, :] = new_pool
                nconv_ref[...] = new_conv
            else:
                for k in range(layer):
                    npool_ref[k] = prev[2 * k][...]
                    nconv_ref[k] = prev[2 * k + 1][...]
                npool_ref[layer, :, seq, :] = new_pool
                nconv_ref[layer] = new_conv

        @pl.when(i < n_tiles - 1)
        def _():
            pool_ext[0:halo, :] = pool_ext[tile:tile + halo, :]
            conv_ext[0:chalo, :] = conv_ext[tile:tile + chalo, :]

    resident = lambda key: None
    if streams:
        @pl.when(first)
        def _():
            body(stream_until)
            if own_cast:
                for m in range(n_big):
                    publish(m).start()

        @pl.when(jnp.logical_not(first))
        def _():
            body(resident)

        if own_cast:
            @pl.when(last)
            def _():
                for m in range(n_big):
                    publish(m).wait()
    else:
        body(resident)


def _sample_kernel(*refs, tile, n_tiles):
    n_big = len(_BIG_FIELDS)
    x_ref, p_ref, pool_hist, conv_hist = refs[:4]
    shared = refs[4:4 + _MIX_FIELD]
    w_mix_hbm = refs[4 + _MIX_FIELD]
    pos = 5 + _MIX_FIELD
    per_layer = [refs[pos + k * n_big:pos + (k + 1) * n_big] for k in range(DEPTH)]
    pos += DEPTH * n_big
    out_ref, npool_ref, nconv_ref = refs[pos:pos + 3]
    buffers = refs[pos + 3:pos + 4 + n_big]
    h_mid, pool_ext, conv_ext, sem = refs[pos + 4 + n_big:]
    w_in_buf, w_mix_buf, w_up_buf, w_down_buf, w_gate_buf = buffers
    w = _Weights(*shared, w_mix=w_mix_buf, w_in=w_in_buf, w_up=w_up_buf, w_down=w_down_buf,
                 w_gate=w_gate_buf)
    early, late = (0, 1), (2, 3, 4)

    def load(k, m):
        w_in_k, w_up_k, w_down_k, w_gate_k = per_layer[k]
        source = (w_in_k, w_mix_hbm.at[k], w_up_k, w_down_k, w_gate_k)[m]
        return pltpu.make_async_copy(source, buffers[m], sem.at[m])

    step = pl.program_id(0)
    layer = step // (2 * n_tiles)
    mlp_half = (step // n_tiles) % 2 == 1
    tile_index = step % n_tiles
    seqs = tile // SUBLANES
    halo = POOL_HALO * seqs
    chalo = CONV_BUF * seqs
    positions = range(SUBLANES)
    by_position = lambda ref: jnp.concatenate([ref[:, t, :] for t in positions], axis=0)

    for k in range(DEPTH):
        @pl.when(step == 2 * k * n_tiles)
        def _():
            if k == 0:
                for m in early + late:
                    load(0, m).start()
            else:
                for m in late:
                    load(k, m).start()
            for m in early:
                load(k, m).wait()

        @pl.when(step == (2 * k + 1) * n_tiles)
        def _():
            for m in late:
                load(k, m).wait()
            if k + 1 < DEPTH:
                for m in early:
                    load(k + 1, m).start()

    @pl.when(jnp.logical_not(mlp_half))
    def _():
        @pl.when(layer == 0)
        def _():
            h_mid[tile_index] = by_position(x_ref)

        h = h_mid[tile_index]
        u, bg, cv = _in_proj(h, w, layer)

        pool_ext[0:seqs, :] = pool_hist[0]
        pool_ext[seqs:halo, :] = pool_hist[...].reshape(POOL_BUF * seqs, POOL_WIDTH)
        pool_ext[halo:halo + tile, :] = u
        for k in range(CONV_BUF):
            conv_ext[k * seqs:(k + 1) * seqs, :] = conv_hist[:, k, :]
        conv_ext[chalo:chalo + tile, :] = cv

        ds = []
        for g, win in enumerate(POOL_WINDOWS):
            sl = slice(g * POOL_GC, (g + 1) * POOL_GC)
            s = pool_ext[halo - (win - 1) * seqs:halo + tile, sl]
            span = 1
            while span < win:
                s = s[span * seqs:, :] + s[:-span * seqs, :]
                span *= 2
            ds.append((s * (1.0 / win) - u[:, sl]).astype(_BF16))
        z = _conv_out(w, layer, conv_ext[0:tile, :], conv_ext[seqs:seqs + tile, :], cv)

        h_mid[tile_index] = _out_proj(h, ds, bg * z, w)
        npool_ref[...] = pool_ext[halo + tile - POOL_BUF * seqs:halo + tile, :].reshape(
            POOL_BUF, seqs, POOL_WIDTH)
        for k in range(CONV_BUF):
            nconv_ref[:, k, :] = conv_ext[tile + k * seqs:tile + (k + 1) * seqs, :]

    @pl.when(mlp_half)
    def _():
        h = _mlp(h_mid[tile_index], w, layer, SAMPLE_FF_CHUNK)
        h = _gated_embed(h, by_position(p_ref), w, layer)

        @pl.when(layer < DEPTH - 1)
        def _():
            h_mid[tile_index] = h

        @pl.when(layer == DEPTH - 1)
        def _():
            y = _rmsnorm(h, w.norm_f[...])
            for t in positions:
                out_ref[:, t, :] = y[t * seqs:(t + 1) * seqs, :]


def _weight_specs(weights, layer):
    once = pl.Buffered(1)
    specs = [pl.BlockSpec(a.shape, lambda *_, nd=a.ndim: (0,) * nd, pipeline_mode=once)
             for a in weights]
    specs[_MIX_FIELD] = pl.BlockSpec((None,) + weights[_MIX_FIELD].shape[1:],
                                     lambda *_: (layer, 0, 0), pipeline_mode=once)
    return specs


def _prompt_layer(layer, x, p, weights, prev_state, cast_next, cast_own=(), load_own=()):
    batch, seq, _ = x.shape
    tile = TOKEN_TILE
    n_tiles = seq // tile
    steps = batch * n_tiles
    assert seq == n_tiles * tile and tile >= POOL_HALO and len(prev_state) == 2 * layer
    for a in cast_next:
        assert a.shape[1] % (steps * BF16_ROWS) == 0
    for a in cast_own:
        assert a.shape[1] % CAST_STAGE[0] == 0 and a.shape[2] % CAST_STAGE[1] == 0
    for a in load_own:
        assert a.shape[0] % LOAD_PART[0] == 0 and a.shape[1] % LOAD_PART[1] == 0
    assert len(weights) + len(cast_own) + len(load_own) == len(_Weights._fields)
    hbm = pl.BlockSpec(memory_space=pl.ANY)
    whole = lambda shape: pl.BlockSpec(shape, lambda b, i, nd=len(shape): (0,) * nd)
    per_batch = pl.BlockSpec((None, CONV_BUF, CONV_WIDTH), lambda b, i: (b, 0, 0))
    pool_shape = (POOL_BUF, batch, POOL_WIDTH)
    in_specs = [
        pl.BlockSpec((None, tile, D_MODEL), lambda b, i: (b, i, 0)),
        pl.BlockSpec((None, None, tile, PLE_DIM), lambda b, i: (layer, b, i, 0)),
    ] + _weight_specs(weights, layer) + [whole(pool_shape), per_batch] * layer + [
        pl.BlockSpec((None, a.shape[1] // steps, a.shape[2]),
                     lambda b, i: (layer + 1, b * n_tiles + i, 0)) for a in cast_next] + [
        hbm] * (len(cast_own) + len(load_own))
    out_specs = [pl.BlockSpec((None, tile, D_MODEL), lambda b, i: (b, i, 0))]
    out_shape = [jax.ShapeDtypeStruct((batch, seq, D_MODEL), _F32)]
    if layer == 0:
        out_specs += [whole(pool_shape), per_batch]
        out_shape += [jax.ShapeDtypeStruct(pool_shape, _F32),
                      jax.ShapeDtypeStruct((batch, CONV_BUF, CONV_WIDTH), _F32)]
    else:
        out_specs += [whole((layer + 1,) + pool_shape),
                      pl.BlockSpec((layer + 1, None, CONV_BUF, CONV_WIDTH), lambda b, i: (0, b, 0, 0))]
        out_shape += [jax.ShapeDtypeStruct((layer + 1,) + pool_shape, _F32),
                      jax.ShapeDtypeStruct((layer + 1, batch, CONV_BUF, CONV_WIDTH), _F32)]
    out_specs += [pl.BlockSpec((a.shape[1] // steps, a.shape[2]), lambda b, i: (b * n_tiles + i, 0))
                  for a in cast_next]
    out_shape += [jax.ShapeDtypeStruct(a.shape[1:], _BF16) for a in cast_next]
    scratch_shapes = [pltpu.VMEM((POOL_HALO + tile, POOL_WIDTH), _F32),
                      pltpu.VMEM((SUBLANES + tile, CONV_WIDTH), _F32)]
    if cast_own:
        out_specs += [hbm] * len(cast_own)
        out_shape += [jax.ShapeDtypeStruct(a.shape[1:], _BF16) for a in cast_own]
        scratch_shapes += [pltpu.VMEM(a.shape[1:], _BF16) for a in cast_own] + [
            pltpu.VMEM((CAST_SLOTS,) + CAST_STAGE, _F32),
            pltpu.SemaphoreType.DMA((CAST_SLOTS,)), pltpu.SemaphoreType.DMA((len(cast_own),))]
    if load_own:
        n_parts = sum(a.shape[0] // LOAD_PART[0] * (a.shape[1] // LOAD_PART[1]) for a in load_own)
        scratch_shapes += [pltpu.VMEM(a.shape, _BF16) for a in load_own] + [
            pltpu.SemaphoreType.DMA((n_parts,))]
    return pl.pallas_call(
        functools.partial(_prompt_kernel, layer=layer, tile=tile, n_tiles=n_tiles,
                          n_cast=len(cast_next), own_cast=bool(cast_own), own_load=bool(load_own)),
        grid=(batch, n_tiles),
        in_specs=in_specs,
        out_specs=out_specs,
        out_shape=out_shape,
        scratch_shapes=scratch_shapes,
        compiler_params=pltpu.CompilerParams(
            dimension_semantics=("arbitrary", "arbitrary"),
            vmem_limit_bytes=VMEM_LIMIT_BYTES),
        name=f"prompt_layer{layer}",
    )(x, p, *weights, *prev_state, *cast_next, *cast_own, *load_own)


def _sample_trunk(x, p, state_pool, state_conv, shared, w_mix, per_layer):
    n_seq = x.shape[0]
    seqs = TOKEN_TILE // SUBLANES
    tile = seqs * SUBLANES
    n_tiles = n_seq // seqs
    assert n_seq == n_tiles * seqs and x.shape[1] == SUBLANES and len(shared) == _MIX_FIELD
    assert len(per_layer) == DEPTH and all(len(big) == len(_BIG_FIELDS) for big in per_layer)
    last = n_tiles - 1

    def at(mixing, mlp):
        def index(s):
            layer, tile_index = s // (2 * n_tiles), s % n_tiles
            in_mlp_half = (s // n_tiles) % 2 == 1
            a, b = mixing(layer, tile_index), mlp(layer, tile_index)
            return tuple(jnp.where(in_mlp_half, j, i) for i, j in zip(a, b))
        return index

    hbm = pl.BlockSpec(memory_space=pl.ANY)
    big = [a for layer_big in per_layer for a in layer_big]
    state_rows = at(lambda l, i: (l, 0, i, 0), lambda l, i: (l, 0, last, 0))
    conv_rows = at(lambda l, i: (l, i, 0, 0), lambda l, i: (l, last, 0, 0))
    in_specs = [
        pl.BlockSpec((seqs, SUBLANES, D_MODEL), lambda s: (jnp.minimum(s, last), 0, 0)),
        pl.BlockSpec((None, seqs, SUBLANES, PLE_DIM),
                     at(lambda l, i: (l, 0, 0, 0), lambda l, i: (l, i, 0, 0))),
        pl.BlockSpec((None, POOL_BUF, seqs, POOL_WIDTH), state_rows),
        pl.BlockSpec((None, seqs, CONV_BUF, CONV_WIDTH), conv_rows),
    ] + [pl.BlockSpec(a.shape, lambda s, nd=a.ndim: (0,) * nd, pipeline_mode=pl.Buffered(1))
         for a in shared] + [hbm] * (1 + len(big))
    out_specs = [
        pl.BlockSpec((seqs, SUBLANES, D_MODEL),
                     at(lambda l, i: (0, 0, 0),
                        lambda l, i: (jnp.where(l == DEPTH - 1, i, 0), 0, 0))),
        pl.BlockSpec((None, POOL_BUF, seqs, POOL_WIDTH), state_rows),
        pl.BlockSpec((None, seqs, CONV_BUF, CONV_WIDTH), conv_rows),
    ]
    out_shape = [jax.ShapeDtypeStruct((n_seq, SUBLANES, D_MODEL), _F32),
                 jax.ShapeDtypeStruct((DEPTH, POOL_BUF, n_seq, POOL_WIDTH), _F32),
                 jax.ShapeDtypeStruct((DEPTH, n_seq, CONV_BUF, CONV_WIDTH), _F32)]
    w_in, w_up, w_down, w_gate = per_layer[0]
    buffers = [w_in.shape, w_mix.shape[1:], w_up.shape, w_down.shape, w_gate.shape]
    return pl.pallas_call(
        functools.partial(_sample_kernel, tile=tile, n_tiles=n_tiles),
        grid=(2 * DEPTH * n_tiles,),
        in_specs=in_specs,
        out_specs=out_specs,
        out_shape=out_shape,
        scratch_shapes=[pltpu.VMEM(shape, _BF16) for shape in buffers] + [
            pltpu.VMEM((n_tiles, tile, D_MODEL), _F32),
            pltpu.VMEM(((POOL_HALO + SUBLANES) * seqs, POOL_WIDTH), _F32),
            pltpu.VMEM(((CONV_BUF + SUBLANES) * seqs, CONV_WIDTH), _F32),
            pltpu.SemaphoreType.DMA((len(buffers),))],
        compiler_params=pltpu.CompilerParams(
            dimension_semantics=("arbitrary",),
            vmem_limit_bytes=VMEM_LIMIT_BYTES),
        name="sample_trunk",
    )(x, p, state_pool, state_conv, *shared, w_mix, *big)


def kernel(x_prompt, x_sample, state_pool, state_conv, p_prompt, p_sample, norm_mix, w_in, pool_w, pool_scale, conv_w, conv_b, w_out, norm_mlp, w_up, w_down, norm_ple, w_ple_gate, w_ple_proj, norm_f):
    assert DEPTH == 2 and PAST_LEN >= POOL_BUF
    shared = (norm_mix, jnp.swapaxes(conv_w, 0, 1), conv_b, norm_mlp, norm_ple,
              norm_f.reshape(1, D_MODEL), w_ple_proj)
    big_f32 = (w_in, w_up, w_down, w_ple_gate)

    w_mix = _fold_pool(pool_w, pool_scale, w_out)

    n_big = len(big_f32)
    h, pool_p, conv_p, *big_bf16 = _prompt_layer(0, x_prompt, p_prompt, shared + (w_mix,), (),
                                                 big_f32, big_f32)
    per_layer = [big_bf16[n_big:], big_bf16[:n_big]]
    y_prompt, pool_p, conv_p = _prompt_layer(1, h, p_prompt, shared + (w_mix,), (pool_p, conv_p), (),
                                             load_own=per_layer[1])

    y_sample, pool_s, conv_s = _sample_trunk(x_sample, p_sample, jnp.swapaxes(state_pool, 1, 2),
                                             state_conv, shared, w_mix, per_layer)
    return (y_prompt, y_sample, jnp.swapaxes(pool_p, 1, 2), conv_p, jnp.swapaxes(pool_s, 1, 2), conv_s)
```

```python
import functools
from typing import NamedTuple

import jax
import jax.numpy as jnp
from jax import lax
from jax.experimental import pallas as pl
from jax.experimental.pallas import tpu as pltpu

D_MODEL = 1024
DEPTH = 2
PAST_LEN = 16384
POOL_WIDTH = 512
CONV_WIDTH = 512
POOL_WINDOWS = (2, 4, 8, 16)
POOL_GC = 128
POOL_BUF = 15
CONV_BUF = 2
D_FF = 4 * D_MODEL
PLE_DIM = 256
IN_COLS = POOL_WIDTH + 3 * CONV_WIDTH
EPS = 1e-6

SUBLANES = 8
BF16_ROWS = 16
POOL_HALO = POOL_BUF + 1
TOKEN_TILE = 512
PROMPT_FF_CHUNK = 1024
SAMPLE_FF_CHUNK = 512
V7X_VMEM_BYTES = 64 * 1024 * 1024
VMEM_LIMIT_BYTES = V7X_VMEM_BYTES * 29 // 32

_BF16 = jnp.bfloat16
_F32 = jnp.float32


class _Weights(NamedTuple):
    norm_mix: jax.Array
    conv_w: jax.Array
    conv_b: jax.Array
    norm_mlp: jax.Array
    norm_ple: jax.Array
    norm_f: jax.Array
    w_proj: jax.Array
    w_mix: jax.Array
    w_in: jax.Array
    w_up: jax.Array
    w_down: jax.Array
    w_gate: jax.Array


_BIG_FIELDS = _Weights._fields[-4:]
_MIX_FIELD = _Weights._fields.index("w_mix")
CAST_STAGE = (512, 1024)
CAST_SLOTS = 3
CAST_LEAD = 2
assert PROMPT_FF_CHUNK % CAST_STAGE[0] == 0 and PROMPT_FF_CHUNK % CAST_STAGE[1] == 0


def _rmsnorm(x, g):
    ms = jnp.mean(x * x, axis=-1, keepdims=True)
    return (x * lax.rsqrt(ms + EPS)) * g


def _dot(a, b):
    return jnp.dot(a, b, preferred_element_type=_F32)


def _norm_split(x, g):
    scale = lax.rsqrt(jnp.mean(x * x, axis=-1, keepdims=True) + EPS)
    return (x * g).astype(_BF16), scale


def _row(vec, layer):
    if isinstance(layer, int):
        return vec[layer:layer + 1, :]
    row = vec[0:1, :]
    for k in range(1, DEPTH):
        row = jnp.where(layer == k, vec[k:k + 1, :], row)
    return row


def _in_proj(h, w, layer):
    hg, scale = _norm_split(h, _row(w.norm_mix[...], layer))
    proj = _dot(hg, w.w_in[...])
    u = proj[:, :POOL_WIDTH] * scale
    bg = proj[:, POOL_WIDTH:POOL_WIDTH + CONV_WIDTH] * scale
    cg = proj[:, POOL_WIDTH + CONV_WIDTH:POOL_WIDTH + 2 * CONV_WIDTH] * scale
    v = proj[:, POOL_WIDTH + 2 * CONV_WIDTH:] * scale
    return u, bg, cg * v


def _fold_kernel(pool_w_ref, pool_scale_ref, w_out_ref, w_mix_ref):
    def split(a):
        head = a.astype(_BF16)
        return head, (a - head.astype(_F32)).astype(_BF16)

    pooling_half = pl.program_id(1) == 0
    for layer in range(DEPTH):
        @pl.when(pooling_half & (pl.program_id(0) == layer))
        def _():
            for g in range(len(POOL_WINDOWS)):
                sl = slice(g * POOL_GC, (g + 1) * POOL_GC)
                a_head, a_rem = split(pool_w_ref[g] * pool_scale_ref[layer:layer + 1, sl])
                b_head, b_rem = split(w_out_ref[sl, :])
                w_mix_ref[sl, :] = _dot(jnp.concatenate([a_head, a_head, a_rem], axis=1),
                                        jnp.concatenate([b_head, b_rem, b_head], axis=0)).astype(_BF16)

    @pl.when(jnp.logical_not(pooling_half))
    def _():
        w_mix_ref[...] = w_out_ref[...].astype(_BF16)


def _fold_pool(pool_w, pool_scale, w_out):
    assert w_out.shape[1] == 2 * POOL_WIDTH
    half = (None, POOL_WIDTH, w_out.shape[2])
    return pl.pallas_call(
        _fold_kernel,
        grid=(DEPTH, 2),
        in_specs=[pl.BlockSpec((None,) + pool_w.shape[1:], lambda l, k: (l, 0, 0, 0)),
                  pl.BlockSpec(pool_scale.shape, lambda l, k: (0, 0)),
                  pl.BlockSpec(half, lambda l, k: (l, k, 0))],
        out_specs=pl.BlockSpec(half, lambda l, k: (l, k, 0)),
        out_shape=jax.ShapeDtypeStruct(w_out.shape, _BF16),
        compiler_params=pltpu.CompilerParams(dimension_semantics=("arbitrary", "arbitrary")),
        name="fold_pool_maps",
    )(pool_w, pool_scale, w_out)


def _conv_out(w, layer, back2, back1, cur):
    tap = lambda k: _row(w.conv_w[k], layer)
    return _row(w.conv_b[...], layer) + (tap(0) * back2 + tap(1) * back1 + tap(2) * cur)


def _out_proj(h, ds, y_conv, w):
    return h + _dot(jnp.concatenate(ds + [y_conv.astype(_BF16)], axis=-1), w.w_mix[...])


def _mlp(h, w, layer, chunk, need=lambda key: None):
    hg, scale = _norm_split(h, _row(w.norm_mlp[...], layer))
    acc = h
    for c in range(D_FF // chunk):
        need(c)
        cs = slice(c * chunk, (c + 1) * chunk)
        f = jnp.maximum(_dot(hg, w.w_up[:, cs]) * scale, 0.0)
        need("spare")
        acc = acc + _dot((f * f).astype(_BF16), w.w_down[cs, :])
    return acc


def _gated_embed(h, p, w, layer):
    hg, scale = _norm_split(h, _row(w.norm_ple[...], layer))
    gate = jax.nn.sigmoid(_dot(hg, w.w_gate[...]) * scale)
    return h + gate * _dot(p.astype(_BF16), w.w_proj[layer].astype(_BF16))


def _prompt_kernel(*refs, layer, tile, n_tiles, n_cast, own_cast):
    n_big = len(_BIG_FIELDS)
    n_w = len(_Weights._fields) - (n_big if own_cast else 0)
    x_ref, p_ref = refs[:2]
    pos = 2 + n_w
    prev = refs[pos:pos + 2 * layer]
    pos += 2 * layer
    cast_in = refs[pos:pos + n_cast]
    pos += n_cast
    if own_cast:
        big_hbm = refs[pos:pos + n_big]
        pos += n_big
    out_ref, npool_ref, nconv_ref = refs[pos:pos + 3]
    cast_out = refs[pos + 3:pos + 3 + n_cast]
    pos += 3 + n_cast
    if own_cast:
        big_out = refs[pos:pos + n_big]
        pool_ext, conv_ext = refs[pos + n_big:pos + n_big + 2]
        big_vmem = refs[pos + n_big + 2:pos + 2 * n_big + 2]
        stage, fetch_sem, publish_sem = refs[pos + 2 * n_big + 2:]
        w = _Weights(*refs[2:2 + n_w], *big_vmem)
    else:
        pool_ext, conv_ext = refs[pos:]
        w = _Weights(*refs[2:2 + n_w])
    i = pl.program_id(1)
    halo = POOL_HALO
    chalo = SUBLANES

    if own_cast:
        first = (pl.program_id(0) == 0) & (i == 0)
        last = (pl.program_id(0) == pl.num_programs(0) - 1) & (i == n_tiles - 1)
        slots, rows, cols = stage.shape
        part = lambda m, r, c: [(m, r0, c0) for r0 in range(r[0], r[1], rows)
                                for c0 in range(c[0], c[1], cols)]
        whole = lambda m: part(m, (0, big_vmem[m].shape[0]), (0, big_vmem[m].shape[1]))
        chunks, ready = whole(0), {}
        ready["in"] = len(chunks)
        for c in range(D_FF // PROMPT_FF_CHUNK):
            span = (c * PROMPT_FF_CHUNK, (c + 1) * PROMPT_FF_CHUNK)
            chunks += part(1, (0, D_MODEL), span) + part(2, span, (0, D_MODEL))
            ready[c] = len(chunks)
        chunks += whole(3)
        ready["gate"] = len(chunks)

        def fetch(k):
            m, r0, c0 = chunks[k]
            return pltpu.make_async_copy(big_hbm[m].at[layer, pl.ds(r0, rows), pl.ds(c0, cols)],
                                         stage.at[k % slots], fetch_sem.at[k % slots])

        def publish(m):
            return pltpu.make_async_copy(big_vmem[m], big_out[m], publish_sem.at[m])

        progress = {"cast": 0, "started": 0}

        def stream_until(key):
            target = max(ready.get(key, 0), min(len(chunks), progress["cast"] + CAST_LEAD))
            while progress["cast"] < target:
                k = progress["cast"]
                while progress["started"] < min(k + slots, len(chunks)):
                    fetch(progress["started"]).start()
                    progress["started"] += 1
                fetch(k).wait()
                m, r0, c0 = chunks[k]
                big_vmem[m][r0:r0 + rows, c0:c0 + cols] = stage[k % slots].astype(_BF16)
                progress["cast"] += 1

    def body(need):
        @pl.when(i == 0)
        def _():
            pool_ext[0:halo, :] = jnp.zeros((halo, POOL_WIDTH), _F32)
            conv_ext[0:chalo, :] = jnp.zeros((chalo, CONV_WIDTH), _F32)

        need("in")
        h = x_ref[...]
        u, bg, cv = _in_proj(h, w, layer)
        pool_ext[halo:halo + tile, :] = u
        conv_ext[chalo:chalo + tile, :] = cv
        need("spare")

        for src, dst in zip(cast_in, cast_out):
            dst[...] = src[...].astype(_BF16)

        position = i * tile + lax.broadcasted_iota(jnp.int32, (tile, 1), 0)
        ds = []
        for g, win in enumerate(POOL_WINDOWS):
            e = pool_ext[:, g * POOL_GC:(g + 1) * POOL_GC]
            s = e
            span = 1
            while span < win:
                s = s + pltpu.roll(s, span, axis=0)
                span *= 2
            cnt = jnp.minimum(win, position + 1).astype(_F32)
            ds.append((s[halo:, :] / cnt - e[halo:, :]).astype(_BF16))
        cvx = conv_ext[...]
        z = _conv_out(w, layer, pltpu.roll(cvx, 2, axis=0)[chalo:, :],
                      pltpu.roll(cvx, 1, axis=0)[chalo:, :], cvx[chalo:, :])

        h = _out_proj(h, ds, bg * z, w)
        need("spare")
        h = _mlp(h, w, layer, PROMPT_FF_CHUNK, need)
        need("gate")
        h = _gated_embed(h, p_ref[...], w, layer)
        out_ref[...] = _rmsnorm(h, w.norm_f[...]) if layer == DEPTH - 1 else h

        @pl.when(i == n_tiles - 1)
        def _():
            seq = pl.ds(pl.program_id(0), 1)
            new_pool = pool_ext[tile + halo - POOL_BUF:tile + halo, :][:, None, :]
            new_conv = conv_ext[tile + chalo - CONV_BUF:tile + chalo, :]
            if layer == 0:
                npool_ref[:, seq, :] = new_pool
                nconv_ref[...] = new_conv
            else:
                for k in range(layer):
                    npool_ref[k] = prev[2 * k][...]
                    nconv_ref[k] = prev[2 * k + 1][...]
                npool_ref[layer, :, seq, :] = new_pool
                nconv_ref[layer] = new_conv

        @pl.when(i < n_tiles - 1)
        def _():
            pool_ext[0:halo, :] = pool_ext[tile:tile + halo, :]
            conv_ext[0:chalo, :] = conv_ext[tile:tile + chalo, :]

    resident = lambda key: None
    if own_cast:
        @pl.when(first)
        def _():
            body(stream_until)
            for m in range(n_big):
                publish(m).start()

        @pl.when(jnp.logical_not(first))
        def _():
            body(resident)

        @pl.when(last)
        def _():
            for m in range(n_big):
                publish(m).wait()
    else:
        body(resident)


def _sample_kernel(*refs, tile, n_tiles):
    n_big = len(_BIG_FIELDS)
    x_ref, p_ref, pool_hist, conv_hist = refs[:4]
    shared = refs[4:4 + _MIX_FIELD]
    w_mix_hbm = refs[4 + _MIX_FIELD]
    pos = 5 + _MIX_FIELD
    per_layer = [refs[pos + k * n_big:pos + (k + 1) * n_big] for k in range(DEPTH)]
    pos += DEPTH * n_big
    out_ref, npool_ref, nconv_ref = refs[pos:pos + 3]
    buffers = refs[pos + 3:pos + 4 + n_big]
    h_mid, pool_ext, conv_ext, sem = refs[pos + 4 + n_big:]
    w_in_buf, w_mix_buf, w_up_buf, w_down_buf, w_gate_buf = buffers
    w = _Weights(*shared, w_mix=w_mix_buf, w_in=w_in_buf, w_up=w_up_buf, w_down=w_down_buf,
                 w_gate=w_gate_buf)
    early, late = (0, 1), (2, 3, 4)

    def load(k, m):
        w_in_k, w_up_k, w_down_k, w_gate_k = per_layer[k]
        source = (w_in_k, w_mix_hbm.at[k], w_up_k, w_down_k, w_gate_k)[m]
        return pltpu.make_async_copy(source, buffers[m], sem.at[m])

    step = pl.program_id(0)
    layer = step // (2 * n_tiles)
    mlp_half = (step // n_tiles) % 2 == 1
    tile_index = step % n_tiles
    seqs = tile // SUBLANES
    halo = POOL_HALO * seqs
    chalo = CONV_BUF * seqs
    positions = range(SUBLANES)
    by_position = lambda ref: jnp.concatenate([ref[:, t, :] for t in positions], axis=0)

    for k in range(DEPTH):
        @pl.when(step == 2 * k * n_tiles)
        def _():
            if k == 0:
                for m in early + late:
                    load(0, m).start()
            else:
                for m in late:
                    load(k, m).start()
            for m in early:
                load(k, m).wait()

        @pl.when(step == (2 * k + 1) * n_tiles)
        def _():
            for m in late:
                load(k, m).wait()
            if k + 1 < DEPTH:
                for m in early:
                    load(k + 1, m).start()

    @pl.when(jnp.logical_not(mlp_half))
    def _():
        @pl.when(layer == 0)
        def _():
            h_mid[tile_index] = by_position(x_ref)

        h = h_mid[tile_index]
        u, bg, cv = _in_proj(h, w, layer)

        pool_ext[0:seqs, :] = pool_hist[0]
        pool_ext[seqs:halo, :] = pool_hist[...].reshape(POOL_BUF * seqs, POOL_WIDTH)
        pool_ext[halo:halo + tile, :] = u
        for k in range(CONV_BUF):
            conv_ext[k * seqs:(k + 1) * seqs, :] = conv_hist[:, k, :]
        conv_ext[chalo:chalo + tile, :] = cv

        ds = []
        for g, win in enumerate(POOL_WINDOWS):
            sl = slice(g * POOL_GC, (g + 1) * POOL_GC)
            s = pool_ext[halo - (win - 1) * seqs:halo + tile, sl]
            span = 1
            while span < win:
                s = s[span * seqs:, :] + s[:-span * seqs, :]
                span *= 2
            ds.append((s * (1.0 / win) - u[:, sl]).astype(_BF16))
        z = _conv_out(w, layer, conv_ext[0:tile, :], conv_ext[seqs:seqs + tile, :], cv)

        h_mid[tile_index] = _out_proj(h, ds, bg * z, w)
        npool_ref[...] = pool_ext[halo + tile - POOL_BUF * seqs:halo + tile, :].reshape(
            POOL_BUF, seqs, POOL_WIDTH)
        for k in range(CONV_BUF):
            nconv_ref[:, k, :] = conv_ext[tile + k * seqs:tile + (k + 1) * seqs, :]

    @pl.when(mlp_half)
    def _():
        h = _mlp(h_mid[tile_index], w, layer, SAMPLE_FF_CHUNK)
        h = _gated_embed(h, by_position(p_ref), w, layer)

        @pl.when(layer < DEPTH - 1)
        def _():
            h_mid[tile_index] = h

        @pl.when(layer == DEPTH - 1)
        def _():
            y = _rmsnorm(h, w.norm_f[...])
            for t in positions:
                out_ref[:, t, :] = y[t * seqs:(t + 1) * seqs, :]


def _weight_specs(weights, layer):
    once = pl.Buffered(1)
    specs = [pl.BlockSpec(a.shape, lambda *_, nd=a.ndim: (0,) * nd, pipeline_mode=once)
             for a in weights]
    specs[_MIX_FIELD] = pl.BlockSpec((None,) + weights[_MIX_FIELD].shape[1:],
                                     lambda *_: (layer, 0, 0), pipeline_mode=once)
    return specs


def _prompt_layer(layer, x, p, weights, prev_state, cast_next, cast_own=()):
    batch, seq, _ = x.shape
    tile = TOKEN_TILE
    n_tiles = seq // tile
    steps = batch * n_tiles
    assert seq == n_tiles * tile and tile >= POOL_HALO and len(prev_state) == 2 * layer
    for a in cast_next:
        assert a.shape[1] % (steps * BF16_ROWS) == 0
    for a in cast_own:
        assert a.shape[1] % CAST_STAGE[0] == 0 and a.shape[2] % CAST_STAGE[1] == 0
    assert len(weights) + len(cast_own) == len(_Weights._fields)
    hbm = pl.BlockSpec(memory_space=pl.ANY)
    whole = lambda shape: pl.BlockSpec(shape, lambda b, i, nd=len(shape): (0,) * nd)
    per_batch = pl.BlockSpec((None, CONV_BUF, CONV_WIDTH), lambda b, i: (b, 0, 0))
    pool_shape = (POOL_BUF, batch, POOL_WIDTH)
    in_specs = [
        pl.BlockSpec((None, tile, D_MODEL), lambda b, i: (b, i, 0)),
        pl.BlockSpec((None, None, tile, PLE_DIM), lambda b, i: (layer, b, i, 0)),
    ] + _weight_specs(weights, layer) + [whole(pool_shape), per_batch] * layer + [
        pl.BlockSpec((None, a.shape[1] // steps, a.shape[2]),
                     lambda b, i: (layer + 1, b * n_tiles + i, 0)) for a in cast_next] + [
        hbm] * len(cast_own)
    out_specs = [pl.BlockSpec((None, tile, D_MODEL), lambda b, i: (b, i, 0))]
    out_shape = [jax.ShapeDtypeStruct((batch, seq, D_MODEL), _F32)]
    if layer == 0:
        out_specs += [whole(pool_shape), per_batch]
        out_shape += [jax.ShapeDtypeStruct(pool_shape, _F32),
                      jax.ShapeDtypeStruct((batch, CONV_BUF, CONV_WIDTH), _F32)]
    else:
        out_specs += [whole((layer + 1,) + pool_shape),
                      pl.BlockSpec((layer + 1, None, CONV_BUF, CONV_WIDTH), lambda b, i: (0, b, 0, 0))]
        out_shape += [jax.ShapeDtypeStruct((layer + 1,) + pool_shape, _F32),
                      jax.ShapeDtypeStruct((layer + 1, batch, CONV_BUF, CONV_WIDTH), _F32)]
    out_specs += [pl.BlockSpec((a.shape[1] // steps, a.shape[2]), lambda b, i: (b * n_tiles + i, 0))
                  for a in cast_next]
    out_shape += [jax.ShapeDtypeStruct(a.shape[1:], _BF16) for a in cast_next]
    scratch_shapes = [pltpu.VMEM((POOL_HALO + tile, POOL_WIDTH), _F32),
                      pltpu.VMEM((SUBLANES + tile, CONV_WIDTH), _F32)]
    if cast_own:
        out_specs += [hbm] * len(cast_own)
        out_shape += [jax.ShapeDtypeStruct(a.shape[1:], _BF16) for a in cast_own]
        scratch_shapes += [pltpu.VMEM(a.shape[1:], _BF16) for a in cast_own] + [
            pltpu.VMEM((CAST_SLOTS,) + CAST_STAGE, _F32),
            pltpu.SemaphoreType.DMA((CAST_SLOTS,)), pltpu.SemaphoreType.DMA((len(cast_own),))]
    return pl.pallas_call(
        functools.partial(_prompt_kernel, layer=layer, tile=tile, n_tiles=n_tiles,
                          n_cast=len(cast_next), own_cast=bool(cast_own)),
        grid=(batch, n_tiles),
        in_specs=in_specs,
        out_specs=out_specs,
        out_shape=out_shape,
        scratch_shapes=scratch_shapes,
        compiler_params=pltpu.CompilerParams(
            dimension_semantics=("arbitrary", "arbitrary"),
            vmem_limit_bytes=VMEM_LIMIT_BYTES),
        name=f"prompt_layer{layer}",
    )(x, p, *weights, *prev_state, *cast_next, *cast_own)


def _sample_trunk(x, p, state_pool, state_conv, shared, w_mix, per_layer):
    n_seq = x.shape[0]
    seqs = TOKEN_TILE // SUBLANES
    tile = seqs * SUBLANES
    n_tiles = n_seq // seqs
    assert n_seq == n_tiles * seqs and x.shape[1] == SUBLANES and len(shared) == _MIX_FIELD
    assert len(per_layer) == DEPTH and all(len(big) == len(_BIG_FIELDS) for big in per_layer)
    last = n_tiles - 1

    def at(mixing, mlp):
        def index(s):
            layer, tile_index = s // (2 * n_tiles), s % n_tiles
            in_mlp_half = (s // n_tiles) % 2 == 1
            a, b = mixing(layer, tile_index), mlp(layer, tile_index)
            return tuple(jnp.where(in_mlp_half, j, i) for i, j in zip(a, b))
        return index

    hbm = pl.BlockSpec(memory_space=pl.ANY)
    big = [a for layer_big in per_layer for a in layer_big]
    state_rows = at(lambda l, i: (l, 0, i, 0), lambda l, i: (l, 0, last, 0))
    conv_rows = at(lambda l, i: (l, i, 0, 0), lambda l, i: (l, last, 0, 0))
    in_specs = [
        pl.BlockSpec((seqs, SUBLANES, D_MODEL), lambda s: (jnp.minimum(s, last), 0, 0)),
        pl.BlockSpec((None, seqs, SUBLANES, PLE_DIM),
                     at(lambda l, i: (l, 0, 0, 0), lambda l, i: (l, i, 0, 0))),
        pl.BlockSpec((None, POOL_BUF, seqs, POOL_WIDTH), state_rows),
        pl.BlockSpec((None, seqs, CONV_BUF, CONV_WIDTH), conv_rows),
    ] + [pl.BlockSpec(a.shape, lambda s, nd=a.ndim: (0,) * nd, pipeline_mode=pl.Buffered(1))
         for a in shared] + [hbm] * (1 + len(big))
    out_specs = [
        pl.BlockSpec((seqs, SUBLANES, D_MODEL),
                     at(lambda l, i: (0, 0, 0),
                        lambda l, i: (jnp.where(l == DEPTH - 1, i, 0), 0, 0))),
        pl.BlockSpec((None, POOL_BUF, seqs, POOL_WIDTH), state_rows),
        pl.BlockSpec((None, seqs, CONV_BUF, CONV_WIDTH), conv_rows),
    ]
    out_shape = [jax.ShapeDtypeStruct((n_seq, SUBLANES, D_MODEL), _F32),
                 jax.ShapeDtypeStruct((DEPTH, POOL_BUF, n_seq, POOL_WIDTH), _F32),
                 jax.ShapeDtypeStruct((DEPTH, n_seq, CONV_BUF, CONV_WIDTH), _F32)]
    w_in, w_up, w_down, w_gate = per_layer[0]
    buffers = [w_in.shape, w_mix.shape[1:], w_up.shape, w_down.shape, w_gate.shape]
    return pl.pallas_call(
        functools.partial(_sample_kernel, tile=tile, n_tiles=n_tiles),
        grid=(2 * DEPTH * n_tiles,),
        in_specs=in_specs,
        out_specs=out_specs,
        out_shape=out_shape,
        scratch_shapes=[pltpu.VMEM(shape, _BF16) for shape in buffers] + [
            pltpu.VMEM((n_tiles, tile, D_MODEL), _F32),
            pltpu.VMEM(((POOL_HALO + SUBLANES) * seqs, POOL_WIDTH), _F32),
            pltpu.VMEM(((CONV_BUF + SUBLANES) * seqs, CONV_WIDTH), _F32),
            pltpu.SemaphoreType.DMA((len(buffers),))],
        compiler_params=pltpu.CompilerParams(
            dimension_semantics=("arbitrary",),
            vmem_limit_bytes=VMEM_LIMIT_BYTES),
        name="sample_trunk",
    )(x, p, state_pool, state_conv, *shared, w_mix, *big)


def kernel(x_prompt, x_sample, state_pool, state_conv, p_prompt, p_sample, norm_mix, w_in, pool_w, pool_scale, conv_w, conv_b, w_out, norm_mlp, w_up, w_down, norm_ple, w_ple_gate, w_ple_proj, norm_f):
    assert DEPTH == 2 and PAST_LEN >= POOL_BUF
    shared = (norm_mix, jnp.swapaxes(conv_w, 0, 1), conv_b, norm_mlp, norm_ple,
              norm_f.reshape(1, D_MODEL), w_ple_proj)
    big_f32 = (w_in, w_up, w_down, w_ple_gate)

    w_mix = _fold_pool(pool_w, pool_scale, w_out)

    n_big = len(big_f32)
    h, pool_p, conv_p, *big_bf16 = _prompt_layer(0, x_prompt, p_prompt, shared + (w_mix,), (),
                                                 big_f32, big_f32)
    per_layer = [big_bf16[n_big:], big_bf16[:n_big]]
    y_prompt, pool_p, conv_p = _prompt_layer(1, h, p_prompt, _Weights(*shared, w_mix, *per_layer[1]),
                                             (pool_p, conv_p), ())

    y_sample, pool_s, conv_s = _sample_trunk(x_sample, p_sample, jnp.swapaxes(state_pool, 1, 2),
                                             state_conv, shared, w_mix, per_layer)
    return (y_prompt, y_sample, jnp.swapaxes(pool_p, 1, 2), conv_p, jnp.swapaxes(pool_s, 1, 2), conv_s)
```
